```python
import math
import jax, jax.numpy as jnp
from jax import lax
import numpy as np

D_MODEL = 1024
BATCH = 8
SEQ = 2048
DEPTH = 2

GRID_W = 64
CTX_LEN = 256
EPS = 1e-6
CONV_WIDTH = 3

GDN_HEADS = 8
GDN_DK = 128
GDN_DV = 128
GDN_CHUNK = 64
GDN_QK_DIM = GDN_HEADS * GDN_DK
GDN_V_DIM = GDN_HEADS * GDN_DV
GDN_QKV = 2 * GDN_QK_DIM + GDN_V_DIM

SSD_D_INNER = D_MODEL
SSD_HEAD_DIM = 64
SSD_HEADS = SSD_D_INNER // SSD_HEAD_DIM
SSD_GROUPS = 2
SSD_HEADS_PER_GROUP = SSD_HEADS // SSD_GROUPS
SSD_STATE = 128
SSD_CHUNK = 128
SSD_XBC = SSD_D_INNER + 2 * SSD_GROUPS * SSD_STATE

ATT_HEADS = 8
ATT_KV_HEADS = 2
ATT_REP = ATT_HEADS // ATT_KV_HEADS
ATT_HEAD_DIM = 128
ATT_BLOCK = 128
ATT_Q_DIM = ATT_HEADS * ATT_HEAD_DIM
ATT_KV_DIM = ATT_KV_HEADS * ATT_HEAD_DIM
ROPE_THETA = 10000.0

N_BRANCHES = 3
D_FF = 4 * D_MODEL

SPLIT_SIZES = (GDN_QKV, GDN_V_DIM, 2 * GDN_HEADS, 2 * GDN_HEADS,
               SSD_D_INNER, SSD_XBC, 2 * SSD_HEADS,
               ATT_Q_DIM, 2 * ATT_KV_DIM,
               N_BRANCHES * D_MODEL)
D_IN = sum(SPLIT_SIZES)

kernel_name = 'hybrid_gdn_ssd_gqa_prefix_dit'


def rms_norm(x, w):
    xf = x.astype(jnp.float32)
    y = xf * lax.rsqrt(jnp.mean(xf * xf, axis=-1, keepdims=True) + EPS)
    return (y * w.astype(jnp.float32)).astype(x.dtype)


def l2_norm(x):
    xf = x.astype(jnp.float32)
    return xf * lax.rsqrt(jnp.sum(xf * xf, axis=-1, keepdims=True) + EPS)


def flip(t):
    return jnp.flip(t, axis=1)


def split_in_proj(p):
    out, start = [], 0
    for size in SPLIT_SIZES:
        out.append(p[..., start:start + size])
        start += size
    return out


def depthwise_conv(x, w, b=None):
    k_w, ch = w.shape
    y = lax.conv_general_dilated(x, w[:, None, :].astype(x.dtype), window_strides=(1,),
                                 padding=[(k_w // 2, k_w // 2)],
                                 dimension_numbers=('NWC', 'WIO', 'NWC'),
                                 feature_group_count=ch)
    if b is not None:
        y = y + b.astype(x.dtype)
    return y


def rope_axis(t, pos):
    h = t.shape[-1] // 2
    freqs = ROPE_THETA ** (-jnp.arange(h, dtype=jnp.float32) / h)
    ang = pos.astype(jnp.float32)[:, None] * freqs[None, :]
    cos = jnp.cos(ang)[None, :, None, :]
    sin = jnp.sin(ang)[None, :, None, :]
    tf = t.astype(jnp.float32)
    t1, t2 = tf[..., :h], tf[..., h:]
    return jnp.concatenate([t1 * cos - t2 * sin, t1 * sin + t2 * cos], axis=-1)


def rope_2d(x, rows, cols):
    half = x.shape[-1] // 2
    return jnp.concatenate([rope_axis(x[..., :half], rows),
                            rope_axis(x[..., half:], cols)], axis=-1).astype(x.dtype)


def gdn_chunked(q, k, v, log_a, beta, s0):
    bsz, seq_len, n_h, dk = q.shape
    dv = v.shape[-1]
    n_c = seq_len // GDN_CHUNK

    def chunks(t):
        t = t.reshape((bsz, n_c, GDN_CHUNK, n_h) + t.shape[3:])
        return jnp.swapaxes(t, 2, 3)

    q = chunks(q * dk ** -0.5)
    k = chunks(k)
    v = chunks(v)
    beta = chunks(beta)
    g = jnp.cumsum(chunks(log_a), axis=-1)
    idx = jnp.arange(GDN_CHUNK)
    incl = idx[:, None] >= idx[None, :]
    strict = idx[:, None] > idx[None, :]
    decay = jnp.exp(jnp.where(incl, g[..., :, None] - g[..., None, :], -jnp.inf))
    k_beta = k * beta[..., None]
    a_mat = jnp.where(strict, jnp.einsum('bnhid,bnhjd->bnhij', k_beta, k) * decay, 0.0)
    eye = jnp.eye(GDN_CHUNK, dtype=jnp.float32)
    rhs = jnp.concatenate([v * beta[..., None], k_beta * jnp.exp(g)[..., None]], axis=-1)
    sol = lax.linalg.triangular_solve(a_mat + eye, rhs, left_side=True, lower=True)
    u, w = sol[..., :dv], sol[..., dv:]
    qk = jnp.where(incl, jnp.einsum('bnhid,bnhjd->bnhij', q, k) * decay, 0.0)
    q_dec = q * jnp.exp(g)[..., None]
    k_dec = k * jnp.exp(g[..., -1:] - g)[..., None]
    chunk_decay = jnp.exp(g[..., -1])

    def step(state, xs):
        qk_c, u_c, w_c, qd_c, kd_c, cd_c = xs
        v_new = u_c - jnp.einsum('bhcd,bhde->bhce', w_c, state)
        o_c = jnp.einsum('bhcd,bhde->bhce', qd_c, state) + jnp.einsum('bhij,bhje->bhie', qk_c, v_new)
        state = state * cd_c[..., None, None] + jnp.einsum('bhcd,bhce->bhde', kd_c, v_new)
        return state, o_c

    xs = tuple(jnp.moveaxis(t, 1, 0) for t in (qk, u, w, q_dec, k_dec, chunk_decay))
    s_final, o = lax.scan(step, s0, xs)
    o = jnp.transpose(o, (1, 0, 3, 2, 4)).reshape(bsz, seq_len, n_h, dv)
    return o, s_final


def gdn_branch(qkv, z, beta_raw, a_raw, conv_w, a_log, dt_bias, norm_w, s0_f, s0_b):
    bsz, seq_len, _ = qkv.shape
    qkv = jax.nn.silu(depthwise_conv(qkv, conv_w))
    q = l2_norm(qkv[..., :GDN_QK_DIM].reshape(bsz, seq_len, GDN_HEADS, GDN_DK))
    k = l2_norm(qkv[..., GDN_QK_DIM:2 * GDN_QK_DIM].reshape(bsz, seq_len, GDN_HEADS, GDN_DK))
    v = qkv[..., 2 * GDN_QK_DIM:].reshape(bsz, seq_len, GDN_HEADS, GDN_DV).astype(jnp.float32)
    beta = jax.nn.sigmoid(beta_raw.astype(jnp.float32)).reshape(bsz, seq_len, 2, GDN_HEADS)
    log_a = -jnp.exp(a_log.astype(jnp.float32)) * jax.nn.softplus(
        a_raw.astype(jnp.float32).reshape(bsz, seq_len, 2, GDN_HEADS) + dt_bias.astype(jnp.float32))
    o_f, s_f = gdn_chunked(q, k, v, log_a[:, :, 0], beta[:, :, 0], s0_f)
    o_b, s_b = gdn_chunked(flip(q), flip(k), flip(v), flip(log_a[:, :, 1]), flip(beta[:, :, 1]), s0_b)
    o = o_f + flip(o_b)
    o = rms_norm(o, norm_w) * jax.nn.silu(z.astype(jnp.float32).reshape(bsz, seq_len, GDN_HEADS, GDN_DV))
    return o.reshape(bsz, seq_len, GDN_V_DIM).astype(qkv.dtype), s_f, s_b


def ssd_chunked(x, dt, a, bm, cm, s0):
    bsz, seq_len = x.shape[:2]
    n_c = seq_len // SSD_CHUNK

    def chunks(t):
        return t.reshape((bsz, n_c, SSD_CHUNK) + t.shape[2:])

    xdt = chunks(x * dt[..., None])
    bm = chunks(bm)
    cm = chunks(cm)
    a_cs = jnp.moveaxis(jnp.cumsum(chunks(dt * a), axis=2), 2, -1)
    idx = jnp.arange(SSD_CHUNK)
    incl = idx[:, None] >= idx[None, :]
    l_mat = jnp.exp(jnp.where(incl, a_cs[..., :, None] - a_cs[..., None, :], -jnp.inf))
    cb = jnp.einsum('bcign,bcjgn->bcgij', cm, bm)
    y_diag = jnp.einsum('bcgij,bcgeij,bcjgep->bcigep', cb, l_mat, xdt)
    decay_to_end = jnp.exp(a_cs[..., -1:] - a_cs)
    states = jnp.einsum('bcjgn,bcgej,bcjgep->bcgepn', bm, decay_to_end, xdt)
    chunk_decay = jnp.exp(a_cs[..., -1])

    def step(state, xs):
        st, cd = xs
        return state * cd[..., None, None] + st, state

    s_final, s_prev = lax.scan(step, s0, (jnp.moveaxis(states, 1, 0), jnp.moveaxis(chunk_decay, 1, 0)))
    s_prev = jnp.moveaxis(s_prev, 0, 1)
    y_off = jnp.einsum('bcign,bcgepn,bcgei->bcigep', cm, s_prev, jnp.exp(a_cs))
    return (y_diag + y_off).reshape(x.shape), s_final


def ssd_branch(z, xbc, dt_raw, conv_w, conv_b, a_log, dt_bias, d_skip, norm_w, s0_f, s0_b):
    bsz, seq_len, _ = xbc.shape
    gn = SSD_GROUPS * SSD_STATE
    xbc = jax.nn.silu(depthwise_conv(xbc, conv_w, conv_b)).astype(jnp.float32)
    xs = xbc[..., :SSD_D_INNER].reshape(bsz, seq_len, SSD_GROUPS, SSD_HEADS_PER_GROUP, SSD_HEAD_DIM)
    bm = xbc[..., SSD_D_INNER:SSD_D_INNER + gn].reshape(bsz, seq_len, SSD_GROUPS, SSD_STATE)
    cm = xbc[..., SSD_D_INNER + gn:].reshape(bsz, seq_len, SSD_GROUPS, SSD_STATE)
    dt = jax.nn.softplus(
        dt_raw.astype(jnp.float32).reshape(bsz, seq_len, 2, SSD_GROUPS, SSD_HEADS_PER_GROUP)
        + dt_bias.astype(jnp.float32).reshape(2, SSD_GROUPS, SSD_HEADS_PER_GROUP))
    a = -jnp.exp(a_log.astype(jnp.float32)).reshape(2, SSD_GROUPS, SSD_HEADS_PER_GROUP)
    y_f, s_f = ssd_chunked(xs, dt[:, :, 0], a[0], bm, cm, s0_f)
    y_b, s_b = ssd_chunked(flip(xs), flip(dt[:, :, 1]), a[1], flip(bm), flip(cm), s0_b)
    y = y_f + flip(y_b) + d_skip.astype(jnp.float32).reshape(SSD_GROUPS, SSD_HEADS_PER_GROUP, 1) * xs
    y = y.reshape(bsz, seq_len, SSD_D_INNER) * jax.nn.silu(z.astype(jnp.float32))
    y = rms_norm(y.reshape(bsz, seq_len, SSD_GROUPS, SSD_D_INNER // SSD_GROUPS),
                 norm_w.reshape(SSD_GROUPS, SSD_D_INNER // SSD_GROUPS))
    return y.reshape(bsz, seq_len, SSD_D_INNER).astype(z.dtype), s_f, s_b


def att_q(q_raw, q_norm):
    bsz, seq_len, _ = q_raw.shape
    return rms_norm(q_raw.reshape(bsz, seq_len, ATT_HEADS, ATT_HEAD_DIM), q_norm)


def att_kv(kv_raw, k_norm):
    bsz, seq_len, _ = kv_raw.shape
    kv = kv_raw.reshape(bsz, seq_len, 2, ATT_KV_HEADS, ATT_HEAD_DIM)
    return rms_norm(kv[:, :, 0], k_norm), kv[:, :, 1]


def gqa_blocks(q, k, v):
    bsz, lq = q.shape[:2]
    n_b = lq // ATT_BLOCK
    q = q.reshape(bsz, n_b, ATT_BLOCK, ATT_KV_HEADS, ATT_REP, ATT_HEAD_DIM)
    qb = jnp.moveaxis(q, 1, 0)
    scale = ATT_HEAD_DIM ** -0.5

    def one_block(q_blk):
        s = jnp.einsum('bqgrd,bkgd->bgrqk', q_blk, k).astype(jnp.float32) * scale
        p = jax.nn.softmax(s, axis=-1).astype(v.dtype)
        return jnp.einsum('bgrqk,bkgd->bqgrd', p, v)

    o = lax.map(one_block, qb)
    return jnp.moveaxis(o, 0, 1).reshape(bsz, lq, ATT_Q_DIM)


def merge_branches(y_gdn, y_ssd, y_att, gates, w_br_gdn, w_br_ssd, w_br_att, w_out):
    g = jax.nn.sigmoid(gates.astype(jnp.float32)).astype(y_gdn.dtype)
    g_gdn, g_ssd, g_att = jnp.split(g, N_BRANCHES, axis=-1)
    m = g_gdn * (y_gdn @ w_br_gdn) + g_ssd * (y_ssd @ w_br_ssd) + g_att * (y_att @ w_br_att)
    return m @ w_out


def mixing_sublayer(h, hc, rows, cols, with_ctx_out, w_in, gdn_conv, gdn_a_log, gdn_dt_bias, gdn_norm,
                    ssd_conv_w, ssd_conv_b, ssd_a_log, ssd_dt_bias, ssd_d, ssd_norm,
                    att_q_norm, att_k_norm, w_br_gdn, w_br_ssd, w_br_att, w_out):
    bsz, seq_len, _ = h.shape
    p_lat = split_in_proj(h @ w_in)
    p_ctx = split_in_proj(hc @ w_in)
    gdn_p = (gdn_conv, gdn_a_log, gdn_dt_bias, gdn_norm)
    ssd_p = (ssd_conv_w, ssd_conv_b, ssd_a_log, ssd_dt_bias, ssd_d, ssd_norm)
    zeros_gdn = jnp.zeros((bsz, GDN_HEADS, GDN_DK, GDN_DV), jnp.float32)
    zeros_ssd = jnp.zeros((bsz, SSD_GROUPS, SSD_HEADS_PER_GROUP, SSD_HEAD_DIM, SSD_STATE), jnp.float32)

    gdn_c, sg_f, sg_b = gdn_branch(*p_ctx[0:4], *gdn_p, zeros_gdn, zeros_gdn)
    ssd_c, ss_f, ss_b = ssd_branch(*p_ctx[4:7], *ssd_p, zeros_ssd, zeros_ssd)
    k_c, v_c = att_kv(p_ctx[8], att_k_norm)

    gdn_l, _, _ = gdn_branch(*p_lat[0:4], *gdn_p, sg_f, sg_b)
    ssd_l, _, _ = ssd_branch(*p_lat[4:7], *ssd_p, ss_f, ss_b)
    q_l = rope_2d(att_q(p_lat[7], att_q_norm), rows, cols)
    k_l, v_l = att_kv(p_lat[8], att_k_norm)
    k_l = rope_2d(k_l, rows, cols)
    att_l = gqa_blocks(q_l, jnp.concatenate([k_c, k_l], axis=1), jnp.concatenate([v_c, v_l], axis=1))
    y_lat = merge_branches(gdn_l, ssd_l, att_l, p_lat[9], w_br_gdn, w_br_ssd, w_br_att, w_out)

    y_ctx = None
    if with_ctx_out:
        att_c = gqa_blocks(att_q(p_ctx[7], att_q_norm), k_c, v_c)
        y_ctx = merge_branches(gdn_c, ssd_c, att_c, p_ctx[9], w_br_gdn, w_br_ssd, w_br_att, w_out)
    return y_lat, y_ctx


def sq_relu_mlp(h, w1, w2):
    return jnp.square(jax.nn.relu(h @ w1)) @ w2


def setup_inputs(seed: int = 0) -> dict:
    key = jax.random.key(seed)
    k = jax.random.split(key, 27)
    f32 = jnp.float32

    def nrm(i, shape, scale):
        return scale * jax.random.normal(k[i], shape, f32)

    def gain(i, shape):
        return 1.0 + 0.02 * jax.random.normal(k[i], shape, f32)

    def a_log(i, n):
        return jnp.log(jax.random.uniform(k[i], (DEPTH, 2, n), f32, 1.0, 16.0))

    def dt_bias(i, n):
        dt = jnp.exp(jax.random.uniform(k[i], (DEPTH, 2, n), f32, math.log(1e-3), math.log(1e-1)))
        return dt + jnp.log(-jnp.expm1(-dt))

    return {
        'x': nrm(0, (BATCH, SEQ, D_MODEL), 1.0),
        'c': nrm(1, (BATCH, D_MODEL), 1.0),
        'ctx': nrm(2, (BATCH, CTX_LEN, D_MODEL), 1.0),
        'c_ctx': nrm(3, (D_MODEL,), 1.0),
        'w_mod': nrm(4, (DEPTH, D_MODEL, 6 * D_MODEL), 0.5 * D_MODEL ** -0.5),
        'b_mod': nrm(5, (DEPTH, 6 * D_MODEL), 0.02),
        'norm_mix': gain(6, (DEPTH, D_MODEL)),
        'norm_mlp': gain(7, (DEPTH, D_MODEL)),
        'w_in': nrm(8, (DEPTH, D_MODEL, D_IN), D_MODEL ** -0.5),
        'gdn_conv': nrm(9, (DEPTH, CONV_WIDTH, GDN_QKV), CONV_WIDTH ** -0.5),
        'gdn_a_log': a_log(10, GDN_HEADS),
        'gdn_dt_bias': dt_bias(11, GDN_HEADS),
        'gdn_norm': gain(12, (DEPTH, GDN_DV)),
        'ssd_conv_w': nrm(13, (DEPTH, CONV_WIDTH, SSD_XBC), CONV_WIDTH ** -0.5),
        'ssd_conv_b': nrm(14, (DEPTH, SSD_XBC), 0.02),
        'ssd_a_log': a_log(15, SSD_HEADS),
        'ssd_dt_bias': dt_bias(16, SSD_HEADS),
        'ssd_d': gain(17, (DEPTH, SSD_HEADS)),
        'ssd_norm': gain(18, (DEPTH, SSD_D_INNER)),
        'att_q_norm': gain(19, (DEPTH, ATT_HEAD_DIM)),
        'att_k_norm': gain(20, (DEPTH, ATT_HEAD_DIM)),
        'w_br_gdn': nrm(21, (DEPTH, GDN_V_DIM, D_MODEL), GDN_V_DIM ** -0.5),
        'w_br_ssd': nrm(22, (DEPTH, SSD_D_INNER, D_MODEL), SSD_D_INNER ** -0.5),
        'w_br_att': nrm(23, (DEPTH, ATT_Q_DIM, D_MODEL), ATT_Q_DIM ** -0.5),
        'w_out': nrm(24, (DEPTH, D_MODEL, D_MODEL), D_MODEL ** -0.5),
        'w_ff1': nrm(25, (DEPTH, D_MODEL, D_FF), D_MODEL ** -0.5),
        'w_ff2': nrm(26, (DEPTH, D_FF, D_MODEL), D_FF ** -0.5),
    }


def reference(x, c, ctx, c_ctx, w_mod, b_mod, norm_mix, norm_mlp, w_in, gdn_conv, gdn_a_log, gdn_dt_bias,
              gdn_norm, ssd_conv_w, ssd_conv_b, ssd_a_log, ssd_dt_bias, ssd_d, ssd_norm, att_q_norm, att_k_norm,
              w_br_gdn, w_br_ssd, w_br_att, w_out, w_ff1, w_ff2):
    seq_len = x.shape[1]
    rows_n = seq_len // GRID_W
    rows = jnp.broadcast_to(jnp.arange(rows_n, dtype=jnp.int32)[:, None], (rows_n, GRID_W)).reshape(-1)
    cols = jnp.broadcast_to(jnp.arange(GRID_W, dtype=jnp.int32)[None, :], (rows_n, GRID_W)).reshape(-1)
    xc = ctx
    for l in range(DEPTH):
        last = l == DEPTH - 1
        sh1, sc1, g1, sh2, sc2, g2 = jnp.split((jax.nn.silu(c) @ w_mod[l] + b_mod[l])[:, None, :], 6, axis=-1)
        csh1, csc1, cg1, csh2, csc2, cg2 = jnp.split(jax.nn.silu(c_ctx) @ w_mod[l] + b_mod[l], 6, axis=-1)

        h = rms_norm(x, norm_mix[l]) * (1.0 + sc1) + sh1
        hc = rms_norm(xc, norm_mix[l]) * (1.0 + csc1) + csh1
        y, yc = mixing_sublayer(h, hc, rows, cols, not last, w_in[l], gdn_conv[l], gdn_a_log[l], gdn_dt_bias[l],
                                gdn_norm[l], ssd_conv_w[l], ssd_conv_b[l], ssd_a_log[l], ssd_dt_bias[l],
                                ssd_d[l], ssd_norm[l], att_q_norm[l], att_k_norm[l],
                                w_br_gdn[l], w_br_ssd[l], w_br_att[l], w_out[l])
        x = x + g1 * y
        h = rms_norm(x, norm_mlp[l]) * (1.0 + sc2) + sh2
        x = x + g2 * sq_relu_mlp(h, w_ff1[l], w_ff2[l])

        if not last:
            xc = xc + cg1 * yc
            hc = rms_norm(xc, norm_mlp[l]) * (1.0 + csc2) + csh2
            xc = xc + cg2 * sq_relu_mlp(hc, w_ff1[l], w_ff2[l])
    return x
```

```python
import functools

import jax
import jax.numpy as jnp
from jax import lax
from jax.experimental import pallas as pl
from jax.experimental.pallas import tpu as pltpu

F32 = jnp.float32
BF16 = jnp.bfloat16

D_MODEL = 1024
GRID_W = 64
EPS = 1e-6

GDN_HEADS = 8
GDN_DK = 128
GDN_DV = 128
GDN_CHUNK = 64
GDN_QK_DIM = GDN_HEADS * GDN_DK
GDN_V_DIM = GDN_HEADS * GDN_DV
GDN_QKV = 2 * GDN_QK_DIM + GDN_V_DIM

SSD_D_INNER = D_MODEL
SSD_HEAD_DIM = 64
SSD_HEADS = SSD_D_INNER // SSD_HEAD_DIM
SSD_GROUPS = 2
SSD_HPG = SSD_HEADS // SSD_GROUPS
SSD_STATE = 128
SSD_CHUNK = 128
SSD_XBC = SSD_D_INNER + 2 * SSD_GROUPS * SSD_STATE

ATT_HEADS = 8
ATT_KV_HEADS = 2
ATT_REP = ATT_HEADS // ATT_KV_HEADS
ATT_HEAD_DIM = 128
ATT_Q_DIM = ATT_HEADS * ATT_HEAD_DIM
ATT_KV_DIM = ATT_KV_HEADS * ATT_HEAD_DIM
ROPE_THETA = 10000.0
D_FF = 4 * D_MODEL

LANES = 128
MOD_ROWS = 16

C_QKV = 0
C_GZ = C_QKV + GDN_QKV
C_SZ = C_GZ + GDN_V_DIM
C_AQ = C_SZ + SSD_D_INNER
C_GATE = C_AQ + ATT_Q_DIM
C_XBC = C_GATE + 3 * D_MODEL
C_AKV = C_XBC + SSD_XBC
C_SMALL = C_AKV + 2 * ATT_KV_DIM
NP_COLS = 11520
SM_BETA = 0
SM_A = 2 * GDN_HEADS
SM_DT = 4 * GDN_HEADS

VMEM_LIMIT = 56 * 1024 * 1024


def _cparams(sem):
    return pltpu.CompilerParams(dimension_semantics=sem, vmem_limit_bytes=VMEM_LIMIT)


def _mm(a, b):
    return jnp.dot(a.astype(BF16), b.astype(BF16), preferred_element_type=F32)


def _mm_nt(a, b):
    return lax.dot_general(a.astype(BF16), b.astype(BF16), (((1,), (1,)), ((), ())),
                           preferred_element_type=F32)


def _mm_tn(a, b):
    return lax.dot_general(a.astype(BF16), b.astype(BF16), (((0,), (0,)), ((), ())),
                           preferred_element_type=F32)


def _split3(x):
    hi = x.astype(BF16)
    r = x - hi.astype(F32)
    mid = r.astype(BF16)
    lo = (r - mid.astype(F32)).astype(BF16)
    return hi, mid, lo


def _mm_sel_l(sel, x):
    hi, mid, lo = _split3(x)
    d = lambda p: jnp.dot(sel, p, preferred_element_type=F32)
    return (d(hi) + d(mid)) + d(lo)


def _mm_sel_r(x, sel):
    hi, mid, lo = _split3(x)
    d = lambda p: jnp.dot(p, sel, preferred_element_type=F32)
    return (d(hi) + d(mid)) + d(lo)


def _mm_x3(a, b):
    ah = a.astype(BF16)
    al = (a - ah.astype(F32)).astype(BF16)
    bh = b.astype(BF16)
    bl = (b - bh.astype(F32)).astype(BF16)
    d = lambda p, q: jnp.dot(p, q, preferred_element_type=F32)
    return d(ah, bh) + (d(ah, bl) + d(al, bh))


def _sigmoid(x):
    return 1.0 / (1.0 + jnp.exp(-x))


def _silu(x):
    return x * _sigmoid(x)


def _softplus(x):
    return jnp.maximum(x, 0.0) + jnp.log(1.0 + jnp.exp(-jnp.abs(x)))


def _rms(x, w):
    return x * lax.rsqrt(jnp.mean(x * x, axis=-1, keepdims=True) + EPS) * w


def _norm_mod(x, nw, scale, shift):
    return _rms(x, nw) * (1.0 + scale) + shift


def _tri(n, lower):
    i = lax.broadcasted_iota(jnp.int32, (n, n), 0)
    j = lax.broadcasted_iota(jnp.int32, (n, n), 1)
    return (i >= j) if lower else (i <= j)


def _rope(x, cos, sin):
    lane = lax.broadcasted_iota(jnp.int32, x.shape, 1)
    q = LANES // 4
    swapped = jnp.where((lane & q) == 0, pltpu.roll(x, LANES - q, 1), pltpu.roll(x, q, 1))
    return x * cos + swapped * sin


def _mod_kernel(c_ref, w_ref, b_ref, o_ref):
    o_ref[0] = _mm(_silu(c_ref[...]), w_ref[0]) + b_ref[0]


def _modulation(cc, w_mod, b_mod):
    depth = w_mod.shape[0]
    n = w_mod.shape[2]
    tn = D_MODEL
    return pl.pallas_call(
        _mod_kernel,
        grid=(depth, n // tn),
        in_specs=[pl.BlockSpec((MOD_ROWS, D_MODEL), lambda l, j: (0, 0)),
                  pl.BlockSpec((1, D_MODEL, tn), lambda l, j: (l, 0, j)),
                  pl.BlockSpec((1, 1, tn), lambda l, j: (l, 0, j))],
        out_specs=pl.BlockSpec((1, MOD_ROWS, tn), lambda l, j: (l, 0, j)),
        out_shape=jax.ShapeDtypeStruct((depth, MOD_ROWS, n), F32),
        compiler_params=_cparams(("arbitrary", "arbitrary")),
        name="modulation",
    )(cc, w_mod, b_mod.reshape(depth, 1, n))


def _inproj_kernel(x_ref, mod_ref, nw_ref, w_ref, o_ref, h_ref):
    @pl.when(pl.program_id(2) == 0)
    def _():
        m = mod_ref[0]
        h_ref[...] = _norm_mod(x_ref[0], nw_ref[...], m[1:2], m[0:1]).astype(BF16)

    o_ref[0] = jnp.dot(h_ref[...], w_ref[...], preferred_element_type=F32)


def _in_projection(x, mod_l, mod_row, nw, w_r, tm):
    bsz, seq, d = x.shape
    tn = NP_COLS // 6
    return pl.pallas_call(
        _inproj_kernel,
        grid=(bsz, seq // tm, NP_COLS // tn),
        in_specs=[pl.BlockSpec((1, tm, d), lambda b, i, j: (b, i, 0)),
                  pl.BlockSpec((1, 6, d), lambda b, i, j: (mod_row(b), 0, 0)),
                  pl.BlockSpec((1, d), lambda b, i, j: (0, 0)),
                  pl.BlockSpec((d, tn), lambda b, i, j: (0, j))],
        out_specs=pl.BlockSpec((1, tm, tn), lambda b, i, j: (b, i, j)),
        out_shape=jax.ShapeDtypeStruct((bsz, seq, NP_COLS), F32),
        scratch_shapes=[pltpu.VMEM((tm, d), BF16)],
        compiler_params=_cparams(("arbitrary", "arbitrary", "arbitrary")),
        name="in_projection",
    )(x, mod_l, nw.reshape(1, d), w_r)


def _conv3(x, prev_row, next_row, w):
    ts = x.shape[0]
    rid = lax.broadcasted_iota(jnp.int32, x.shape, 0)
    xm1 = jnp.where(rid == 0, prev_row, pltpu.roll(x, 1, 0))
    xp1 = jnp.where(rid == ts - 1, next_row, pltpu.roll(x, ts - 1, 0))
    return w[0:1] * xm1 + w[1:2] * x + w[2:3] * xp1


def _conv_halo(xp_ref, xn_ref):
    i = pl.program_id(1)
    prev_row = jnp.where(i == 0, 0.0, xp_ref[0, 7:8, :])
    next_row = jnp.where(i == pl.num_programs(1) - 1, 0.0, xn_ref[0, 0:1, :])
    return prev_row, next_row


def _gdn_conv_kernel(x_ref, xp_ref, xn_ref, w_ref, o_ref):
    prev_row, next_row = _conv_halo(xp_ref, xn_ref)
    y = _silu(_conv3(x_ref[0], prev_row, next_row, w_ref[...]))
    j = pl.program_id(2)
    parts = []
    for h in range(GDN_HEADS):
        yh = y[:, h * GDN_DK:(h + 1) * GDN_DK]
        parts.append(yh * lax.rsqrt(jnp.sum(yh * yh, axis=-1, keepdims=True) + EPS))
    yn = jnp.concatenate(parts, axis=-1)
    yn = yn * jnp.where(j == 0, GDN_DK ** -0.5, 1.0)
    o_ref[0] = jnp.where(j < 2, yn, y)


def _ssd_conv_kernel(x_ref, xp_ref, xn_ref, w_ref, b_ref, o_ref):
    prev_row, next_row = _conv_halo(xp_ref, xn_ref)
    o_ref[0] = _silu(_conv3(x_ref[0], prev_row, next_row, w_ref[...]) + b_ref[...])


def _short_conv(p, col0, width, tc, ts, conv_w, conv_b):
    bsz, seq, _ = p.shape
    cb0 = col0 // tc
    nrb = seq // 8
    rpb = ts // 8
    x_specs = [pl.BlockSpec((1, ts, tc), lambda b, i, j: (b, i, cb0 + j)),
               pl.BlockSpec((1, 8, tc), lambda b, i, j: (b, jnp.maximum(i * rpb - 1, 0), cb0 + j)),
               pl.BlockSpec((1, 8, tc), lambda b, i, j: (b, jnp.minimum((i + 1) * rpb, nrb - 1), cb0 + j)),
               pl.BlockSpec((3, tc), lambda b, i, j: (0, j))]
    args = [p, p, p, conv_w]
    if conv_b is None:
        body = _gdn_conv_kernel
    else:
        body = _ssd_conv_kernel
        x_specs.append(pl.BlockSpec((1, tc), lambda b, i, j: (0, j)))
        args.append(conv_b.reshape(1, width))
    return pl.pallas_call(
        body,
        grid=(bsz, seq // ts, width // tc),
        in_specs=x_specs,
        out_specs=pl.BlockSpec((1, ts, tc), lambda b, i, j: (b, i, j)),
        out_shape=jax.ShapeDtypeStruct((bsz, seq, width), F32),
        compiler_params=_cparams(("arbitrary", "arbitrary", "arbitrary")),
        name="short_conv",
    )(*args)


def _unit_tri_inverse(a):
    n = a.shape[0]
    eye = (lax.broadcasted_iota(jnp.int32, (n, n), 0) == lax.broadcasted_iota(jnp.int32, (n, n), 1)).astype(F32)
    inv = eye - a
    power = a
    steps = n.bit_length() - 2
    for _ in range(steps):
        power = _mm_x3(power, power)
        inv = inv + _mm_x3(inv, power)
    return inv


def _gdn_chunk(q, k, v, beta, gc, gr, state, fwd):
    c = q.shape[0]
    incl = _tri(c, fwd)
    ii = lax.broadcasted_iota(jnp.int32, (c, c), 0)
    jj = lax.broadcasted_iota(jnp.int32, (c, c), 1)
    strict = (ii > jj) if fwd else (ii < jj)
    decay = jnp.exp(jnp.where(incl, gc - gr, -jnp.inf))
    kb = k * beta
    a = jnp.where(strict, _mm_nt(kb, k) * decay, 0.0)
    inv = _unit_tri_inverse(a)
    eg = jnp.exp(gc)
    sol = _mm_x3(inv, jnp.concatenate([v * beta, kb * eg], axis=-1))
    u = sol[:, :GDN_DV]
    w = sol[:, GDN_DV:]
    qk = jnp.where(incl, _mm_nt(q, k) * decay, 0.0)
    g_last = gc[c - 1:c] if fwd else gc[0:1]
    q_dec = q * eg
    k_dec = k * jnp.exp(g_last - gc)
    v_new = u - _mm(w, state)
    o = _mm(q_dec, state) + _mm(qk, v_new)
    new_state = state * jnp.exp(g_last) + _mm_tn(k_dec, v_new)
    return o, new_state


def _gdn_gates(s_ref, r_ref, alog_l, dtb_l, alog_s, dtb_s, fwd):
    c = GDN_CHUNK
    raw = s_ref[0]
    beta = _sigmoid(raw)
    la = -jnp.exp(alog_l) * _softplus(raw + dtb_l)
    rraw = r_ref[0, 0]
    la_r = -jnp.exp(alog_s) * _softplus(rraw + dtb_s)
    low = _tri(c, True).astype(BF16)
    up = _tri(c, False).astype(BF16)
    if fwd:
        return beta, _mm_sel_l(low, la), _mm_sel_r(la_r, up)
    return beta, _mm_sel_l(up, la), _mm_sel_r(la_r, low)


def _gdn_kernel(qf_ref, kf_ref, vf_ref, sf_ref, rf_ref, qb_ref, kb_ref, vb_ref, sb_ref, rb_ref,
                alog_l_ref, dtb_l_ref, alog_s_ref, dtb_s_ref, s0f_ref, s0b_ref,
                of_ref, ob_ref, stf_ref, stb_ref):
    @pl.when(pl.program_id(1) == 0)
    def _():
        stf_ref[...] = s0f_ref[...]
        stb_ref[...] = s0b_ref[...]

    alog_l, dtb_l, alog_s, dtb_s = alog_l_ref[...], dtb_l_ref[...], alog_s_ref[...], dtb_s_ref[...]
    for d, (q_ref, k_ref, v_ref, s_ref, r_ref, o_ref, st_ref) in enumerate(
            ((qf_ref, kf_ref, vf_ref, sf_ref, rf_ref, of_ref, stf_ref),
             (qb_ref, kb_ref, vb_ref, sb_ref, rb_ref, ob_ref, stb_ref))):
        fwd = d == 0
        beta_all, gc_all, gr_all = _gdn_gates(s_ref, r_ref, alog_l, dtb_l, alog_s, dtb_s, fwd)
        for h in range(GDN_HEADS):
            lb = SM_BETA + d * GDN_HEADS + h
            la = SM_A + d * GDN_HEADS + h
            sl = slice(h * GDN_DK, (h + 1) * GDN_DK)
            o, st = _gdn_chunk(q_ref[0, :, sl], k_ref[0, :, sl], v_ref[0, :, sl],
                               beta_all[:, lb:lb + 1], gc_all[:, la:la + 1], gr_all[la:la + 1, :],
                               st_ref[0, h], fwd)
            o_ref[0, :, sl] = o
            st_ref[0, h] = st


def _lane_param(vals, offset):
    flat = vals.reshape(-1).astype(F32)
    v = jnp.zeros((LANES,), F32).at[offset:offset + flat.shape[0]].set(flat)
    return v.reshape(1, LANES), v.reshape(LANES, 1)


def _small_rows(p, chunk):
    bsz, seq, _ = p.shape
    small = p[:, :, C_SMALL:C_SMALL + LANES]
    return jnp.swapaxes(small.reshape(bsz, seq // chunk, chunk, LANES), 2, 3)


def _gdn_scan(qkv, p, a_log, dt_bias, s0f, s0b):
    bsz, seq, _ = qkv.shape
    c = GDN_CHUNK
    nc = seq // c
    rows = _small_rows(p, c)
    alog_l, alog_s = _lane_param(a_log, SM_A)
    dtb_l, dtb_s = _lane_param(dt_bias, SM_A)
    hw = GDN_QK_DIM
    sm_blk = C_SMALL // LANES

    def chunk_specs(cmap):
        return [pl.BlockSpec((1, c, hw), lambda b, i: (b, cmap(i), 0)),
                pl.BlockSpec((1, c, hw), lambda b, i: (b, cmap(i), 1)),
                pl.BlockSpec((1, c, hw), lambda b, i: (b, cmap(i), 2)),
                pl.BlockSpec((1, c, LANES), lambda b, i: (b, cmap(i), sm_blk)),
                pl.BlockSpec((1, 1, LANES, c), lambda b, i: (b, cmap(i), 0, 0))]

    fw = lambda i: i
    bw = lambda i: nc - 1 - i
    vec_l = pl.BlockSpec((1, LANES), lambda b, i: (0, 0))
    vec_s = pl.BlockSpec((LANES, 1), lambda b, i: (0, 0))
    st_spec = pl.BlockSpec((1, GDN_HEADS, GDN_DK, GDN_DV), lambda b, i: (b, 0, 0, 0))
    out_shape = [jax.ShapeDtypeStruct((bsz, seq, GDN_V_DIM), F32)] * 2 + \
                [jax.ShapeDtypeStruct((bsz, GDN_HEADS, GDN_DK, GDN_DV), F32)] * 2
    return pl.pallas_call(
        _gdn_kernel,
        grid=(bsz, nc),
        in_specs=chunk_specs(fw) + chunk_specs(bw) + [vec_l, vec_l, vec_s, vec_s, st_spec, st_spec],
        out_specs=[pl.BlockSpec((1, c, GDN_V_DIM), lambda b, i: (b, i, 0)),
                   pl.BlockSpec((1, c, GDN_V_DIM), lambda b, i: (b, nc - 1 - i, 0)),
                   st_spec, st_spec],
        out_shape=out_shape,
        compiler_params=_cparams(("arbitrary", "arbitrary")),
        name="gdn_scan",
    )(qkv, qkv, qkv, p, rows, qkv, qkv, qkv, p, rows, alog_l, dtb_l, alog_s, dtb_s, s0f, s0b)


def _ssd_gates(s_ref, r_ref, alog_l, dtb_l, alog_s, dtb_s, fwd):
    c = SSD_CHUNK
    dt = _softplus(s_ref[0] + dtb_l)
    da = dt * (-jnp.exp(alog_l))
    da_r = _softplus(r_ref[0, 0] + dtb_s) * (-jnp.exp(alog_s))
    low = _tri(c, True).astype(BF16)
    up = _tri(c, False).astype(BF16)
    if fwd:
        return dt, _mm_sel_l(low, da), _mm_sel_r(da_r, up)
    return dt, _mm_sel_l(up, da), _mm_sel_r(da_r, low)


def _ssd_kernel(xf_ref, bf_ref, cf_ref, sf_ref, rf_ref, xb_ref, bb_ref, cb_ref, sb_ref, rb_ref,
                alog_l_ref, dtb_l_ref, alog_s_ref, dtb_s_ref, s0f_ref, s0b_ref,
                yf_ref, yb_ref, stf_ref, stb_ref):
    @pl.when(pl.program_id(1) == 0)
    def _():
        stf_ref[...] = s0f_ref[...]
        stb_ref[...] = s0b_ref[...]

    c = SSD_CHUNK
    alog_l, dtb_l, alog_s, dtb_s = alog_l_ref[...], dtb_l_ref[...], alog_s_ref[...], dtb_s_ref[...]
    for d, (x_ref, b_ref, c_ref, s_ref, r_ref, y_ref, st_ref) in enumerate(
            ((xf_ref, bf_ref, cf_ref, sf_ref, rf_ref, yf_ref, stf_ref),
             (xb_ref, bb_ref, cb_ref, sb_ref, rb_ref, yb_ref, stb_ref))):
        fwd = d == 0
        incl = _tri(c, fwd)
        dt_all, ac_all, ar_all = _ssd_gates(s_ref, r_ref, alog_l, dtb_l, alog_s, dtb_s, fwd)
        for g in range(SSD_GROUPS):
            bm = b_ref[0, :, g * SSD_STATE:(g + 1) * SSD_STATE]
            cm = c_ref[0, :, g * SSD_STATE:(g + 1) * SSD_STATE]
            cbm = _mm_nt(cm, bm)
            for e in range(SSD_HPG):
                hid = g * SSD_HPG + e
                ln = SM_DT + d * SSD_HEADS + hid
                sl = slice(hid * SSD_HEAD_DIM, (hid + 1) * SSD_HEAD_DIM)
                dt = dt_all[:, ln:ln + 1]
                ac = ac_all[:, ln:ln + 1]
                ar = ar_all[ln:ln + 1, :]
                lmat = jnp.exp(jnp.where(incl, ac - ar, -jnp.inf))
                xdt = x_ref[0, :, sl] * dt
                a_last = ac[c - 1:c] if fwd else ac[0:1]
                state = st_ref[0, hid]
                y = _mm(cbm * lmat, xdt) + _mm_nt(cm, state) * jnp.exp(ac)
                y_ref[0, :, sl] = y
                st_ref[0, hid] = state * jnp.exp(a_last) + _mm_tn(xdt * jnp.exp(a_last - ac), bm)


def _ssd_scan(xbc, p, a_log, dt_bias, s0f, s0b):
    bsz, seq, _ = xbc.shape
    c = SSD_CHUNK
    nc = seq // c
    rows = _small_rows(p, c)
    alog_l, alog_s = _lane_param(a_log, SM_DT)
    dtb_l, dtb_s = _lane_param(dt_bias, SM_DT)
    gn = SSD_GROUPS * SSD_STATE
    sm_blk = C_SMALL // LANES

    def chunk_specs(cmap):
        return [pl.BlockSpec((1, c, SSD_D_INNER), lambda b, i: (b, cmap(i), 0)),
                pl.BlockSpec((1, c, gn), lambda b, i: (b, cmap(i), SSD_D_INNER // gn)),
                pl.BlockSpec((1, c, gn), lambda b, i: (b, cmap(i), SSD_D_INNER // gn + 1)),
                pl.BlockSpec((1, c, LANES), lambda b, i: (b, cmap(i), sm_blk)),
                pl.BlockSpec((1, 1, LANES, c), lambda b, i: (b, cmap(i), 0, 0))]

    fw = lambda i: i
    bw = lambda i: nc - 1 - i
    vec_l = pl.BlockSpec((1, LANES), lambda b, i: (0, 0))
    vec_s = pl.BlockSpec((LANES, 1), lambda b, i: (0, 0))
    st_spec = pl.BlockSpec((1, SSD_HEADS, SSD_HEAD_DIM, SSD_STATE), lambda b, i: (b, 0, 0, 0))
    out_shape = [jax.ShapeDtypeStruct((bsz, seq, SSD_D_INNER), F32)] * 2 + \
                [jax.ShapeDtypeStruct((bsz, SSD_HEADS, SSD_HEAD_DIM, SSD_STATE), F32)] * 2
    return pl.pallas_call(
        _ssd_kernel,
        grid=(bsz, nc),
        in_specs=chunk_specs(fw) + chunk_specs(bw) + [vec_l, vec_l, vec_s, vec_s, st_spec, st_spec],
        out_specs=[pl.BlockSpec((1, c, SSD_D_INNER), lambda b, i: (b, i, 0)),
                   pl.BlockSpec((1, c, SSD_D_INNER), lambda b, i: (b, nc - 1 - i, 0)),
                   st_spec, st_spec],
        out_shape=out_shape,
        compiler_params=_cparams(("arbitrary", "arbitrary")),
        name="ssd_scan",
    )(xbc, xbc, xbc, p, rows, xbc, xbc, xbc, p, rows, alog_l, dtb_l, alog_s, dtb_s, s0f, s0b)


def _kv_prep_kernel(k_ref, v_ref, nw_ref, cos_ref, sin_ref, ko_ref, vo_ref, *, rope):
    parts = []
    for h in range(ATT_KV_HEADS):
        kh = _rms(k_ref[0, :, h * ATT_HEAD_DIM:(h + 1) * ATT_HEAD_DIM], nw_ref[...])
        if rope:
            kh = _rope(kh, cos_ref[...], sin_ref[...])
        parts.append(kh)
    ko_ref[0] = jnp.concatenate(parts, axis=-1).astype(BF16)
    vo_ref[0] = v_ref[0].astype(BF16)


def _kv_prep(p, k_norm, cos, sin, rope, ts):
    bsz, seq, _ = p.shape
    kb = C_AKV // ATT_KV_DIM
    tab = pl.BlockSpec((ts, ATT_HEAD_DIM), lambda b, i: (i if rope else 0, 0))
    return pl.pallas_call(
        functools.partial(_kv_prep_kernel, rope=rope),
        grid=(bsz, seq // ts),
        in_specs=[pl.BlockSpec((1, ts, ATT_KV_DIM), lambda b, i: (b, i, kb)),
                  pl.BlockSpec((1, ts, ATT_KV_DIM), lambda b, i: (b, i, kb + 1)),
                  pl.BlockSpec((1, ATT_HEAD_DIM), lambda b, i: (0, 0)), tab, tab],
        out_specs=[pl.BlockSpec((1, ts, ATT_KV_DIM), lambda b, i: (b, i, 0))] * 2,
        out_shape=[jax.ShapeDtypeStruct((bsz, seq, ATT_KV_DIM), BF16)] * 2,
        compiler_params=_cparams(("arbitrary", "arbitrary")),
        name="kv_prep",
    )(p, p, k_norm.reshape(1, ATT_HEAD_DIM), cos, sin)


def _attn_kernel(*refs, n_seg, rope):
    q_ref, nw_ref, cos_ref, sin_ref = refs[:4]
    kv_refs = refs[4:4 + 2 * n_seg]
    o_ref = refs[4 + 2 * n_seg]
    scale = ATT_HEAD_DIM ** -0.5
    for r in range(ATT_REP):
        sl = slice(r * ATT_HEAD_DIM, (r + 1) * ATT_HEAD_DIM)
        qh = _rms(q_ref[0, :, sl], nw_ref[...])
        if rope:
            qh = _rope(qh, cos_ref[...], sin_ref[...])
        qh = qh.astype(BF16)
        scores = [_mm_nt(qh, kv_refs[2 * s][0]) * scale for s in range(n_seg)]
        m = functools.reduce(jnp.maximum, [jnp.max(sc, axis=-1, keepdims=True) for sc in scores])
        probs = [jnp.exp(sc - m) for sc in scores]
        denom = functools.reduce(jnp.add, [jnp.sum(pr, axis=-1, keepdims=True) for pr in probs])
        acc = functools.reduce(jnp.add, [_mm(probs[s], kv_refs[2 * s + 1][0]) for s in range(n_seg)])
        o_ref[0, :, sl] = acc / denom


def _attention(p, q_norm, cos, sin, kv_segs, rope, tq):
    bsz, seq, _ = p.shape
    gw = ATT_REP * ATT_HEAD_DIM
    qb = C_AQ // gw
    n_seg = len(kv_segs)
    tab = pl.BlockSpec((tq, ATT_HEAD_DIM), lambda b, g, i: (i if rope else 0, 0))
    in_specs = [pl.BlockSpec((1, tq, gw), lambda b, g, i: (b, i, qb + g)),
                pl.BlockSpec((1, ATT_HEAD_DIM), lambda b, g, i: (0, 0)), tab, tab]
    args = [p, q_norm.reshape(1, ATT_HEAD_DIM), cos, sin]
    for k_arr, v_arr in kv_segs:
        lk = k_arr.shape[1]
        spec = pl.BlockSpec((1, lk, ATT_HEAD_DIM), lambda b, g, i: (b, 0, g))
        in_specs += [spec, spec]
        args += [k_arr, v_arr]
    return pl.pallas_call(
        functools.partial(_attn_kernel, n_seg=n_seg, rope=rope),
        grid=(bsz, ATT_KV_HEADS, seq // tq),
        in_specs=in_specs,
        out_specs=pl.BlockSpec((1, tq, gw), lambda b, g, i: (b, i, g)),
        out_shape=jax.ShapeDtypeStruct((bsz, seq, ATT_Q_DIM), F32),
        compiler_params=_cparams(("arbitrary", "arbitrary", "arbitrary")),
        name="attention",
    )(*args)


def _merge_kernel(x_ref, mod_ref, of_ref, ob_ref, gz_ref, yf_ref, yb_ref, xs_ref, sz_ref, att_ref, gate_ref,
                  gnw_ref, snw_ref, dsk_ref, wg_ref, ws_ref, wa_ref, wo_ref, o_ref):
    o = of_ref[0] + ob_ref[0]
    gz = gz_ref[0]
    parts = []
    for h in range(GDN_HEADS):
        sl = slice(h * GDN_DV, (h + 1) * GDN_DV)
        parts.append(_rms(o[:, sl], gnw_ref[...]) * _silu(gz[:, sl]))
    y_gdn = jnp.concatenate(parts, axis=-1)

    y = yf_ref[0] + yb_ref[0] + dsk_ref[...] * xs_ref[0]
    y = y * _silu(sz_ref[0])
    snw = snw_ref[...]
    gw = SSD_D_INNER // SSD_GROUPS
    y_ssd = jnp.concatenate([_rms(y[:, g * gw:(g + 1) * gw], snw[:, g * gw:(g + 1) * gw])
                             for g in range(SSD_GROUPS)], axis=-1)

    gates = _sigmoid(gate_ref[0])
    d = D_MODEL
    m = (gates[:, :d] * _mm(y_gdn, wg_ref[...]) + gates[:, d:2 * d] * _mm(y_ssd, ws_ref[...])
         + gates[:, 2 * d:] * _mm(att_ref[0], wa_ref[...]))
    g1 = mod_ref[0][2:3]
    o_ref[0] = x_ref[0] + g1 * _mm(m, wo_ref[...])


def _merge(x, mod_l, mod_row, o_f, o_b, y_f, y_b, xbc, att, p, gdn_norm, ssd_norm, ssd_d, wg, ws, wa, wo, tm):
    bsz, seq, d = x.shape
    row = lambda cb: pl.BlockSpec((1, tm, d), lambda b, i: (b, i, cb))
    vec = lambda n: pl.BlockSpec((1, n), lambda b, i: (0, 0))
    wsp = pl.BlockSpec((d, d), lambda b, i: (0, 0))
    return pl.pallas_call(
        _merge_kernel,
        grid=(bsz, seq // tm),
        in_specs=[row(0), pl.BlockSpec((1, 6, d), lambda b, i: (mod_row(b), 0, 0)),
                  row(0), row(0), row(C_GZ // d), row(0), row(0), row(0), row(C_SZ // d), row(0),
                  pl.BlockSpec((1, tm, 3 * d), lambda b, i: (b, i, C_GATE // (3 * d))),
                  vec(GDN_DV), vec(d), vec(d), wsp, wsp, wsp, wsp],
        out_specs=row(0),
        out_shape=jax.ShapeDtypeStruct((bsz, seq, d), F32),
        compiler_params=_cparams(("arbitrary", "arbitrary")),
        name="merge",
    )(x, mod_l, o_f, o_b, p, y_f, y_b, xbc, p, att, p,
      gdn_norm.reshape(1, GDN_DV), ssd_norm.reshape(1, d),
      jnp.repeat(ssd_d, SSD_HEAD_DIM).reshape(1, d), wg, ws, wa, wo)


def _mlp_kernel(x_ref, mod_ref, nw_ref, w1_ref, w2_ref, o_ref, h_ref, acc_ref):
    k = pl.program_id(2)
    m = mod_ref[0]

    @pl.when(k == 0)
    def _():
        h_ref[...] = _norm_mod(x_ref[0], nw_ref[...], m[4:5], m[3:4]).astype(BF16)
        acc_ref[...] = jnp.zeros_like(acc_ref)

    a = jnp.maximum(jnp.dot(h_ref[...], w1_ref[...], preferred_element_type=F32), 0.0)
    acc_ref[...] += _mm(a * a, w2_ref[...])

    @pl.when(k == pl.num_programs(2) - 1)
    def _():
        o_ref[0] = x_ref[0] + m[5:6] * acc_ref[...]


def _mlp(x, mod_l, mod_row, nw, w1, w2, tm, tf):
    bsz, seq, d = x.shape
    return pl.pallas_call(
        _mlp_kernel,
        grid=(bsz, seq // tm, D_FF // tf),
        in_specs=[pl.BlockSpec((1, tm, d), lambda b, i, k: (b, i, 0)),
                  pl.BlockSpec((1, 6, d), lambda b, i, k: (mod_row(b), 0, 0)),
                  pl.BlockSpec((1, d), lambda b, i, k: (0, 0)),
                  pl.BlockSpec((d, tf), lambda b, i, k: (0, k)),
                  pl.BlockSpec((tf, d), lambda b, i, k: (k, 0))],
        out_specs=pl.BlockSpec((1, tm, d), lambda b, i, k: (b, i, 0)),
        out_shape=jax.ShapeDtypeStruct((bsz, seq, d), F32),
        scratch_shapes=[pltpu.VMEM((tm, d), BF16), pltpu.VMEM((tm, d), F32)],
        compiler_params=_cparams(("arbitrary", "arbitrary", "arbitrary")),
        name="mlp",
    )(x, mod_l, nw.reshape(1, d), w1, w2)


def _reorder_w_in(w_in):
    depth, d, _ = w_in.shape
    o = 0
    seg = {}
    for name, size in (("qkv", GDN_QKV), ("gz", GDN_V_DIM), ("beta", 2 * GDN_HEADS), ("a", 2 * GDN_HEADS),
                       ("sz", SSD_D_INNER), ("xbc", SSD_XBC), ("dt", 2 * SSD_HEADS),
                       ("aq", ATT_Q_DIM), ("akv", 2 * ATT_KV_DIM), ("gate", 3 * D_MODEL)):
        seg[name] = w_in[:, :, o:o + size]
        o += size
    pad = jnp.zeros((depth, d, NP_COLS - (C_SMALL + SM_DT + 2 * SSD_HEADS)), w_in.dtype)
    out = jnp.concatenate([seg["qkv"], seg["gz"], seg["sz"], seg["aq"], seg["gate"], seg["xbc"], seg["akv"],
                           seg["beta"], seg["a"], seg["dt"], pad], axis=-1)
    return out.astype(BF16)


def _rope_tables(seq):
    t = jnp.arange(seq, dtype=jnp.int32)
    q = ATT_HEAD_DIM // 4
    freqs = ROPE_THETA ** (-jnp.arange(q, dtype=F32) / q)
    ang_r = (t // GRID_W).astype(F32)[:, None] * freqs[None, :]
    ang_c = (t % GRID_W).astype(F32)[:, None] * freqs[None, :]
    cos = jnp.concatenate([jnp.cos(ang_r)] * 2 + [jnp.cos(ang_c)] * 2, axis=-1)
    sin = jnp.concatenate([-jnp.sin(ang_r), jnp.sin(ang_r), -jnp.sin(ang_c), jnp.sin(ang_c)], axis=-1)
    return cos, sin


def _tile(seq, want):
    return min(seq, want)


def kernel(x, c, ctx, c_ctx, w_mod, b_mod, norm_mix, norm_mlp, w_in, gdn_conv, gdn_a_log, gdn_dt_bias, gdn_norm,
           ssd_conv_w, ssd_conv_b, ssd_a_log, ssd_dt_bias, ssd_d, ssd_norm, att_q_norm, att_k_norm,
           w_br_gdn, w_br_ssd, w_br_att, w_out, w_ff1, w_ff2):
    bsz, seq, d = x.shape
    ctx_len = ctx.shape[1]
    depth = w_in.shape[0]
    assert bsz < MOD_ROWS and d == D_MODEL

    cc = jnp.zeros((MOD_ROWS, d), F32).at[:bsz].set(c).at[bsz].set(c_ctx)
    mod = _modulation(cc, w_mod, b_mod).reshape(depth, MOD_ROWS, 6, d)
    lat_row = lambda b: b
    ctx_row = lambda b: bsz

    w_in_r = _reorder_w_in(w_in)
    wg, ws, wa, wo = (w.astype(BF16) for w in (w_br_gdn, w_br_ssd, w_br_att, w_out))
    w1, w2 = w_ff1.astype(BF16), w_ff2.astype(BF16)
    cos, sin = _rope_tables(seq)

    zg = jnp.zeros((bsz, GDN_HEADS, GDN_DK, GDN_DV), F32)
    zs = jnp.zeros((bsz, SSD_HEADS, SSD_HEAD_DIM, SSD_STATE), F32)
    xc = ctx
    for l in range(depth):
        last = l == depth - 1
        p_lat = _in_projection(x, mod[l], lat_row, norm_mix[l], w_in_r[l], _tile(seq, 512))
        p_ctx = _in_projection(xc, mod[l], ctx_row, norm_mix[l], w_in_r[l], _tile(ctx_len, 512))

        qkv_c = _short_conv(p_ctx, C_QKV, GDN_QKV, GDN_QK_DIM, _tile(ctx_len, 256), gdn_conv[l], None)
        qkv_l = _short_conv(p_lat, C_QKV, GDN_QKV, GDN_QK_DIM, _tile(seq, 256), gdn_conv[l], None)
        xbc_c = _short_conv(p_ctx, C_XBC, SSD_XBC, 512, _tile(ctx_len, 512), ssd_conv_w[l], ssd_conv_b[l])
        xbc_l = _short_conv(p_lat, C_XBC, SSD_XBC, 512, _tile(seq, 512), ssd_conv_w[l], ssd_conv_b[l])

        ogf_c, ogb_c, sgf, sgb = _gdn_scan(qkv_c, p_ctx, gdn_a_log[l], gdn_dt_bias[l], zg, zg)
        ogf_l, ogb_l, _, _ = _gdn_scan(qkv_l, p_lat, gdn_a_log[l], gdn_dt_bias[l], sgf, sgb)
        ysf_c, ysb_c, ssf, ssb = _ssd_scan(xbc_c, p_ctx, ssd_a_log[l], ssd_dt_bias[l], zs, zs)
        ysf_l, ysb_l, _, _ = _ssd_scan(xbc_l, p_lat, ssd_a_log[l], ssd_dt_bias[l], ssf, ssb)

        k_c, v_c = _kv_prep(p_ctx, att_k_norm[l], cos, sin, False, _tile(ctx_len, 256))
        k_l, v_l = _kv_prep(p_lat, att_k_norm[l], cos, sin, True, _tile(seq, 512))
        att_l = _attention(p_lat, att_q_norm[l], cos, sin, [(k_c, v_c), (k_l, v_l)], True, _tile(seq, 256))

        x = _merge(x, mod[l], lat_row, ogf_l, ogb_l, ysf_l, ysb_l, xbc_l, att_l, p_lat,
                   gdn_norm[l], ssd_norm[l], ssd_d[l], wg[l], ws[l], wa[l], wo[l], _tile(seq, 256))
        x = _mlp(x, mod[l], lat_row, norm_mlp[l], w1[l], w2[l], _tile(seq, 512), 1024)

        if not last:
            att_c = _attention(p_ctx, att_q_norm[l], cos, sin, [(k_c, v_c)], False, _tile(ctx_len, 256))
            xc = _merge(xc, mod[l], ctx_row, ogf_c, ogb_c, ysf_c, ysb_c, xbc_c, att_c, p_ctx,
                        gdn_norm[l], ssd_norm[l], ssd_d[l], wg[l], ws[l], wa[l], wo[l], _tile(ctx_len, 256))
            xc = _mlp(xc, mod[l], ctx_row, norm_mlp[l], w1[l], w2[l], _tile(ctx_len, 512), 1024)
    return x
```

```python
import functools

import jax
import jax.numpy as jnp
from jax import lax
from jax.experimental import pallas as pl
from jax.experimental.pallas import tpu as pltpu

F32 = jnp.float32
BF16 = jnp.bfloat16

D_MODEL = 1024
GRID_W = 64
EPS = 1e-6

GDN_HEADS = 8
GDN_DK = 128
GDN_DV = 128
GDN_CHUNK = 64
GDN_QK_DIM = GDN_HEADS * GDN_DK
GDN_V_DIM = GDN_HEADS * GDN_DV
GDN_QKV = 2 * GDN_QK_DIM + GDN_V_DIM

SSD_D_INNER = D_MODEL
SSD_HEAD_DIM = 64
SSD_HEADS = SSD_D_INNER // SSD_HEAD_DIM
SSD_GROUPS = 2
SSD_HPG = SSD_HEADS // SSD_GROUPS
SSD_STATE = 128
SSD_CHUNK = 128
SSD_XBC = SSD_D_INNER + 2 * SSD_GROUPS * SSD_STATE

ATT_HEADS = 8
ATT_KV_HEADS = 2
ATT_REP = ATT_HEADS // ATT_KV_HEADS
ATT_HEAD_DIM = 128
ATT_Q_DIM = ATT_HEADS * ATT_HEAD_DIM
ATT_KV_DIM = ATT_KV_HEADS * ATT_HEAD_DIM
ROPE_THETA = 10000.0
D_FF = 4 * D_MODEL

LANES = 128
MOD_ROWS = 16

C_QKV = 0
C_GZ = C_QKV + GDN_QKV
C_SZ = C_GZ + GDN_V_DIM
C_AQ = C_SZ + SSD_D_INNER
C_GATE = C_AQ + ATT_Q_DIM
C_XBC = C_GATE + 3 * D_MODEL
C_AKV = C_XBC + SSD_XBC
C_SMALL = C_AKV + 2 * ATT_KV_DIM
NP_COLS = 11520
SM_BETA = 0
SM_A = 2 * GDN_HEADS
SM_DT = 4 * GDN_HEADS

VMEM_LIMIT = 56 * 1024 * 1024


def _cparams(sem):
    return pltpu.CompilerParams(dimension_semantics=sem, vmem_limit_bytes=VMEM_LIMIT)


def _mm(a, b):
    return jnp.dot(a.astype(BF16), b.astype(BF16), preferred_element_type=F32)


def _mm_nt(a, b):
    return lax.dot_general(a.astype(BF16), b.astype(BF16), (((1,), (1,)), ((), ())),
                           preferred_element_type=F32)


def _mm_tn(a, b):
    return lax.dot_general(a.astype(BF16), b.astype(BF16), (((0,), (0,)), ((), ())),
                           preferred_element_type=F32)


def _split3(x):
    hi = x.astype(BF16)
    r = x - hi.astype(F32)
    mid = r.astype(BF16)
    lo = (r - mid.astype(F32)).astype(BF16)
    return hi, mid, lo


def _mm_sel_l(sel, x):
    hi, mid, lo = _split3(x)
    d = lambda p: jnp.dot(sel, p, preferred_element_type=F32)
    return (d(hi) + d(mid)) + d(lo)


def _mm_sel_r(x, sel):
    hi, mid, lo = _split3(x)
    d = lambda p: jnp.dot(p, sel, preferred_element_type=F32)
    return (d(hi) + d(mid)) + d(lo)


def _mm_x3(a, b):
    ah = a.astype(BF16)
    al = (a - ah.astype(F32)).astype(BF16)
    bh = b.astype(BF16)
    bl = (b - bh.astype(F32)).astype(BF16)
    d = lambda p, q: jnp.dot(p, q, preferred_element_type=F32)
    return d(ah, bh) + (d(ah, bl) + d(al, bh))


def _sigmoid(x):
    return 1.0 / (1.0 + jnp.exp(-x))


def _silu(x):
    return x * _sigmoid(x)


def _softplus(x):
    return jnp.maximum(x, 0.0) + jnp.log(1.0 + jnp.exp(-jnp.abs(x)))


def _rms(x, w):
    return x * lax.rsqrt(jnp.mean(x * x, axis=-1, keepdims=True) + EPS) * w


def _norm_mod(x, nw, scale, shift):
    return _rms(x, nw) * (1.0 + scale) + shift


def _tri(n, lower):
    i = lax.broadcasted_iota(jnp.int32, (n, n), 0)
    j = lax.broadcasted_iota(jnp.int32, (n, n), 1)
    return (i >= j) if lower else (i <= j)


def _rope(x, cos, sin):
    lane = lax.broadcasted_iota(jnp.int32, x.shape, 1)
    q = LANES // 4
    swapped = jnp.where((lane & q) == 0, pltpu.roll(x, LANES - q, 1), pltpu.roll(x, q, 1))
    return x * cos + swapped * sin


def _mod_kernel(c_ref, w_ref, b_ref, o_ref):
    o_ref[0] = _mm(_silu(c_ref[...]), w_ref[0]) + b_ref[0]


def _modulation(cc, w_mod, b_mod):
    depth = w_mod.shape[0]
    n = w_mod.shape[2]
    tn = D_MODEL
    return pl.pallas_call(
        _mod_kernel,
        grid=(depth, n // tn),
        in_specs=[pl.BlockSpec((MOD_ROWS, D_MODEL), lambda l, j: (0, 0)),
                  pl.BlockSpec((1, D_MODEL, tn), lambda l, j: (l, 0, j)),
                  pl.BlockSpec((1, 1, tn), lambda l, j: (l, 0, j))],
        out_specs=pl.BlockSpec((1, MOD_ROWS, tn), lambda l, j: (l, 0, j)),
        out_shape=jax.ShapeDtypeStruct((depth, MOD_ROWS, n), F32),
        compiler_params=_cparams(("arbitrary", "arbitrary")),
        name="modulation",
    )(cc, w_mod, b_mod.reshape(depth, 1, n))


def _inproj_kernel(x_ref, mod_ref, nw_ref, w_ref, o_ref, h_ref):
    @pl.when(pl.program_id(2) == 0)
    def _():
        m = mod_ref[0]
        h_ref[...] = _norm_mod(x_ref[0], nw_ref[...], m[1:2], m[0:1]).astype(BF16)

    o_ref[0] = jnp.dot(h_ref[...], w_ref[...], preferred_element_type=F32)


def _in_projection(x, mod_l, mod_row, nw, w_r, tm):
    bsz, seq, d = x.shape
    tn = NP_COLS // 6
    return pl.pallas_call(
        _inproj_kernel,
        grid=(bsz, seq // tm, NP_COLS // tn),
        in_specs=[pl.BlockSpec((1, tm, d), lambda b, i, j: (b, i, 0)),
                  pl.BlockSpec((1, 6, d), lambda b, i, j: (mod_row(b), 0, 0)),
                  pl.BlockSpec((1, d), lambda b, i, j: (0, 0)),
                  pl.BlockSpec((d, tn), lambda b, i, j: (0, j))],
        out_specs=pl.BlockSpec((1, tm, tn), lambda b, i, j: (b, i, j)),
        out_shape=jax.ShapeDtypeStruct((bsz, seq, NP_COLS), F32),
        scratch_shapes=[pltpu.VMEM((tm, d), BF16)],
        compiler_params=_cparams(("arbitrary", "arbitrary", "arbitrary")),
        name="in_projection",
    )(x, mod_l, nw.reshape(1, d), w_r)


def _conv3(x, prev_row, next_row, w):
    ts = x.shape[0]
    rid = lax.broadcasted_iota(jnp.int32, x.shape, 0)
    xm1 = jnp.where(rid == 0, prev_row, pltpu.roll(x, 1, 0))
    xp1 = jnp.where(rid == ts - 1, next_row, pltpu.roll(x, ts - 1, 0))
    return w[0:1] * xm1 + w[1:2] * x + w[2:3] * xp1


def _conv_halo(xp_ref, xn_ref):
    i = pl.program_id(1)
    prev_row = jnp.where(i == 0, 0.0, xp_ref[0, 7:8, :])
    next_row = jnp.where(i == pl.num_programs(1) - 1, 0.0, xn_ref[0, 0:1, :])
    return prev_row, next_row


def _gdn_conv_kernel(x_ref, xp_ref, xn_ref, w_ref, o_ref):
    prev_row, next_row = _conv_halo(xp_ref, xn_ref)
    y = _silu(_conv3(x_ref[0], prev_row, next_row, w_ref[...]))
    j = pl.program_id(2)
    parts = []
    for h in range(GDN_HEADS):
        yh = y[:, h * GDN_DK:(h + 1) * GDN_DK]
        parts.append(yh * lax.rsqrt(jnp.sum(yh * yh, axis=-1, keepdims=True) + EPS))
    yn = jnp.concatenate(parts, axis=-1)
    yn = yn * jnp.where(j == 0, GDN_DK ** -0.5, 1.0)
    o_ref[0] = jnp.where(j < 2, yn, y)


def _ssd_conv_kernel(x_ref, xp_ref, xn_ref, w_ref, b_ref, o_ref):
    prev_row, next_row = _conv_halo(xp_ref, xn_ref)
    o_ref[0] = _silu(_conv3(x_ref[0], prev_row, next_row, w_ref[...]) + b_ref[...])


def _short_conv(p, col0, width, tc, ts, conv_w, conv_b):
    bsz, seq, _ = p.shape
    cb0 = col0 // tc
    nrb = seq // 8
    rpb = ts // 8
    x_specs = [pl.BlockSpec((1, ts, tc), lambda b, i, j: (b, i, cb0 + j)),
               pl.BlockSpec((1, 8, tc), lambda b, i, j: (b, jnp.maximum(i * rpb - 1, 0), cb0 + j)),
               pl.BlockSpec((1, 8, tc), lambda b, i, j: (b, jnp.minimum((i + 1) * rpb, nrb - 1), cb0 + j)),
               pl.BlockSpec((3, tc), lambda b, i, j: (0, j))]
    args = [p, p, p, conv_w]
    if conv_b is None:
        body = _gdn_conv_kernel
    else:
        body = _ssd_conv_kernel
        x_specs.append(pl.BlockSpec((1, tc), lambda b, i, j: (0, j)))
        args.append(conv_b.reshape(1, width))
    return pl.pallas_call(
        body,
        grid=(bsz, seq // ts, width // tc),
        in_specs=x_specs,
        out_specs=pl.BlockSpec((1, ts, tc), lambda b, i, j: (b, i, j)),
        out_shape=jax.ShapeDtypeStruct((bsz, seq, width), F32),
        compiler_params=_cparams(("arbitrary", "arbitrary", "arbitrary")),
        name="short_conv",
    )(*args)


def _unit_tri_inverses(mats):
    n = mats[0].shape[0]
    eye = (lax.broadcasted_iota(jnp.int32, (n, n), 0) == lax.broadcasted_iota(jnp.int32, (n, n), 1)).astype(F32)
    invs = [eye - a for a in mats]
    powers = list(mats)
    for _ in range(n.bit_length() - 2):
        powers = [_mm_x3(p, p) for p in powers]
        invs = [i + _mm_x3(i, p) for i, p in zip(invs, powers)]
    return invs


def _gdn_chunks(chains):
    c = chains[0][0].shape[0]
    ii = lax.broadcasted_iota(jnp.int32, (c, c), 0)
    jj = lax.broadcasted_iota(jnp.int32, (c, c), 1)
    masks = {True: (ii >= jj, ii > jj), False: (ii <= jj, ii < jj)}
    decays = [jnp.exp(jnp.where(masks[fwd][0], gc - gr, -jnp.inf)) for (_, _, _, _, gc, gr, _, fwd) in chains]
    kbs = [k * beta for (_, k, _, beta, _, _, _, _) in chains]
    a_mats = [jnp.where(masks[ch[7]][1], _mm_nt(kb, ch[1]) * dec, 0.0)
              for ch, kb, dec in zip(chains, kbs, decays)]
    qks = [jnp.where(masks[ch[7]][0], _mm_nt(ch[0], ch[1]) * dec, 0.0) for ch, dec in zip(chains, decays)]
    invs = _unit_tri_inverses(a_mats)
    egs = [jnp.exp(ch[4]) for ch in chains]
    sols = [_mm_x3(inv, jnp.concatenate([ch[2] * ch[3], kb * eg], axis=-1))
            for inv, ch, kb, eg in zip(invs, chains, kbs, egs)]
    g_lasts = [ch[4][c - 1:c] if ch[7] else ch[4][0:1] for ch in chains]
    v_news = [sol[:, :GDN_DV] - _mm(sol[:, GDN_DV:], ch[6]) for sol, ch in zip(sols, chains)]
    outs = [_mm(ch[0] * eg, ch[6]) + _mm(qk, vn) for ch, eg, qk, vn in zip(chains, egs, qks, v_news)]
    states = [ch[6] * jnp.exp(gl) + _mm_tn(ch[1] * jnp.exp(gl - ch[4]), vn)
              for ch, gl, vn in zip(chains, g_lasts, v_news)]
    return list(zip(outs, states))


def _gdn_gates(s_ref, r_ref, alog_l, dtb_l, alog_s, dtb_s, fwd):
    c = GDN_CHUNK
    raw = s_ref[0]
    beta = _sigmoid(raw)
    la = -jnp.exp(alog_l) * _softplus(raw + dtb_l)
    rraw = r_ref[0, 0]
    la_r = -jnp.exp(alog_s) * _softplus(rraw + dtb_s)
    low = _tri(c, True).astype(BF16)
    up = _tri(c, False).astype(BF16)
    if fwd:
        return beta, _mm_sel_l(low, la), _mm_sel_r(la_r, up)
    return beta, _mm_sel_l(up, la), _mm_sel_r(la_r, low)


def _gdn_kernel(qf_ref, kf_ref, vf_ref, sf_ref, rf_ref, qb_ref, kb_ref, vb_ref, sb_ref, rb_ref,
                alog_l_ref, dtb_l_ref, alog_s_ref, dtb_s_ref, s0f_ref, s0b_ref,
                of_ref, ob_ref, stf_ref, stb_ref):
    @pl.when(pl.program_id(1) == 0)
    def _():
        stf_ref[...] = s0f_ref[...]
        stb_ref[...] = s0b_ref[...]

    alog_l, dtb_l, alog_s, dtb_s = alog_l_ref[...], dtb_l_ref[...], alog_s_ref[...], dtb_s_ref[...]
    chains, sinks = [], []
    for d, (q_ref, k_ref, v_ref, s_ref, r_ref, o_ref, st_ref) in enumerate(
            ((qf_ref, kf_ref, vf_ref, sf_ref, rf_ref, of_ref, stf_ref),
             (qb_ref, kb_ref, vb_ref, sb_ref, rb_ref, ob_ref, stb_ref))):
        fwd = d == 0
        beta_all, gc_all, gr_all = _gdn_gates(s_ref, r_ref, alog_l, dtb_l, alog_s, dtb_s, fwd)
        for h in range(GDN_HEADS):
            lb = SM_BETA + d * GDN_HEADS + h
            la = SM_A + d * GDN_HEADS + h
            sl = slice(h * GDN_DK, (h + 1) * GDN_DK)
            chains.append((q_ref[0, :, sl], k_ref[0, :, sl], v_ref[0, :, sl],
                           beta_all[:, lb:lb + 1], gc_all[:, la:la + 1], gr_all[la:la + 1, :],
                           st_ref[0, h], fwd))
            sinks.append((o_ref, st_ref, h, sl))
    for (o, st), (o_ref, st_ref, h, sl) in zip(_gdn_chunks(chains), sinks):
        o_ref[0, :, sl] = o
        st_ref[0, h] = st


def _lane_param(vals, offset):
    flat = vals.reshape(-1).astype(F32)
    v = jnp.zeros((LANES,), F32).at[offset:offset + flat.shape[0]].set(flat)
    return v.reshape(1, LANES), v.reshape(LANES, 1)


def _small_rows(p, chunk):
    bsz, seq, _ = p.shape
    small = p[:, :, C_SMALL:C_SMALL + LANES]
    return jnp.swapaxes(small.reshape(bsz, seq // chunk, chunk, LANES), 2, 3)


def _gdn_scan(qkv, p, a_log, dt_bias, s0f, s0b):
    bsz, seq, _ = qkv.shape
    c = GDN_CHUNK
    nc = seq // c
    rows = _small_rows(p, c)
    alog_l, alog_s = _lane_param(a_log, SM_A)
    dtb_l, dtb_s = _lane_param(dt_bias, SM_A)
    hw = GDN_QK_DIM
    sm_blk = C_SMALL // LANES

    def chunk_specs(cmap):
        return [pl.BlockSpec((1, c, hw), lambda b, i: (b, cmap(i), 0)),
                pl.BlockSpec((1, c, hw), lambda b, i: (b, cmap(i), 1)),
                pl.BlockSpec((1, c, hw), lambda b, i: (b, cmap(i), 2)),
                pl.BlockSpec((1, c, LANES), lambda b, i: (b, cmap(i), sm_blk)),
                pl.BlockSpec((1, 1, LANES, c), lambda b, i: (b, cmap(i), 0, 0))]

    fw = lambda i: i
    bw = lambda i: nc - 1 - i
    vec_l = pl.BlockSpec((1, LANES), lambda b, i: (0, 0))
    vec_s = pl.BlockSpec((LANES, 1), lambda b, i: (0, 0))
    st_spec = pl.BlockSpec((1, GDN_HEADS, GDN_DK, GDN_DV), lambda b, i: (b, 0, 0, 0))
    out_shape = [jax.ShapeDtypeStruct((bsz, seq, GDN_V_DIM), F32)] * 2 + \
                [jax.ShapeDtypeStruct((bsz, GDN_HEADS, GDN_DK, GDN_DV), F32)] * 2
    return pl.pallas_call(
        _gdn_kernel,
        grid=(bsz, nc),
        in_specs=chunk_specs(fw) + chunk_specs(bw) + [vec_l, vec_l, vec_s, vec_s, st_spec, st_spec],
        out_specs=[pl.BlockSpec((1, c, GDN_V_DIM), lambda b, i: (b, i, 0)),
                   pl.BlockSpec((1, c, GDN_V_DIM), lambda b, i: (b, nc - 1 - i, 0)),
                   st_spec, st_spec],
        out_shape=out_shape,
        compiler_params=_cparams(("arbitrary", "arbitrary")),
        name="gdn_scan",
    )(qkv, qkv, qkv, p, rows, qkv, qkv, qkv, p, rows, alog_l, dtb_l, alog_s, dtb_s, s0f, s0b)


def _ssd_gates(s_ref, r_ref, alog_l, dtb_l, alog_s, dtb_s, fwd):
    c = SSD_CHUNK
    dt = _softplus(s_ref[0] + dtb_l)
    da = dt * (-jnp.exp(alog_l))
    da_r = _softplus(r_ref[0, 0] + dtb_s) * (-jnp.exp(alog_s))
    low = _tri(c, True).astype(BF16)
    up = _tri(c, False).astype(BF16)
    if fwd:
        return dt, _mm_sel_l(low, da), _mm_sel_r(da_r, up)
    return dt, _mm_sel_l(up, da), _mm_sel_r(da_r, low)


def _ssd_kernel(xf_ref, bf_ref, cf_ref, sf_ref, rf_ref, xb_ref, bb_ref, cb_ref, sb_ref, rb_ref,
                alog_l_ref, dtb_l_ref, alog_s_ref, dtb_s_ref, s0f_ref, s0b_ref,
                yf_ref, yb_ref, stf_ref, stb_ref):
    @pl.when(pl.program_id(1) == 0)
    def _():
        stf_ref[...] = s0f_ref[...]
        stb_ref[...] = s0b_ref[...]

    c = SSD_CHUNK
    alog_l, dtb_l, alog_s, dtb_s = alog_l_ref[...], dtb_l_ref[...], alog_s_ref[...], dtb_s_ref[...]
    chains = []
    for d, (x_ref, b_ref, c_ref, s_ref, r_ref, y_ref, st_ref) in enumerate(
            ((xf_ref, bf_ref, cf_ref, sf_ref, rf_ref, yf_ref, stf_ref),
             (xb_ref, bb_ref, cb_ref, sb_ref, rb_ref, yb_ref, stb_ref))):
        fwd = d == 0
        incl = _tri(c, fwd)
        dt_all, ac_all, ar_all = _ssd_gates(s_ref, r_ref, alog_l, dtb_l, alog_s, dtb_s, fwd)
        for g in range(SSD_GROUPS):
            bm = b_ref[0, :, g * SSD_STATE:(g + 1) * SSD_STATE]
            cm = c_ref[0, :, g * SSD_STATE:(g + 1) * SSD_STATE]
            cbm = _mm_nt(cm, bm)
            for e in range(SSD_HPG):
                hid = g * SSD_HPG + e
                ln = SM_DT + d * SSD_HEADS + hid
                sl = slice(hid * SSD_HEAD_DIM, (hid + 1) * SSD_HEAD_DIM)
                ac = ac_all[:, ln:ln + 1]
                chains.append(dict(incl=incl, bm=bm, cm=cm, cbm=cbm, ac=ac, ar=ar_all[ln:ln + 1, :],
                                   a_last=ac[c - 1:c] if fwd else ac[0:1],
                                   xdt=x_ref[0, :, sl] * dt_all[:, ln:ln + 1],
                                   state=st_ref[0, hid], y_ref=y_ref, st_ref=st_ref, hid=hid, sl=sl))
    lmats = [jnp.exp(jnp.where(ch["incl"], ch["ac"] - ch["ar"], -jnp.inf)) for ch in chains]
    ys = [_mm(ch["cbm"] * lm, ch["xdt"]) + _mm_nt(ch["cm"], ch["state"]) * jnp.exp(ch["ac"])
          for ch, lm in zip(chains, lmats)]
    sts = [ch["state"] * jnp.exp(ch["a_last"]) + _mm_tn(ch["xdt"] * jnp.exp(ch["a_last"] - ch["ac"]), ch["bm"])
           for ch in chains]
    for ch, y, st in zip(chains, ys, sts):
        ch["y_ref"][0, :, ch["sl"]] = y
        ch["st_ref"][0, ch["hid"]] = st


def _ssd_scan(xbc, p, a_log, dt_bias, s0f, s0b):
    bsz, seq, _ = xbc.shape
    c = SSD_CHUNK
    nc = seq // c
    rows = _small_rows(p, c)
    alog_l, alog_s = _lane_param(a_log, SM_DT)
    dtb_l, dtb_s = _lane_param(dt_bias, SM_DT)
    gn = SSD_GROUPS * SSD_STATE
    sm_blk = C_SMALL // LANES

    def chunk_specs(cmap):
        return [pl.BlockSpec((1, c, SSD_D_INNER), lambda b, i: (b, cmap(i), 0)),
                pl.BlockSpec((1, c, gn), lambda b, i: (b, cmap(i), SSD_D_INNER // gn)),
                pl.BlockSpec((1, c, gn), lambda b, i: (b, cmap(i), SSD_D_INNER // gn + 1)),
                pl.BlockSpec((1, c, LANES), lambda b, i: (b, cmap(i), sm_blk)),
                pl.BlockSpec((1, 1, LANES, c), lambda b, i: (b, cmap(i), 0, 0))]

    fw = lambda i: i
    bw = lambda i: nc - 1 - i
    vec_l = pl.BlockSpec((1, LANES), lambda b, i: (0, 0))
    vec_s = pl.BlockSpec((LANES, 1), lambda b, i: (0, 0))
    st_spec = pl.BlockSpec((1, SSD_HEADS, SSD_HEAD_DIM, SSD_STATE), lambda b, i: (b, 0, 0, 0))
    out_shape = [jax.ShapeDtypeStruct((bsz, seq, SSD_D_INNER), F32)] * 2 + \
                [jax.ShapeDtypeStruct((bsz, SSD_HEADS, SSD_HEAD_DIM, SSD_STATE), F32)] * 2
    return pl.pallas_call(
        _ssd_kernel,
        grid=(bsz, nc),
        in_specs=chunk_specs(fw) + chunk_specs(bw) + [vec_l, vec_l, vec_s, vec_s, st_spec, st_spec],
        out_specs=[pl.BlockSpec((1, c, SSD_D_INNER), lambda b, i: (b, i, 0)),
                   pl.BlockSpec((1, c, SSD_D_INNER), lambda b, i: (b, nc - 1 - i, 0)),
                   st_spec, st_spec],
        out_shape=out_shape,
        compiler_params=_cparams(("arbitrary", "arbitrary")),
        name="ssd_scan",
    )(xbc, xbc, xbc, p, rows, xbc, xbc, xbc, p, rows, alog_l, dtb_l, alog_s, dtb_s, s0f, s0b)


def _kv_prep_kernel(k_ref, v_ref, nw_ref, cos_ref, sin_ref, ko_ref, vo_ref, *, rope):
    parts = []
    for h in range(ATT_KV_HEADS):
        kh = _rms(k_ref[0, :, h * ATT_HEAD_DIM:(h + 1) * ATT_HEAD_DIM], nw_ref[...])
        if rope:
            kh = _rope(kh, cos_ref[...], sin_ref[...])
        parts.append(kh)
    ko_ref[0] = jnp.concatenate(parts, axis=-1).astype(BF16)
    vo_ref[0] = v_ref[0].astype(BF16)


def _kv_prep(p, k_norm, cos, sin, rope, ts):
    bsz, seq, _ = p.shape
    kb = C_AKV // ATT_KV_DIM
    tab = pl.BlockSpec((ts, ATT_HEAD_DIM), lambda b, i: (i if rope else 0, 0))
    return pl.pallas_call(
        functools.partial(_kv_prep_kernel, rope=rope),
        grid=(bsz, seq // ts),
        in_specs=[pl.BlockSpec((1, ts, ATT_KV_DIM), lambda b, i: (b, i, kb)),
                  pl.BlockSpec((1, ts, ATT_KV_DIM), lambda b, i: (b, i, kb + 1)),
                  pl.BlockSpec((1, ATT_HEAD_DIM), lambda b, i: (0, 0)), tab, tab],
        out_specs=[pl.BlockSpec((1, ts, ATT_KV_DIM), lambda b, i: (b, i, 0))] * 2,
        out_shape=[jax.ShapeDtypeStruct((bsz, seq, ATT_KV_DIM), BF16)] * 2,
        compiler_params=_cparams(("arbitrary", "arbitrary")),
        name="kv_prep",
    )(p, p, k_norm.reshape(1, ATT_HEAD_DIM), cos, sin)


def _attn_kernel(*refs, n_seg, rope):
    q_ref, nw_ref, cos_ref, sin_ref = refs[:4]
    kv_refs = refs[4:4 + 2 * n_seg]
    o_ref = refs[4 + 2 * n_seg]
    scale = ATT_HEAD_DIM ** -0.5
    for r in range(ATT_REP):
        sl = slice(r * ATT_HEAD_DIM, (r + 1) * ATT_HEAD_DIM)
        qh = _rms(q_ref[0, :, sl], nw_ref[...])
        if rope:
            qh = _rope(qh, cos_ref[...], sin_ref[...])
        qh = qh.astype(BF16)
        scores = [_mm_nt(qh, kv_refs[2 * s][0]) * scale for s in range(n_seg)]
        m = functools.reduce(jnp.maximum, [jnp.max(sc, axis=-1, keepdims=True) for sc in scores])
        probs = [jnp.exp(sc - m) for sc in scores]
        denom = functools.reduce(jnp.add, [jnp.sum(pr, axis=-1, keepdims=True) for pr in probs])
        acc = functools.reduce(jnp.add, [_mm(probs[s], kv_refs[2 * s + 1][0]) for s in range(n_seg)])
        o_ref[0, :, sl] = acc / denom


def _attention(p, q_norm, cos, sin, kv_segs, rope, tq):
    bsz, seq, _ = p.shape
    gw = ATT_REP * ATT_HEAD_DIM
    qb = C_AQ // gw
    n_seg = len(kv_segs)
    tab = pl.BlockSpec((tq, ATT_HEAD_DIM), lambda b, g, i: (i if rope else 0, 0))
    in_specs = [pl.BlockSpec((1, tq, gw), lambda b, g, i: (b, i, qb + g)),
                pl.BlockSpec((1, ATT_HEAD_DIM), lambda b, g, i: (0, 0)), tab, tab]
    args = [p, q_norm.reshape(1, ATT_HEAD_DIM), cos, sin]
    for k_arr, v_arr in kv_segs:
        lk = k_arr.shape[1]
        spec = pl.BlockSpec((1, lk, ATT_HEAD_DIM), lambda b, g, i: (b, 0, g))
        in_specs += [spec, spec]
        args += [k_arr, v_arr]
    return pl.pallas_call(
        functools.partial(_attn_kernel, n_seg=n_seg, rope=rope),
        grid=(bsz, ATT_KV_HEADS, seq // tq),
        in_specs=in_specs,
        out_specs=pl.BlockSpec((1, tq, gw), lambda b, g, i: (b, i, g)),
        out_shape=jax.ShapeDtypeStruct((bsz, seq, ATT_Q_DIM), F32),
        compiler_params=_cparams(("arbitrary", "arbitrary", "arbitrary")),
        name="attention",
    )(*args)


def _merge_kernel(x_ref, mod_ref, of_ref, ob_ref, gz_ref, yf_ref, yb_ref, xs_ref, sz_ref, att_ref, gate_ref,
                  gnw_ref, snw_ref, dsk_ref, wg_ref, ws_ref, wa_ref, wo_ref, o_ref):
    o = of_ref[0] + ob_ref[0]
    gz = gz_ref[0]
    parts = []
    for h in range(GDN_HEADS):
        sl = slice(h * GDN_DV, (h + 1) * GDN_DV)
        parts.append(_rms(o[:, sl], gnw_ref[...]) * _silu(gz[:, sl]))
    y_gdn = jnp.concatenate(parts, axis=-1)

    y = yf_ref[0] + yb_ref[0] + dsk_ref[...] * xs_ref[0]
    y = y * _silu(sz_ref[0])
    snw = snw_ref[...]
    gw = SSD_D_INNER // SSD_GROUPS
    y_ssd = jnp.concatenate([_rms(y[:, g * gw:(g + 1) * gw], snw[:, g * gw:(g + 1) * gw])
                             for g in range(SSD_GROUPS)], axis=-1)

    gates = _sigmoid(gate_ref[0])
    d = D_MODEL
    m = (gates[:, :d] * _mm(y_gdn, wg_ref[...]) + gates[:, d:2 * d] * _mm(y_ssd, ws_ref[...])
         + gates[:, 2 * d:] * _mm(att_ref[0], wa_ref[...]))
    g1 = mod_ref[0][2:3]
    o_ref[0] = x_ref[0] + g1 * _mm(m, wo_ref[...])


def _merge(x, mod_l, mod_row, o_f, o_b, y_f, y_b, xbc, att, p, gdn_norm, ssd_norm, ssd_d, wg, ws, wa, wo, tm):
    bsz, seq, d = x.shape
    row = lambda cb: pl.BlockSpec((1, tm, d), lambda b, i: (b, i, cb))
    vec = lambda n: pl.BlockSpec((1, n), lambda b, i: (0, 0))
    wsp = pl.BlockSpec((d, d), lambda b, i: (0, 0))
    return pl.pallas_call(
        _merge_kernel,
        grid=(bsz, seq // tm),
        in_specs=[row(0), pl.BlockSpec((1, 6, d), lambda b, i: (mod_row(b), 0, 0)),
                  row(0), row(0), row(C_GZ // d), row(0), row(0), row(0), row(C_SZ // d), row(0),
                  pl.BlockSpec((1, tm, 3 * d), lambda b, i: (b, i, C_GATE // (3 * d))),
                  vec(GDN_DV), vec(d), vec(d), wsp, wsp, wsp, wsp],
        out_specs=row(0),
        out_shape=jax.ShapeDtypeStruct((bsz, seq, d), F32),
        compiler_params=_cparams(("arbitrary", "arbitrary")),
        name="merge",
    )(x, mod_l, o_f, o_b, p, y_f, y_b, xbc, p, att, p,
      gdn_norm.reshape(1, GDN_DV), ssd_norm.reshape(1, d),
      jnp.repeat(ssd_d, SSD_HEAD_DIM).reshape(1, d), wg, ws, wa, wo)


def _mlp_kernel(x_ref, mod_ref, nw_ref, w1_ref, w2_ref, o_ref, h_ref, acc_ref):
    k = pl.program_id(2)
    m = mod_ref[0]

    @pl.when(k == 0)
    def _():
        h_ref[...] = _norm_mod(x_ref[0], nw_ref[...], m[4:5], m[3:4]).astype(BF16)
        acc_ref[...] = jnp.zeros_like(acc_ref)

    a = jnp.maximum(jnp.dot(h_ref[...], w1_ref[...], preferred_element_type=F32), 0.0)
    acc_ref[...] += _mm(a * a, w2_ref[...])

    @pl.when(k == pl.num_programs(2) - 1)
    def _():
        o_ref[0] = x_ref[0] + m[5:6] * acc_ref[...]


def _mlp(x, mod_l, mod_row, nw, w1, w2, tm, tf):
    bsz, seq, d = x.shape
    return pl.pallas_call(
        _mlp_kernel,
        grid=(bsz, seq // tm, D_FF // tf),
        in_specs=[pl.BlockSpec((1, tm, d), lambda b, i, k: (b, i, 0)),
                  pl.BlockSpec((1, 6, d), lambda b, i, k: (mod_row(b), 0, 0)),
                  pl.BlockSpec((1, d), lambda b, i, k: (0, 0)),
                  pl.BlockSpec((d, tf), lambda b, i, k: (0, k)),
                  pl.BlockSpec((tf, d), lambda b, i, k: (k, 0))],
        out_specs=pl.BlockSpec((1, tm, d), lambda b, i, k: (b, i, 0)),
        out_shape=jax.ShapeDtypeStruct((bsz, seq, d), F32),
        scratch_shapes=[pltpu.VMEM((tm, d), BF16), pltpu.VMEM((tm, d), F32)],
        compiler_params=_cparams(("arbitrary", "arbitrary", "arbitrary")),
        name="mlp",
    )(x, mod_l, nw.reshape(1, d), w1, w2)


def _reorder_w_in(w_in):
    depth, d, _ = w_in.shape
    o = 0
    seg = {}
    for name, size in (("qkv", GDN_QKV), ("gz", GDN_V_DIM), ("beta", 2 * GDN_HEADS), ("a", 2 * GDN_HEADS),
                       ("sz", SSD_D_INNER), ("xbc", SSD_XBC), ("dt", 2 * SSD_HEADS),
                       ("aq", ATT_Q_DIM), ("akv", 2 * ATT_KV_DIM), ("gate", 3 * D_MODEL)):
        seg[name] = w_in[:, :, o:o + size]
        o += size
    pad = jnp.zeros((depth, d, NP_COLS - (C_SMALL + SM_DT + 2 * SSD_HEADS)), w_in.dtype)
    out = jnp.concatenate([seg["qkv"], seg["gz"], seg["sz"], seg["aq"], seg["gate"], seg["xbc"], seg["akv"],
                           seg["beta"], seg["a"], seg["dt"], pad], axis=-1)
    return out.astype(BF16)


def _rope_tables(seq):
    t = jnp.arange(seq, dtype=jnp.int32)
    q = ATT_HEAD_DIM // 4
    freqs = ROPE_THETA ** (-jnp.arange(q, dtype=F32) / q)
    ang_r = (t // GRID_W).astype(F32)[:, None] * freqs[None, :]
    ang_c = (t % GRID_W).astype(F32)[:, None] * freqs[None, :]
    cos = jnp.concatenate([jnp.cos(ang_r)] * 2 + [jnp.cos(ang_c)] * 2, axis=-1)
    sin = jnp.concatenate([-jnp.sin(ang_r), jnp.sin(ang_r), -jnp.sin(ang_c), jnp.sin(ang_c)], axis=-1)
    return cos, sin


def _tile(seq, want):
    return min(seq, want)


def kernel(x, c, ctx, c_ctx, w_mod, b_mod, norm_mix, norm_mlp, w_in, gdn_conv, gdn_a_log, gdn_dt_bias, gdn_norm,
           ssd_conv_w, ssd_conv_b, ssd_a_log, ssd_dt_bias, ssd_d, ssd_norm, att_q_norm, att_k_norm,
           w_br_gdn, w_br_ssd, w_br_att, w_out, w_ff1, w_ff2):
    bsz, seq, d = x.shape
    ctx_len = ctx.shape[1]
    depth = w_in.shape[0]
    assert bsz < MOD_ROWS and d == D_MODEL

    cc = jnp.zeros((MOD_ROWS, d), F32).at[:bsz].set(c).at[bsz].set(c_ctx)
    mod = _modulation(cc, w_mod, b_mod).reshape(depth, MOD_ROWS, 6, d)
    lat_row = lambda b: b
    ctx_row = lambda b: bsz

    w_in_r = _reorder_w_in(w_in)
    wg, ws, wa, wo = (w.astype(BF16) for w in (w_br_gdn, w_br_ssd, w_br_att, w_out))
    w1, w2 = w_ff1.astype(BF16), w_ff2.astype(BF16)
    cos, sin = _rope_tables(seq)

    zg = jnp.zeros((bsz, GDN_HEADS, GDN_DK, GDN_DV), F32)
    zs = jnp.zeros((bsz, SSD_HEADS, SSD_HEAD_DIM, SSD_STATE), F32)
    xc = ctx
    for l in range(depth):
        last = l == depth - 1
        p_lat = _in_projection(x, mod[l], lat_row, norm_mix[l], w_in_r[l], _tile(seq, 512))
        p_ctx = _in_projection(xc, mod[l], ctx_row, norm_mix[l], w_in_r[l], _tile(ctx_len, 512))

        qkv_c = _short_conv(p_ctx, C_QKV, GDN_QKV, GDN_QK_DIM, _tile(ctx_len, 256), gdn_conv[l], None)
        qkv_l = _short_conv(p_lat, C_QKV, GDN_QKV, GDN_QK_DIM, _tile(seq, 256), gdn_conv[l], None)
        xbc_c = _short_conv(p_ctx, C_XBC, SSD_XBC, 512, _tile(ctx_len, 512), ssd_conv_w[l], ssd_conv_b[l])
        xbc_l = _short_conv(p_lat, C_XBC, SSD_XBC, 512, _tile(seq, 512), ssd_conv_w[l], ssd_conv_b[l])

        ogf_c, ogb_c, sgf, sgb = _gdn_scan(qkv_c, p_ctx, gdn_a_log[l], gdn_dt_bias[l], zg, zg)
        ogf_l, ogb_l, _, _ = _gdn_scan(qkv_l, p_lat, gdn_a_log[l], gdn_dt_bias[l], sgf, sgb)
        ysf_c, ysb_c, ssf, ssb = _ssd_scan(xbc_c, p_ctx, ssd_a_log[l], ssd_dt_bias[l], zs, zs)
        ysf_l, ysb_l, _, _ = _ssd_scan(xbc_l, p_lat, ssd_a_log[l], ssd_dt_bias[l], ssf, ssb)

        k_c, v_c = _kv_prep(p_ctx, att_k_norm[l], cos, sin, False, _tile(ctx_len, 256))
        k_l, v_l = _kv_prep(p_lat, att_k_norm[l], cos, sin, True, _tile(seq, 512))
        att_l = _attention(p_lat, att_q_norm[l], cos, sin, [(k_c, v_c), (k_l, v_l)], True, _tile(seq, 256))

        x = _merge(x, mod[l], lat_row, ogf_l, ogb_l, ysf_l, ysb_l, xbc_l, att_l, p_lat,
                   gdn_norm[l], ssd_norm[l], ssd_d[l], wg[l], ws[l], wa[l], wo[l], _tile(seq, 256))
        x = _mlp(x, mod[l], lat_row, norm_mlp[l], w1[l], w2[l], _tile(seq, 512), 1024)

        if not last:
            att_c = _attention(p_ctx, att_q_norm[l], cos, sin, [(k_c, v_c)], False, _tile(ctx_len, 256))
            xc = _merge(xc, mod[l], ctx_row, ogf_c, ogb_c, ysf_c, ysb_c, xbc_c, att_c, p_ctx,
                        gdn_norm[l], ssd_norm[l], ssd_d[l], wg[l], ws[l], wa[l], wo[l], _tile(ctx_len, 256))
            xc = _mlp(xc, mod[l], ctx_row, norm_mlp[l], w1[l], w2[l], _tile(ctx_len, 512), 1024)
    return x
```

```python
import functools

import jax
import jax.numpy as jnp
from jax import lax
from jax.experimental import pallas as pl
from jax.experimental.pallas import tpu as pltpu

F32 = jnp.float32
BF16 = jnp.bfloat16

D_MODEL = 1024
GRID_W = 64
EPS = 1e-6

GDN_HEADS = 8
GDN_DK = 128
GDN_DV = 128
GDN_CHUNK = 64
TRI_HI_LEVELS = 0
GDN_QK_DIM = GDN_HEADS * GDN_DK
GDN_V_DIM = GDN_HEADS * GDN_DV
GDN_QKV = 2 * GDN_QK_DIM + GDN_V_DIM

SSD_D_INNER = D_MODEL
SSD_HEAD_DIM = 64
SSD_HEADS = SSD_D_INNER // SSD_HEAD_DIM
SSD_GROUPS = 2
SSD_HPG = SSD_HEADS // SSD_GROUPS
SSD_STATE = 128
SSD_CHUNK = 128
SSD_XBC = SSD_D_INNER + 2 * SSD_GROUPS * SSD_STATE

ATT_HEADS = 8
ATT_KV_HEADS = 2
ATT_REP = ATT_HEADS // ATT_KV_HEADS
ATT_HEAD_DIM = 128
ATT_Q_DIM = ATT_HEADS * ATT_HEAD_DIM
ATT_KV_DIM = ATT_KV_HEADS * ATT_HEAD_DIM
ROPE_THETA = 10000.0
LOG2E = 1.4426950408889634
D_FF = 4 * D_MODEL

LANES = 128
HALO = 16
MOD_ROWS = 16

C_QKV = 0
C_GZ = C_QKV + GDN_QKV
C_SZ = C_GZ + GDN_V_DIM
C_AQ = C_SZ + SSD_D_INNER
C_GATE = C_AQ + ATT_Q_DIM
C_XBC = C_GATE + 3 * D_MODEL
C_AKV = C_XBC + SSD_XBC
C_SMALL = C_AKV + 2 * ATT_KV_DIM
NP_COLS = 11520
N_TILES = 6
SM_BETA = 0
SM_A = 2 * GDN_HEADS
SM_DT = 4 * GDN_HEADS

VMEM_LIMIT = 56 * 1024 * 1024


def _cparams(sem):
    return pltpu.CompilerParams(dimension_semantics=sem, vmem_limit_bytes=VMEM_LIMIT)


def _mm(a, b):
    return jnp.dot(a.astype(BF16), b.astype(BF16), preferred_element_type=F32)


def _mm_nt(a, b):
    return lax.dot_general(a.astype(BF16), b.astype(BF16), (((1,), (1,)), ((), ())),
                           preferred_element_type=F32)


def _mm_tn(a, b):
    return lax.dot_general(a.astype(BF16), b.astype(BF16), (((0,), (0,)), ((), ())),
                           preferred_element_type=F32)


def _split3(x):
    hi = x.astype(BF16)
    r = x - hi.astype(F32)
    mid = r.astype(BF16)
    lo = (r - mid.astype(F32)).astype(BF16)
    return hi, mid, lo


def _mm_sel_l(sel, x):
    hi, mid, lo = _split3(x)
    d = lambda p: jnp.dot(sel, p, preferred_element_type=F32)
    return (d(hi) + d(mid)) + d(lo)


def _mm_sel_r(x, sel):
    hi, mid, lo = _split3(x)
    d = lambda p: jnp.dot(p, sel, preferred_element_type=F32)
    return (d(hi) + d(mid)) + d(lo)


def _mm_x3(a, b):
    ah = a.astype(BF16)
    al = (a - ah.astype(F32)).astype(BF16)
    bh = b.astype(BF16)
    bl = (b - bh.astype(F32)).astype(BF16)
    d = lambda p, q: jnp.dot(p, q, preferred_element_type=F32)
    return d(ah, bh) + (d(ah, bl) + d(al, bh))


def _sigmoid(x):
    return 1.0 / (1.0 + jnp.exp(-x))


def _silu(x):
    return x * _sigmoid(x)


def _softplus(x):
    return jnp.maximum(x, 0.0) + jnp.log(1.0 + jnp.exp(-jnp.abs(x)))


def _rms(x, w):
    return x * lax.rsqrt(jnp.mean(x * x, axis=-1, keepdims=True) + EPS) * w


def _norm_mod(x, nw, scale, shift):
    return _rms(x, nw) * (1.0 + scale) + shift


def _tri(n, lower):
    i = lax.broadcasted_iota(jnp.int32, (n, n), 0)
    j = lax.broadcasted_iota(jnp.int32, (n, n), 1)
    return (i >= j) if lower else (i <= j)


def _rope(x, cos, sin):
    lane = lax.broadcasted_iota(jnp.int32, x.shape, 1)
    q = LANES // 4
    swapped = jnp.where((lane & q) == 0, pltpu.roll(x, LANES - q, 1), pltpu.roll(x, q, 1))
    return x * cos + swapped * sin


def _mod_kernel(c_ref, w_ref, b_ref, o_ref):
    o_ref[0] = _mm(_silu(c_ref[...]), w_ref[0]) + b_ref[0]


def _modulation(cc, w_mod, b_mod):
    depth = w_mod.shape[0]
    n = w_mod.shape[2]
    tn = D_MODEL
    return pl.pallas_call(
        _mod_kernel,
        grid=(depth, n // tn),
        in_specs=[pl.BlockSpec((MOD_ROWS, D_MODEL), lambda l, j: (0, 0)),
                  pl.BlockSpec((1, D_MODEL, tn), lambda l, j: (l, 0, j)),
                  pl.BlockSpec((1, 1, tn), lambda l, j: (l, 0, j))],
        out_specs=pl.BlockSpec((1, MOD_ROWS, tn), lambda l, j: (l, 0, j)),
        out_shape=jax.ShapeDtypeStruct((depth, MOD_ROWS, n), F32),
        compiler_params=_cparams(("arbitrary", "arbitrary")),
        name="modulation",
    )(cc, w_mod, b_mod.reshape(depth, 1, n))


def _inproj_kernel(x_ref, mod_ref, nw_ref, w_ref, o_ref, sm_ref, h_ref):
    j = pl.program_id(2)

    @pl.when(j == 0)
    def _():
        m = mod_ref[0]
        h_ref[...] = _norm_mod(x_ref[0], nw_ref[...], m[1:2], m[0:1]).astype(BF16)

    p = jnp.dot(h_ref[...], w_ref[...], preferred_element_type=F32)
    o_ref[0] = p.astype(o_ref.dtype)

    @pl.when(j == pl.num_programs(2) - 1)
    def _():
        off = C_SMALL - (NP_COLS // N_TILES) * (N_TILES - 1)
        sm_ref[0] = p[:, off:off + LANES]


def _in_projection(x, mod_l, mod_row, nw, w_r, layer, tm):
    bsz, seq, d = x.shape
    tn = NP_COLS // N_TILES
    return pl.pallas_call(
        _inproj_kernel,
        grid=(bsz, seq // tm, N_TILES),
        in_specs=[pl.BlockSpec((1, tm, d), lambda b, i, j: (b, i, 0)),
                  pl.BlockSpec((1, 6, d), lambda b, i, j: (mod_row(b), 0, 0)),
                  pl.BlockSpec((1, d), lambda b, i, j: (0, 0)),
                  pl.BlockSpec((None, d, tn), lambda b, i, j: (layer, 0, j))],
        out_specs=[pl.BlockSpec((1, tm, tn), lambda b, i, j: (b, i, j)),
                   pl.BlockSpec((1, tm, LANES), lambda b, i, j: (b, i, 0))],
        out_shape=[jax.ShapeDtypeStruct((bsz, seq, NP_COLS), BF16),
                   jax.ShapeDtypeStruct((bsz, seq, LANES), F32)],
        scratch_shapes=[pltpu.VMEM((tm, d), BF16)],
        compiler_params=_cparams(("arbitrary", "arbitrary", "arbitrary")),
        name="in_projection",
    )(x, mod_l, nw.reshape(1, d), w_r)


def _conv3(x, prev_row, next_row, w):
    ts = x.shape[0]
    rid = lax.broadcasted_iota(jnp.int32, x.shape, 0)
    xm1 = jnp.where(rid == 0, prev_row, pltpu.roll(x, 1, 0))
    xp1 = jnp.where(rid == ts - 1, next_row, pltpu.roll(x, ts - 1, 0))
    return w[0:1] * xm1 + w[1:2] * x + w[2:3] * xp1


def _conv_halo(xp_ref, xn_ref):
    i = pl.program_id(1)
    prev_row = jnp.where(i == 0, 0.0, xp_ref[0, HALO - 1:HALO, :].astype(F32))
    next_row = jnp.where(i == pl.num_programs(1) - 1, 0.0, xn_ref[0, 0:1, :].astype(F32))
    return prev_row, next_row


def _gdn_conv_kernel(x_ref, xp_ref, xn_ref, w_ref, o_ref):
    prev_row, next_row = _conv_halo(xp_ref, xn_ref)
    y = _silu(_conv3(x_ref[0].astype(F32), prev_row, next_row, w_ref[...]))
    j = pl.program_id(2)
    parts = []
    for h in range(GDN_HEADS):
        yh = y[:, h * GDN_DK:(h + 1) * GDN_DK]
        parts.append(yh * lax.rsqrt(jnp.sum(yh * yh, axis=-1, keepdims=True) + EPS))
    yn = jnp.concatenate(parts, axis=-1)
    yn = yn * jnp.where(j == 0, GDN_DK ** -0.5, 1.0)
    o_ref[0] = jnp.where(j < 2, yn, y).astype(o_ref.dtype)


def _ssd_conv_kernel(x_ref, xp_ref, xn_ref, w_ref, b_ref, o_ref):
    prev_row, next_row = _conv_halo(xp_ref, xn_ref)
    o_ref[0] = _silu(_conv3(x_ref[0].astype(F32), prev_row, next_row, w_ref[...]) + b_ref[...]).astype(o_ref.dtype)


def _short_conv(p, col0, width, tc, ts, conv_w, conv_b):
    bsz, seq, _ = p.shape
    cb0 = col0 // tc
    nrb = seq // HALO
    rpb = ts // HALO
    x_specs = [pl.BlockSpec((1, ts, tc), lambda b, i, j: (b, i, cb0 + j)),
               pl.BlockSpec((1, HALO, tc), lambda b, i, j: (b, jnp.maximum(i * rpb - 1, 0), cb0 + j)),
               pl.BlockSpec((1, HALO, tc), lambda b, i, j: (b, jnp.minimum((i + 1) * rpb, nrb - 1), cb0 + j)),
               pl.BlockSpec((3, tc), lambda b, i, j: (0, j))]
    args = [p, p, p, conv_w]
    if conv_b is None:
        body = _gdn_conv_kernel
    else:
        body = _ssd_conv_kernel
        x_specs.append(pl.BlockSpec((1, tc), lambda b, i, j: (0, j)))
        args.append(conv_b.reshape(1, width))
    return pl.pallas_call(
        body,
        grid=(bsz, seq // ts, width // tc),
        in_specs=x_specs,
        out_specs=pl.BlockSpec((1, ts, tc), lambda b, i, j: (b, i, j)),
        out_shape=jax.ShapeDtypeStruct((bsz, seq, width), BF16),
        compiler_params=_cparams(("arbitrary", "arbitrary", "arbitrary")),
        name="short_conv",
    )(*args)


def _unit_tri_inverses(mats):
    n = mats[0].shape[0]
    eye = (lax.broadcasted_iota(jnp.int32, (n, n), 0) == lax.broadcasted_iota(jnp.int32, (n, n), 1)).astype(F32)
    invs = [eye - a for a in mats]
    powers = list(mats)
    for level in range(n.bit_length() - 2):
        mm = _mm_x3 if level < TRI_HI_LEVELS else _mm
        powers = [mm(p, p) for p in powers]
        invs = [i + mm(i, p) for i, p in zip(invs, powers)]
    return invs


def _gdn_chunks(chains):
    c = chains[0][0].shape[0]
    ii = lax.broadcasted_iota(jnp.int32, (c, c), 0)
    jj = lax.broadcasted_iota(jnp.int32, (c, c), 1)
    masks = {True: (ii >= jj, ii > jj), False: (ii <= jj, ii < jj)}
    decays = [jnp.exp(jnp.where(masks[fwd][0], gc - gr, -jnp.inf)) for (_, _, _, _, gc, gr, _, fwd) in chains]
    kbs = [k * beta for (_, k, _, beta, _, _, _, _) in chains]
    a_mats = [jnp.where(masks[ch[7]][1], _mm_nt(kb, ch[1]) * dec, 0.0)
              for ch, kb, dec in zip(chains, kbs, decays)]
    qks = [jnp.where(masks[ch[7]][0], _mm_nt(ch[0], ch[1]) * dec, 0.0) for ch, dec in zip(chains, decays)]
    invs = _unit_tri_inverses(a_mats)
    egs = [jnp.exp(ch[4]) for ch in chains]
    sols = [_mm(inv, jnp.concatenate([ch[2] * ch[3], kb * eg], axis=-1))
            for inv, ch, kb, eg in zip(invs, chains, kbs, egs)]
    g_lasts = [ch[4][c - 1:c] if ch[7] else ch[4][0:1] for ch in chains]
    v_news = [sol[:, :GDN_DV] - _mm(sol[:, GDN_DV:], ch[6]) for sol, ch in zip(sols, chains)]
    outs = [_mm(ch[0] * eg, ch[6]) + _mm(qk, vn) for ch, eg, qk, vn in zip(chains, egs, qks, v_news)]
    states = [ch[6] * jnp.exp(gl) + _mm_tn(ch[1] * jnp.exp(gl - ch[4]), vn)
              for ch, gl, vn in zip(chains, g_lasts, v_news)]
    return list(zip(outs, states))


def _gdn_gates(s_ref, r_ref, alog_l, dtb_l, alog_s, dtb_s, fwd):
    c = GDN_CHUNK
    raw = s_ref[0]
    beta = _sigmoid(raw)
    la = -jnp.exp(alog_l) * _softplus(raw + dtb_l)
    rraw = r_ref[0, 0]
    la_r = -jnp.exp(alog_s) * _softplus(rraw + dtb_s)
    low = _tri(c, True).astype(BF16)
    up = _tri(c, False).astype(BF16)
    if fwd:
        return beta, _mm_sel_l(low, la), _mm_sel_r(la_r, up)
    return beta, _mm_sel_l(up, la), _mm_sel_r(la_r, low)


def _gdn_kernel(qf_ref, kf_ref, vf_ref, sf_ref, rf_ref, qb_ref, kb_ref, vb_ref, sb_ref, rb_ref,
                alog_l_ref, dtb_l_ref, alog_s_ref, dtb_s_ref, s0f_ref, s0b_ref,
                of_ref, ob_ref, stf_ref, stb_ref):
    @pl.when(pl.program_id(1) == 0)
    def _():
        stf_ref[...] = s0f_ref[...]
        stb_ref[...] = s0b_ref[...]

    alog_l, dtb_l, alog_s, dtb_s = alog_l_ref[...], dtb_l_ref[...], alog_s_ref[...], dtb_s_ref[...]
    chains, sinks = [], []
    for d, (q_ref, k_ref, v_ref, s_ref, r_ref, o_ref, st_ref) in enumerate(
            ((qf_ref, kf_ref, vf_ref, sf_ref, rf_ref, of_ref, stf_ref),
             (qb_ref, kb_ref, vb_ref, sb_ref, rb_ref, ob_ref, stb_ref))):
        fwd = d == 0
        beta_all, gc_all, gr_all = _gdn_gates(s_ref, r_ref, alog_l, dtb_l, alog_s, dtb_s, fwd)
        for h in range(GDN_HEADS):
            lb = SM_BETA + d * GDN_HEADS + h
            la = SM_A + d * GDN_HEADS + h
            sl = slice(h * GDN_DK, (h + 1) * GDN_DK)
            chains.append((q_ref[0, :, sl], k_ref[0, :, sl], v_ref[0, :, sl],
                           beta_all[:, lb:lb + 1], gc_all[:, la:la + 1], gr_all[la:la + 1, :],
                           st_ref[0, h], fwd))
            sinks.append((o_ref, st_ref, h, sl))
    for (o, st), (o_ref, st_ref, h, sl) in zip(_gdn_chunks(chains), sinks):
        o_ref[0, :, sl] = o
        st_ref[0, h] = st


def _lane_param(vals, offset):
    flat = vals.reshape(-1).astype(F32)
    v = jnp.zeros((LANES,), F32).at[offset:offset + flat.shape[0]].set(flat)
    return v.reshape(1, LANES), v.reshape(LANES, 1)


def _small_rows(small, chunk):
    bsz, seq, _ = small.shape
    return jnp.swapaxes(small.reshape(bsz, seq // chunk, chunk, LANES), 2, 3)


def _gdn_scan(qkv, small, a_log, dt_bias, s0f, s0b):
    bsz, seq, _ = qkv.shape
    c = GDN_CHUNK
    nc = seq // c
    rows = _small_rows(small, c)
    alog_l, alog_s = _lane_param(a_log, SM_A)
    dtb_l, dtb_s = _lane_param(dt_bias, SM_A)
    hw = GDN_QK_DIM

    def chunk_specs(cmap):
        return [pl.BlockSpec((1, c, hw), lambda b, i: (b, cmap(i), 0)),
                pl.BlockSpec((1, c, hw), lambda b, i: (b, cmap(i), 1)),
                pl.BlockSpec((1, c, hw), lambda b, i: (b, cmap(i), 2)),
                pl.BlockSpec((1, c, LANES), lambda b, i: (b, cmap(i), 0)),
                pl.BlockSpec((1, 1, LANES, c), lambda b, i: (b, cmap(i), 0, 0))]

    fw = lambda i: i
    bw = lambda i: nc - 1 - i
    vec_l = pl.BlockSpec((1, LANES), lambda b, i: (0, 0))
    vec_s = pl.BlockSpec((LANES, 1), lambda b, i: (0, 0))
    st_spec = pl.BlockSpec((1, GDN_HEADS, GDN_DK, GDN_DV), lambda b, i: (b, 0, 0, 0))
    out_shape = [jax.ShapeDtypeStruct((bsz, seq, GDN_V_DIM), F32)] * 2 + \
                [jax.ShapeDtypeStruct((bsz, GDN_HEADS, GDN_DK, GDN_DV), F32)] * 2
    return pl.pallas_call(
        _gdn_kernel,
        grid=(bsz, nc),
        in_specs=chunk_specs(fw) + chunk_specs(bw) + [vec_l, vec_l, vec_s, vec_s, st_spec, st_spec],
        out_specs=[pl.BlockSpec((1, c, GDN_V_DIM), lambda b, i: (b, i, 0)),
                   pl.BlockSpec((1, c, GDN_V_DIM), lambda b, i: (b, nc - 1 - i, 0)),
                   st_spec, st_spec],
        out_shape=out_shape,
        compiler_params=_cparams(("arbitrary", "arbitrary")),
        name="gdn_scan",
    )(qkv, qkv, qkv, small, rows, qkv, qkv, qkv, small, rows, alog_l, dtb_l, alog_s, dtb_s, s0f, s0b)


def _ssd_gates(s_ref, r_ref, alog_l, dtb_l, alog_s, dtb_s, fwd):
    c = SSD_CHUNK
    da = _softplus(s_ref[0] + dtb_l) * (-jnp.exp(alog_l))
    dt_r = _softplus(r_ref[0, 0] + dtb_s)
    da_r = dt_r * (-jnp.exp(alog_s))
    low = _tri(c, True).astype(BF16)
    up = _tri(c, False).astype(BF16)
    if fwd:
        return _mm_sel_l(low, da), dt_r, _mm_sel_r(da_r, up)
    return _mm_sel_l(up, da), dt_r, _mm_sel_r(da_r, low)


def _ssd_kernel(xf_ref, bf_ref, cf_ref, sf_ref, rf_ref, xb_ref, bb_ref, cb_ref, sb_ref, rb_ref,
                alog_l_ref, dtb_l_ref, alog_s_ref, dtb_s_ref, s0f_ref, s0b_ref,
                yf_ref, yb_ref, stf_ref, stb_ref):
    @pl.when(pl.program_id(1) == 0)
    def _():
        stf_ref[...] = s0f_ref[...]
        stb_ref[...] = s0b_ref[...]

    c = SSD_CHUNK
    alog_l, dtb_l, alog_s, dtb_s = alog_l_ref[...], dtb_l_ref[...], alog_s_ref[...], dtb_s_ref[...]
    chains = []
    for d, (x_ref, b_ref, c_ref, s_ref, r_ref, y_ref, st_ref) in enumerate(
            ((xf_ref, bf_ref, cf_ref, sf_ref, rf_ref, yf_ref, stf_ref),
             (xb_ref, bb_ref, cb_ref, sb_ref, rb_ref, yb_ref, stb_ref))):
        fwd = d == 0
        incl = _tri(c, fwd)
        ac_all, dtr_all, ar_all = _ssd_gates(s_ref, r_ref, alog_l, dtb_l, alog_s, dtb_s, fwd)
        for g in range(SSD_GROUPS):
            bm = b_ref[0, :, g * SSD_STATE:(g + 1) * SSD_STATE].astype(F32)
            cm = c_ref[0, :, g * SSD_STATE:(g + 1) * SSD_STATE].astype(F32)
            cbm = _mm_nt(cm, bm)
            bm_t = bm.T
            for e in range(SSD_HPG):
                hid = g * SSD_HPG + e
                ln = SM_DT + d * SSD_HEADS + hid
                sl = slice(hid * SSD_HEAD_DIM, (hid + 1) * SSD_HEAD_DIM)
                ar = ar_all[ln:ln + 1, :]
                chains.append(dict(incl=incl, cm=cm, cbm=cbm, bm_t=bm_t, ar=ar, dtr=dtr_all[ln:ln + 1, :],
                                   acol=jnp.broadcast_to(ac_all[:, ln:ln + 1], (c, c)),
                                   a_last=ar[:, c - 1:c] if fwd else ar[:, 0:1],
                                   x=x_ref[0, :, sl], state=st_ref[0, hid],
                                   y_ref=y_ref, st_ref=st_ref, hid=hid, sl=sl))
    lhs = [jnp.concatenate(
        [(ch["cbm"] * jnp.exp(jnp.where(ch["incl"], ch["acol"] - ch["ar"], -jnp.inf)) * ch["dtr"]).astype(BF16),
         (ch["cm"] * jnp.exp(ch["acol"])).astype(BF16)], axis=-1) for ch in chains]
    rhs = [jnp.concatenate([ch["x"].astype(BF16), ch["state"].astype(BF16)], axis=0) for ch in chains]
    ys = [jnp.dot(l, r, preferred_element_type=F32) for l, r in zip(lhs, rhs)]
    sts = [ch["state"] * jnp.exp(ch["a_last"])
           + _mm(ch["bm_t"] * (ch["dtr"] * jnp.exp(ch["a_last"] - ch["ar"])), ch["x"]) for ch in chains]
    for ch, y, st in zip(chains, ys, sts):
        ch["y_ref"][0, :, ch["sl"]] = y
        ch["st_ref"][0, ch["hid"]] = st


def _ssd_scan(xbc, small, a_log, dt_bias, s0f, s0b):
    bsz, seq, _ = xbc.shape
    c = SSD_CHUNK
    nc = seq // c
    rows = _small_rows(small, c)
    alog_l, alog_s = _lane_param(a_log, SM_DT)
    dtb_l, dtb_s = _lane_param(dt_bias, SM_DT)
    gn = SSD_GROUPS * SSD_STATE

    def chunk_specs(cmap):
        return [pl.BlockSpec((1, c, SSD_D_INNER), lambda b, i: (b, cmap(i), 0)),
                pl.BlockSpec((1, c, gn), lambda b, i: (b, cmap(i), SSD_D_INNER // gn)),
                pl.BlockSpec((1, c, gn), lambda b, i: (b, cmap(i), SSD_D_INNER // gn + 1)),
                pl.BlockSpec((1, c, LANES), lambda b, i: (b, cmap(i), 0)),
                pl.BlockSpec((1, 1, LANES, c), lambda b, i: (b, cmap(i), 0, 0))]

    fw = lambda i: i
    bw = lambda i: nc - 1 - i
    vec_l = pl.BlockSpec((1, LANES), lambda b, i: (0, 0))
    vec_s = pl.BlockSpec((LANES, 1), lambda b, i: (0, 0))
    st_spec = pl.BlockSpec((1, SSD_HEADS, SSD_STATE, SSD_HEAD_DIM), lambda b, i: (b, 0, 0, 0))
    out_shape = [jax.ShapeDtypeStruct((bsz, seq, SSD_D_INNER), F32)] * 2 + \
                [jax.ShapeDtypeStruct((bsz, SSD_HEADS, SSD_STATE, SSD_HEAD_DIM), F32)] * 2
    return pl.pallas_call(
        _ssd_kernel,
        grid=(bsz, nc),
        in_specs=chunk_specs(fw) + chunk_specs(bw) + [vec_l, vec_l, vec_s, vec_s, st_spec, st_spec],
        out_specs=[pl.BlockSpec((1, c, SSD_D_INNER), lambda b, i: (b, i, 0)),
                   pl.BlockSpec((1, c, SSD_D_INNER), lambda b, i: (b, nc - 1 - i, 0)),
                   st_spec, st_spec],
        out_shape=out_shape,
        compiler_params=_cparams(("arbitrary", "arbitrary")),
        name="ssd_scan",
    )(xbc, xbc, xbc, small, rows, xbc, xbc, xbc, small, rows, alog_l, dtb_l, alog_s, dtb_s, s0f, s0b)


def _kv_prep_kernel(k_ref, v_ref, nw_ref, cos_ref, sin_ref, ko_ref, vo_ref, *, rope):
    parts = []
    for h in range(ATT_KV_HEADS):
        kh = _rms(k_ref[0, :, h * ATT_HEAD_DIM:(h + 1) * ATT_HEAD_DIM].astype(F32), nw_ref[...])
        if rope:
            kh = _rope(kh, cos_ref[...], sin_ref[...])
        parts.append(kh)
    ko_ref[0] = jnp.concatenate(parts, axis=-1).astype(BF16)
    v = v_ref[0]
    ones_blk = (lax.broadcasted_iota(jnp.int32, (v.shape[0], ATT_HEAD_DIM), 1) == 0).astype(BF16)
    vparts = []
    for h in range(ATT_KV_HEADS):
        vparts += [v[:, h * ATT_HEAD_DIM:(h + 1) * ATT_HEAD_DIM].astype(BF16), ones_blk]
    vo_ref[0] = jnp.concatenate(vparts, axis=-1)


def _kv_prep(p, k_norm, cos, sin, rope, ts):
    bsz, seq, _ = p.shape
    kb = C_AKV // ATT_KV_DIM
    tab = pl.BlockSpec((ts, ATT_HEAD_DIM), lambda b, i: (i if rope else 0, 0))
    return pl.pallas_call(
        functools.partial(_kv_prep_kernel, rope=rope),
        grid=(bsz, seq // ts),
        in_specs=[pl.BlockSpec((1, ts, ATT_KV_DIM), lambda b, i: (b, i, kb)),
                  pl.BlockSpec((1, ts, ATT_KV_DIM), lambda b, i: (b, i, kb + 1)),
                  pl.BlockSpec((1, ATT_HEAD_DIM), lambda b, i: (0, 0)), tab, tab],
        out_specs=[pl.BlockSpec((1, ts, ATT_KV_DIM), lambda b, i: (b, i, 0)),
                   pl.BlockSpec((1, ts, 2 * ATT_KV_DIM), lambda b, i: (b, i, 0))],
        out_shape=[jax.ShapeDtypeStruct((bsz, seq, ATT_KV_DIM), BF16),
                   jax.ShapeDtypeStruct((bsz, seq, 2 * ATT_KV_DIM), BF16)],
        compiler_params=_cparams(("arbitrary", "arbitrary")),
        name="kv_prep",
    )(p, p, k_norm.reshape(1, ATT_HEAD_DIM), cos, sin)


def _attn_kernel(*refs, n_seg, rope):
    q_ref, nw_ref, cos_ref, sin_ref = refs[:4]
    kv_refs = refs[4:4 + 2 * n_seg]
    o_ref = refs[4 + 2 * n_seg]
    qscale = ATT_HEAD_DIM ** -0.5 * LOG2E
    for r in range(ATT_REP):
        sl = slice(r * ATT_HEAD_DIM, (r + 1) * ATT_HEAD_DIM)
        qh = _rms(q_ref[0, :, sl].astype(F32), nw_ref[...])
        if rope:
            qh = _rope(qh, cos_ref[...], sin_ref[...])
        qh = (qh * qscale).astype(BF16)
        scores = [_mm_nt(qh, kv_refs[2 * s][0]) for s in range(n_seg)]
        m = functools.reduce(jnp.maximum, [jnp.max(sc, axis=-1, keepdims=True) for sc in scores])
        acc = functools.reduce(jnp.add, [_mm(jnp.exp2(scores[s] - m), kv_refs[2 * s + 1][0]) for s in range(n_seg)])
        o_ref[0, :, sl] = acc[:, :ATT_HEAD_DIM] / acc[:, ATT_HEAD_DIM:ATT_HEAD_DIM + 1]


def _attention(p, q_norm, cos, sin, kv_segs, rope, tq):
    bsz, seq, _ = p.shape
    gw = ATT_REP * ATT_HEAD_DIM
    qb = C_AQ // gw
    n_seg = len(kv_segs)
    tab = pl.BlockSpec((tq, ATT_HEAD_DIM), lambda b, g, i: (i if rope else 0, 0))
    in_specs = [pl.BlockSpec((1, tq, gw), lambda b, g, i: (b, i, qb + g)),
                pl.BlockSpec((1, ATT_HEAD_DIM), lambda b, g, i: (0, 0)), tab, tab]
    args = [p, q_norm.reshape(1, ATT_HEAD_DIM), cos, sin]
    for k_arr, v_arr in kv_segs:
        lk = k_arr.shape[1]
        in_specs += [pl.BlockSpec((1, lk, ATT_HEAD_DIM), lambda b, g, i: (b, 0, g)),
                     pl.BlockSpec((1, lk, 2 * ATT_HEAD_DIM), lambda b, g, i: (b, 0, g))]
        args += [k_arr, v_arr]
    return pl.pallas_call(
        functools.partial(_attn_kernel, n_seg=n_seg, rope=rope),
        grid=(bsz, ATT_KV_HEADS, seq // tq),
        in_specs=in_specs,
        out_specs=pl.BlockSpec((1, tq, gw), lambda b, g, i: (b, i, g)),
        out_shape=jax.ShapeDtypeStruct((bsz, seq, ATT_Q_DIM), F32),
        compiler_params=_cparams(("arbitrary", "arbitrary", "arbitrary")),
        name="attention",
    )(*args)


def _merge_kernel(x_ref, mod_ref, of_ref, ob_ref, gz_ref, yf_ref, yb_ref, xs_ref, sz_ref, att_ref, gate_ref,
                  gnw_ref, snw_ref, dsk_ref, wg_ref, ws_ref, wa_ref, wo_ref, o_ref):
    o = of_ref[0] + ob_ref[0]
    gz = gz_ref[0].astype(F32)
    parts = []
    for h in range(GDN_HEADS):
        sl = slice(h * GDN_DV, (h + 1) * GDN_DV)
        parts.append(_rms(o[:, sl], gnw_ref[...]) * _silu(gz[:, sl]))
    y_gdn = jnp.concatenate(parts, axis=-1)

    y = yf_ref[0] + yb_ref[0] + dsk_ref[...] * xs_ref[0].astype(F32)
    y = y * _silu(sz_ref[0].astype(F32))
    snw = snw_ref[...]
    gw = SSD_D_INNER // SSD_GROUPS
    y_ssd = jnp.concatenate([_rms(y[:, g * gw:(g + 1) * gw], snw[:, g * gw:(g + 1) * gw])
                             for g in range(SSD_GROUPS)], axis=-1)

    gates = _sigmoid(gate_ref[0].astype(F32))
    d = D_MODEL
    m = (gates[:, :d] * _mm(y_gdn, wg_ref[...]) + gates[:, d:2 * d] * _mm(y_ssd, ws_ref[...])
         + gates[:, 2 * d:] * _mm(att_ref[0], wa_ref[...]))
    g1 = mod_ref[0][2:3]
    o_ref[0] = x_ref[0] + g1 * _mm(m, wo_ref[...])


def _merge(x, mod_l, mod_row, o_f, o_b, y_f, y_b, xbc, att, p, gdn_norm, ssd_norm, ssd_d, wg, ws, wa, wo, layer, tm):
    bsz, seq, d = x.shape
    row = lambda cb: pl.BlockSpec((1, tm, d), lambda b, i: (b, i, cb))
    vec = lambda n: pl.BlockSpec((1, n), lambda b, i: (0, 0))
    wsp = pl.BlockSpec((None, d, d), lambda b, i: (layer, 0, 0))
    return pl.pallas_call(
        _merge_kernel,
        grid=(bsz, seq // tm),
        in_specs=[row(0), pl.BlockSpec((1, 6, d), lambda b, i: (mod_row(b), 0, 0)),
                  row(0), row(0), row(C_GZ // d), row(0), row(0), row(0), row(C_SZ // d), row(0),
                  pl.BlockSpec((1, tm, 3 * d), lambda b, i: (b, i, C_GATE // (3 * d))),
                  vec(GDN_DV), vec(d), vec(d), wsp, wsp, wsp, wsp],
        out_specs=row(0),
        out_shape=jax.ShapeDtypeStruct((bsz, seq, d), F32),
        compiler_params=_cparams(("arbitrary", "arbitrary")),
        name="merge",
    )(x, mod_l, o_f, o_b, p, y_f, y_b, xbc, p, att, p,
      gdn_norm.reshape(1, GDN_DV), ssd_norm.reshape(1, d),
      jnp.repeat(ssd_d, SSD_HEAD_DIM).reshape(1, d), wg, ws, wa, wo)


def _mlp_kernel(x_ref, mod_ref, nw_ref, w1_ref, w2_ref, o_ref, h_ref, acc_ref):
    k = pl.program_id(2)
    m = mod_ref[0]

    @pl.when(k == 0)
    def _():
        h_ref[...] = _norm_mod(x_ref[0], nw_ref[...], m[4:5], m[3:4]).astype(BF16)
        acc_ref[...] = jnp.zeros_like(acc_ref)

    a = jnp.maximum(jnp.dot(h_ref[...], w1_ref[...], preferred_element_type=F32), 0.0)
    acc_ref[...] += _mm(a * a, w2_ref[...])

    @pl.when(k == pl.num_programs(2) - 1)
    def _():
        o_ref[0] = x_ref[0] + m[5:6] * acc_ref[...]


def _mlp(x, mod_l, mod_row, nw, w1, w2, layer, tm, tf):
    bsz, seq, d = x.shape
    return pl.pallas_call(
        _mlp_kernel,
        grid=(bsz, seq // tm, D_FF // tf),
        in_specs=[pl.BlockSpec((1, tm, d), lambda b, i, k: (b, i, 0)),
                  pl.BlockSpec((1, 6, d), lambda b, i, k: (mod_row(b), 0, 0)),
                  pl.BlockSpec((1, d), lambda b, i, k: (0, 0)),
                  pl.BlockSpec((None, d, tf), lambda b, i, k: (layer, 0, k)),
                  pl.BlockSpec((None, tf, d), lambda b, i, k: (layer, k, 0))],
        out_specs=pl.BlockSpec((1, tm, d), lambda b, i, k: (b, i, 0)),
        out_shape=jax.ShapeDtypeStruct((bsz, seq, d), F32),
        scratch_shapes=[pltpu.VMEM((tm, d), BF16), pltpu.VMEM((tm, d), F32)],
        compiler_params=_cparams(("arbitrary", "arbitrary", "arbitrary")),
        name="mlp",
    )(x, mod_l, nw.reshape(1, d), w1, w2)


def _reorder_w_in(w_in):
    depth, d, _ = w_in.shape
    o = 0
    seg = {}
    for name, size in (("qkv", GDN_QKV), ("gz", GDN_V_DIM), ("beta", 2 * GDN_HEADS), ("a", 2 * GDN_HEADS),
                       ("sz", SSD_D_INNER), ("xbc", SSD_XBC), ("dt", 2 * SSD_HEADS),
                       ("aq", ATT_Q_DIM), ("akv", 2 * ATT_KV_DIM), ("gate", 3 * D_MODEL)):
        seg[name] = w_in[:, :, o:o + size]
        o += size
    pad = jnp.zeros((depth, d, NP_COLS - (C_SMALL + SM_DT + 2 * SSD_HEADS)), w_in.dtype)
    out = jnp.concatenate([seg["qkv"], seg["gz"], seg["sz"], seg["aq"], seg["gate"], seg["xbc"], seg["akv"],
                           seg["beta"], seg["a"], seg["dt"], pad], axis=-1)
    return out.astype(BF16)


def _rope_tables(seq):
    t = jnp.arange(seq, dtype=jnp.int32)
    q = ATT_HEAD_DIM // 4
    freqs = ROPE_THETA ** (-jnp.arange(q, dtype=F32) / q)
    ang_r = (t // GRID_W).astype(F32)[:, None] * freqs[None, :]
    ang_c = (t % GRID_W).astype(F32)[:, None] * freqs[None, :]
    cos = jnp.concatenate([jnp.cos(ang_r)] * 2 + [jnp.cos(ang_c)] * 2, axis=-1)
    sin = jnp.concatenate([-jnp.sin(ang_r), jnp.sin(ang_r), -jnp.sin(ang_c), jnp.sin(ang_c)], axis=-1)
    return cos, sin


def _tile(seq, want):
    return min(seq, want)


def kernel(x, c, ctx, c_ctx, w_mod, b_mod, norm_mix, norm_mlp, w_in, gdn_conv, gdn_a_log, gdn_dt_bias, gdn_norm,
           ssd_conv_w, ssd_conv_b, ssd_a_log, ssd_dt_bias, ssd_d, ssd_norm, att_q_norm, att_k_norm,
           w_br_gdn, w_br_ssd, w_br_att, w_out, w_ff1, w_ff2):
    bsz, seq, d = x.shape
    ctx_len = ctx.shape[1]
    depth = w_in.shape[0]
    assert bsz < MOD_ROWS and d == D_MODEL

    cc = jnp.zeros((MOD_ROWS, d), F32).at[:bsz].set(c).at[bsz].set(c_ctx)
    mod = _modulation(cc, w_mod, b_mod).reshape(depth, MOD_ROWS, 6, d)
    lat_row = lambda b: b
    ctx_row = lambda b: bsz

    w_in_r = _reorder_w_in(w_in)
    wg, ws, wa, wo = (w.astype(BF16) for w in (w_br_gdn, w_br_ssd, w_br_att, w_out))
    w1, w2 = w_ff1.astype(BF16), w_ff2.astype(BF16)
    cos, sin = _rope_tables(seq)

    zg = jnp.zeros((bsz, GDN_HEADS, GDN_DK, GDN_DV), F32)
    zs = jnp.zeros((bsz, SSD_HEADS, SSD_STATE, SSD_HEAD_DIM), F32)
    xc = ctx
    for l in range(depth):
        last = l == depth - 1
        p_lat, sm_lat = _in_projection(x, mod[l], lat_row, norm_mix[l], w_in_r, l, _tile(seq, 1024))
        p_ctx, sm_ctx = _in_projection(xc, mod[l], ctx_row, norm_mix[l], w_in_r, l, _tile(ctx_len, 1024))

        qkv_c = _short_conv(p_ctx, C_QKV, GDN_QKV, GDN_QK_DIM, _tile(ctx_len, 256), gdn_conv[l], None)
        qkv_l = _short_conv(p_lat, C_QKV, GDN_QKV, GDN_QK_DIM, _tile(seq, 256), gdn_conv[l], None)
        xbc_c = _short_conv(p_ctx, C_XBC, SSD_XBC, 512, _tile(ctx_len, 512), ssd_conv_w[l], ssd_conv_b[l])
        xbc_l = _short_conv(p_lat, C_XBC, SSD_XBC, 512, _tile(seq, 512), ssd_conv_w[l], ssd_conv_b[l])

        ogf_c, ogb_c, sgf, sgb = _gdn_scan(qkv_c, sm_ctx, gdn_a_log[l], gdn_dt_bias[l], zg, zg)
        ogf_l, ogb_l, _, _ = _gdn_scan(qkv_l, sm_lat, gdn_a_log[l], gdn_dt_bias[l], sgf, sgb)
        ysf_c, ysb_c, ssf, ssb = _ssd_scan(xbc_c, sm_ctx, ssd_a_log[l], ssd_dt_bias[l], zs, zs)
        ysf_l, ysb_l, _, _ = _ssd_scan(xbc_l, sm_lat, ssd_a_log[l], ssd_dt_bias[l], ssf, ssb)

        k_c, v_c = _kv_prep(p_ctx, att_k_norm[l], cos, sin, False, _tile(ctx_len, 256))
        k_l, v_l = _kv_prep(p_lat, att_k_norm[l], cos, sin, True, _tile(seq, 512))
        att_l = _attention(p_lat, att_q_norm[l], cos, sin, [(k_c, v_c), (k_l, v_l)], True, _tile(seq, 256))

        x = _merge(x, mod[l], lat_row, ogf_l, ogb_l, ysf_l, ysb_l, xbc_l, att_l, p_lat,
                   gdn_norm[l], ssd_norm[l], ssd_d[l], wg, ws, wa, wo, l, _tile(seq, 256))
        x = _mlp(x, mod[l], lat_row, norm_mlp[l], w1, w2, l, _tile(seq, 1024), 1024)

        if not last:
            att_c = _attention(p_ctx, att_q_norm[l], cos, sin, [(k_c, v_c)], False, _tile(ctx_len, 256))
            xc = _merge(xc, mod[l], ctx_row, ogf_c, ogb_c, ysf_c, ysb_c, xbc_c, att_c, p_ctx,
                        gdn_norm[l], ssd_norm[l], ssd_d[l], wg, ws, wa, wo, l, _tile(ctx_len, 256))
            xc = _mlp(xc, mod[l], ctx_row, norm_mlp[l], w1, w2, l, _tile(ctx_len, 512), 1024)
    return x
```

```python
import functools

import jax
import jax.numpy as jnp
from jax import lax
from jax.experimental import pallas as pl
from jax.experimental.pallas import tpu as pltpu

F32 = jnp.float32
BF16 = jnp.bfloat16

D_MODEL = 1024
GRID_W = 64
EPS = 1e-6

GDN_HEADS = 8
GDN_DK = 128
GDN_DV = 128
GDN_CHUNK = 64
GDN_QK_DIM = GDN_HEADS * GDN_DK
GDN_V_DIM = GDN_HEADS * GDN_DV
GDN_QKV = 2 * GDN_QK_DIM + GDN_V_DIM

SSD_D_INNER = D_MODEL
SSD_HEAD_DIM = 64
SSD_HEADS = SSD_D_INNER // SSD_HEAD_DIM
SSD_GROUPS = 2
SSD_HPG = SSD_HEADS // SSD_GROUPS
SSD_STATE = 128
SSD_CHUNK = 128
SSD_XBC = SSD_D_INNER + 2 * SSD_GROUPS * SSD_STATE

ATT_HEADS = 8
ATT_KV_HEADS = 2
ATT_REP = ATT_HEADS // ATT_KV_HEADS
ATT_HEAD_DIM = 128
ATT_Q_DIM = ATT_HEADS * ATT_HEAD_DIM
ATT_KV_DIM = ATT_KV_HEADS * ATT_HEAD_DIM
ROPE_THETA = 10000.0
LOG2E = 1.4426950408889634
D_FF = 4 * D_MODEL

LANES = 128
HALO = 16
MOD_ROWS = 16

C_QKV = 0
C_GZ = C_QKV + GDN_QKV
C_SZ = C_GZ + GDN_V_DIM
C_AQ = C_SZ + SSD_D_INNER
C_GATE = C_AQ + ATT_Q_DIM
C_XBC = C_GATE + 3 * D_MODEL
C_AKV = C_XBC + SSD_XBC
C_SMALL = C_AKV + 2 * ATT_KV_DIM
NP_COLS = 11520
N_TILES = 6
SM_BETA = 0
SM_A = 2 * GDN_HEADS
SM_DT = 4 * GDN_HEADS

VMEM_LIMIT = 56 * 1024 * 1024


def _cparams(sem):
    return pltpu.CompilerParams(dimension_semantics=sem, vmem_limit_bytes=VMEM_LIMIT)


def _mm(a, b):
    return jnp.dot(a.astype(BF16), b.astype(BF16), preferred_element_type=F32)


def _mm_nt(a, b):
    return lax.dot_general(a.astype(BF16), b.astype(BF16), (((1,), (1,)), ((), ())),
                           preferred_element_type=F32)


def _mm_tn(a, b):
    return lax.dot_general(a.astype(BF16), b.astype(BF16), (((0,), (0,)), ((), ())),
                           preferred_element_type=F32)


def _split3(x):
    hi = x.astype(BF16)
    r = x - hi.astype(F32)
    mid = r.astype(BF16)
    lo = (r - mid.astype(F32)).astype(BF16)
    return hi, mid, lo


def _mm_sel_l(sel, x):
    hi, mid, lo = _split3(x)
    d = lambda p: jnp.dot(sel, p, preferred_element_type=F32)
    return (d(hi) + d(mid)) + d(lo)


def _mm_sel_r(x, sel):
    hi, mid, lo = _split3(x)
    d = lambda p: jnp.dot(p, sel, preferred_element_type=F32)
    return (d(hi) + d(mid)) + d(lo)


def _sigmoid(x):
    return 1.0 / (1.0 + jnp.exp(-x))


def _silu(x):
    return x * _sigmoid(x)


def _softplus(x):
    return jnp.maximum(x, 0.0) + jnp.log(1.0 + jnp.exp(-jnp.abs(x)))


def _rms(x, w):
    return x * lax.rsqrt(jnp.mean(x * x, axis=-1, keepdims=True) + EPS) * w


def _norm_mod(x, nw, scale, shift):
    return _rms(x, nw) * (1.0 + scale) + shift


def _tri(n, lower):
    i = lax.broadcasted_iota(jnp.int32, (n, n), 0)
    j = lax.broadcasted_iota(jnp.int32, (n, n), 1)
    return (i >= j) if lower else (i <= j)


def _rope(x, cos, sin):
    lane = lax.broadcasted_iota(jnp.int32, x.shape, 1)
    q = LANES // 4
    swapped = jnp.where((lane & q) == 0, pltpu.roll(x, LANES - q, 1), pltpu.roll(x, q, 1))
    return x * cos + swapped * sin


def _mod_kernel(c_ref, w_ref, b_ref, o_ref):
    o_ref[0] = _mm(_silu(c_ref[...]), w_ref[0]) + b_ref[0]


def _modulation(cc, w_mod, b_mod):
    depth = w_mod.shape[0]
    n = w_mod.shape[2]
    tn = D_MODEL
    return pl.pallas_call(
        _mod_kernel,
        grid=(depth, n // tn),
        in_specs=[pl.BlockSpec((MOD_ROWS, D_MODEL), lambda l, j: (0, 0)),
                  pl.BlockSpec((1, D_MODEL, tn), lambda l, j: (l, 0, j)),
                  pl.BlockSpec((1, 1, tn), lambda l, j: (l, 0, j))],
        out_specs=pl.BlockSpec((1, MOD_ROWS, tn), lambda l, j: (l, 0, j)),
        out_shape=jax.ShapeDtypeStruct((depth, MOD_ROWS, n), F32),
        compiler_params=_cparams(("arbitrary", "arbitrary")),
        name="modulation",
    )(cc, w_mod, b_mod.reshape(depth, 1, n))


def _inproj_kernel(x_ref, mod_ref, nw_ref, w_ref, o_ref, sm_ref, h_ref):
    j = pl.program_id(2)

    @pl.when(j == 0)
    def _():
        m = mod_ref[0]
        h_ref[...] = _norm_mod(x_ref[0], nw_ref[...], m[1:2], m[0:1]).astype(BF16)

    p = jnp.dot(h_ref[...], w_ref[...], preferred_element_type=F32)
    o_ref[0] = p.astype(o_ref.dtype)

    @pl.when(j == pl.num_programs(2) - 1)
    def _():
        off = C_SMALL - (NP_COLS // N_TILES) * (N_TILES - 1)
        sm_ref[0] = p[:, off:off + LANES]


def _in_projection(x, mod_l, mod_row, nw, w_r, layer, tm):
    bsz, seq, d = x.shape
    tn = NP_COLS // N_TILES
    return pl.pallas_call(
        _inproj_kernel,
        grid=(bsz, seq // tm, N_TILES),
        in_specs=[pl.BlockSpec((1, tm, d), lambda b, i, j: (b, i, 0)),
                  pl.BlockSpec((1, 6, d), lambda b, i, j: (mod_row(b), 0, 0)),
                  pl.BlockSpec((1, d), lambda b, i, j: (0, 0)),
                  pl.BlockSpec((None, d, tn), lambda b, i, j: (layer, 0, j))],
        out_specs=[pl.BlockSpec((1, tm, tn), lambda b, i, j: (b, i, j)),
                   pl.BlockSpec((1, tm, LANES), lambda b, i, j: (b, i, 0))],
        out_shape=[jax.ShapeDtypeStruct((bsz, seq, NP_COLS), BF16),
                   jax.ShapeDtypeStruct((bsz, seq, LANES), F32)],
        scratch_shapes=[pltpu.VMEM((tm, d), BF16)],
        compiler_params=_cparams(("arbitrary", "arbitrary", "arbitrary")),
        name="in_projection",
    )(x, mod_l, nw.reshape(1, d), w_r)


def _conv3(x, prev_row, next_row, w):
    ts = x.shape[0]
    rid = lax.broadcasted_iota(jnp.int32, x.shape, 0)
    xm1 = jnp.where(rid == 0, prev_row, pltpu.roll(x, 1, 0))
    xp1 = jnp.where(rid == ts - 1, next_row, pltpu.roll(x, ts - 1, 0))
    return w[0:1] * xm1 + w[1:2] * x + w[2:3] * xp1


def _conv_halo(xp_ref, xn_ref):
    i = pl.program_id(1)
    prev_row = jnp.where(i == 0, 0.0, xp_ref[0, HALO - 1:HALO, :].astype(F32))
    next_row = jnp.where(i == pl.num_programs(1) - 1, 0.0, xn_ref[0, 0:1, :].astype(F32))
    return prev_row, next_row


def _gdn_conv_kernel(x_ref, xp_ref, xn_ref, w_ref, o_ref):
    prev_row, next_row = _conv_halo(xp_ref, xn_ref)
    y = _silu(_conv3(x_ref[0].astype(F32), prev_row, next_row, w_ref[...]))
    j = pl.program_id(2)
    parts = []
    for h in range(GDN_HEADS):
        yh = y[:, h * GDN_DK:(h + 1) * GDN_DK]
        parts.append(yh * lax.rsqrt(jnp.sum(yh * yh, axis=-1, keepdims=True) + EPS))
    yn = jnp.concatenate(parts, axis=-1)
    yn = yn * jnp.where(j == 0, GDN_DK ** -0.5, 1.0)
    o_ref[0] = jnp.where(j < 2, yn, y).astype(o_ref.dtype)


def _ssd_conv_kernel(x_ref, xp_ref, xn_ref, w_ref, b_ref, o_ref):
    prev_row, next_row = _conv_halo(xp_ref, xn_ref)
    o_ref[0] = _silu(_conv3(x_ref[0].astype(F32), prev_row, next_row, w_ref[...]) + b_ref[...]).astype(o_ref.dtype)


def _short_conv(p, col0, width, tc, ts, conv_w, conv_b):
    bsz, seq, _ = p.shape
    cb0 = col0 // tc
    nrb = seq // HALO
    rpb = ts // HALO
    x_specs = [pl.BlockSpec((1, ts, tc), lambda b, i, j: (b, i, cb0 + j)),
               pl.BlockSpec((1, HALO, tc), lambda b, i, j: (b, jnp.maximum(i * rpb - 1, 0), cb0 + j)),
               pl.BlockSpec((1, HALO, tc), lambda b, i, j: (b, jnp.minimum((i + 1) * rpb, nrb - 1), cb0 + j)),
               pl.BlockSpec((3, tc), lambda b, i, j: (0, j))]
    args = [p, p, p, conv_w]
    if conv_b is None:
        body = _gdn_conv_kernel
    else:
        body = _ssd_conv_kernel
        x_specs.append(pl.BlockSpec((1, tc), lambda b, i, j: (0, j)))
        args.append(conv_b.reshape(1, width))
    return pl.pallas_call(
        body,
        grid=(bsz, seq // ts, width // tc),
        in_specs=x_specs,
        out_specs=pl.BlockSpec((1, ts, tc), lambda b, i, j: (b, i, j)),
        out_shape=jax.ShapeDtypeStruct((bsz, seq, width), BF16),
        compiler_params=_cparams(("arbitrary", "arbitrary", "arbitrary")),
        name="short_conv",
    )(*args)


GDN_PACK = 4
GDN_STEP_CHUNKS = 4
GDN_PAIRS = GDN_HEADS // 2


def _bd4(x):
    x = x.astype(BF16)
    n = x.shape[0]
    lo = lax.broadcasted_iota(jnp.int32, (n, LANES), 1) < LANES // 2
    z = jnp.zeros((n, LANES), BF16)
    a, b = x[:, :LANES], x[:, LANES:]
    return jnp.concatenate([jnp.concatenate([jnp.where(lo, a, z), z], 1),
                            jnp.concatenate([jnp.where(lo, z, a), z], 1),
                            jnp.concatenate([z, jnp.where(lo, b, z)], 1),
                            jnp.concatenate([z, jnp.where(lo, z, b)], 1)], 0)


def _bd_blocks(blocks):
    z = jnp.zeros_like(blocks[0])
    n = len(blocks)
    return jnp.concatenate([jnp.concatenate([blk if j == i else z for j in range(n)], 1)
                            for i, blk in enumerate(blocks)], 0)


def _pack_cols(cols):
    n = cols[0].shape[0]
    lo = lax.broadcasted_iota(jnp.int32, (n, LANES), 1) < LANES // 2
    bc = [jnp.broadcast_to(col, (n, LANES)) for col in cols]
    return jnp.concatenate([jnp.where(lo, bc[0], bc[1]), jnp.where(lo, bc[2], bc[3])], 1)


def _chunk_tri(n, c, lower):
    i = lax.broadcasted_iota(jnp.int32, (n, n), 0)
    j = lax.broadcasted_iota(jnp.int32, (n, n), 1)
    same = (i // c) == (j // c)
    return (same & ((i >= j) if lower else (i <= j))).astype(BF16)


def _gdn_gates(s_ref, r_ref, alog_l, dtb_l, alog_s, dtb_s, fwd):
    raw = s_ref[0]
    n = raw.shape[0]
    lbeta = -_softplus(-raw)
    la = -jnp.exp(alog_l) * _softplus(raw + dtb_l)
    rraw = r_ref[0, 0]
    beta_r = _sigmoid(rraw)
    la_r = -jnp.exp(alog_s) * _softplus(rraw + dtb_s)
    low = _chunk_tri(n, GDN_CHUNK, True)
    up = _chunk_tri(n, GDN_CHUNK, False)
    if fwd:
        return lbeta, _mm_sel_l(low, la), beta_r, _mm_sel_r(la_r, up)
    return lbeta, _mm_sel_l(up, la), beta_r, _mm_sel_r(la_r, low)


def _gdn_pre(groups):
    c = GDN_CHUNK
    w = GDN_PACK * c
    ii = lax.broadcasted_iota(jnp.int32, (c, w), 0)
    jj = lax.broadcasted_iota(jnp.int32, (c, w), 1) & (c - 1)
    masks = {True: (ii >= jj, ii > jj), False: (ii <= jj, ii < jj)}
    for g in groups:
        g["gc_p"] = _pack_cols(g["gcs"])
        g["gcb_p"] = _pack_cols([gc + lb for gc, lb in zip(g["gcs"], g["lbs"])])
        g["gr_p"] = jnp.concatenate(g["grs"], 1)
        g["beta_rp"] = jnp.concatenate(g["brs"], 1)
        g["kbd"] = _bd_blocks([g["k4"][:, i * GDN_DK:(i + 1) * GDN_DK] for i in range(GDN_PACK)])
    for g in groups:
        incl, strict = masks[g["fwd"]]
        g["a_p"] = _mm_nt(g["k4"], g["kbd"]) * jnp.exp(jnp.where(strict, g["gcb_p"] - g["gr_p"], -jnp.inf))
        g["qk_p"] = _mm_nt(g["q4"], g["kbd"]) * jnp.exp(jnp.where(incl, g["gc_p"] - g["gr_p"], -jnp.inf))
    for level in range(c.bit_length() - 1):
        bi, bj = ii >> level, jj >> level
        pair = {True: ((bi & 1) == 1) & (bj == bi - 1), False: ((bi & 1) == 0) & (bj == bi + 1)}
        offs = [jnp.where(pair[g["fwd"]], g["a_p"], 0.0) for g in groups]
        if level == 0:
            eye = (ii == jj).astype(F32)
            for g, off in zip(groups, offs):
                g["inv"] = eye - off
        else:
            tmps = [jnp.dot(g["inv"].astype(BF16), _bd4(off), preferred_element_type=F32)
                    for g, off in zip(groups, offs)]
            for g, tmp in zip(groups, tmps):
                g["inv"] = g["inv"] - jnp.dot(tmp.astype(BF16), _bd4(g["inv"]), preferred_element_type=F32)
    for g in groups:
        vbd = _bd_blocks([g["v4"][:, i * GDN_DV:(i + 1) * GDN_DV] for i in range(GDN_PACK)])
        g["u4"] = _mm(g["inv"] * g["beta_rp"], vbd)
        g["w4"] = _mm(g["inv"] * (g["beta_rp"] * jnp.exp(g["gr_p"])), g["kbd"])


def _gdn_pair_steps(items, states):
    c = GDN_CHUNK
    rs = []
    for it, s2 in zip(items, states):
        sbd = _bd_blocks([s2[:, :GDN_DV].astype(BF16), s2[:, GDN_DV:].astype(BF16)])
        egs = [jnp.exp(g) for g in it["gcs"]]
        qd2 = jnp.concatenate([it["q2"][:, i * GDN_DK:(i + 1) * GDN_DK] * egs[i] for i in range(2)], 1)
        rs.append(_mm(jnp.concatenate([it["w2"], qd2], 0), sbd))
    outs, news = [], []
    for it, s2, r in zip(items, states, rs):
        g_lasts = [g[c - 1:c] if it["fwd"] else g[0:1] for g in it["gcs"]]
        vn2 = it["u2"] - r[:c]
        vnbd = _bd_blocks([vn2[:, :GDN_DV].astype(BF16), vn2[:, GDN_DV:].astype(BF16)])
        outs.append(r[c:] + _mm(it["qk2"], vnbd))
        kd = jnp.concatenate([it["k2"][:, i * GDN_DK:(i + 1) * GDN_DK] * jnp.exp(g_lasts[i] - it["gcs"][i])
                              for i in range(2)], 0)
        cd2 = jnp.concatenate([jnp.broadcast_to(jnp.exp(gl), (1, GDN_DV)) for gl in g_lasts], 1)
        news.append(s2 * cd2 + _mm_tn(kd, vnbd))
    return outs, news


def _gdn_kernel(qf_ref, kf_ref, vf_ref, sf_ref, rf_ref, qb_ref, kb_ref, vb_ref, sb_ref, rb_ref,
                alog_l_ref, dtb_l_ref, alog_s_ref, dtb_s_ref, s0f_ref, s0b_ref,
                of_ref, ob_ref, stf_ref, stb_ref):
    @pl.when(pl.program_id(1) == 0)
    def _():
        stf_ref[...] = s0f_ref[...]
        stb_ref[...] = s0b_ref[...]

    c = GDN_CHUNK
    nck = qf_ref.shape[1] // c
    alog_l, dtb_l, alog_s, dtb_s = alog_l_ref[...], dtb_l_ref[...], alog_s_ref[...], dtb_s_ref[...]
    groups = []
    for d, (q_ref, k_ref, v_ref, s_ref, r_ref, o_ref, st_ref) in enumerate(
            ((qf_ref, kf_ref, vf_ref, sf_ref, rf_ref, of_ref, stf_ref),
             (qb_ref, kb_ref, vb_ref, sb_ref, rb_ref, ob_ref, stb_ref))):
        fwd = d == 0
        lbeta_all, gc_all, beta_r_all, gr_all = _gdn_gates(s_ref, r_ref, alog_l, dtb_l, alog_s, dtb_s, fwd)
        for t in range(nck):
            ci = t if fwd else nck - 1 - t
            rows = slice(ci * c, (ci + 1) * c)
            for half in range(GDN_HEADS // GDN_PACK):
                h0 = half * GDN_PACK
                lanes_b = [SM_BETA + d * GDN_HEADS + h0 + i for i in range(GDN_PACK)]
                lanes_a = [SM_A + d * GDN_HEADS + h0 + i for i in range(GDN_PACK)]
                sl4 = slice(h0 * GDN_DK, (h0 + GDN_PACK) * GDN_DK)
                groups.append(dict(
                    fwd=fwd, t=t, rows=rows, h0=h0, o_ref=o_ref, st_ref=st_ref,
                    q4=q_ref[0, rows, sl4], k4=k_ref[0, rows, sl4], v4=v_ref[0, rows, sl4],
                    gcs=[gc_all[rows, la:la + 1] for la in lanes_a],
                    lbs=[lbeta_all[rows, lb:lb + 1] for lb in lanes_b],
                    grs=[gr_all[la:la + 1, rows] for la in lanes_a],
                    brs=[beta_r_all[lb:lb + 1, rows] for lb in lanes_b]))
    _gdn_pre(groups)

    keys = [(d, pair) for d in range(2) for pair in range(GDN_PAIRS)]
    st_refs = (stf_ref, stb_ref)
    states = [st_refs[d][0, pair] for d, pair in keys]
    for t in range(nck):
        items = []
        for d, pair in keys:
            g = next(g for g in groups if g["fwd"] == (d == 0) and g["t"] == t and g["h0"] == (pair // 2) * GDN_PACK)
            pr = pair % 2
            s2 = slice(pr * 2 * GDN_DK, (pr + 1) * 2 * GDN_DK)
            items.append(dict(fwd=g["fwd"], q2=g["q4"][:, s2], k2=g["k4"][:, s2], u2=g["u4"][:, s2], w2=g["w4"][:, s2],
                              qk2=g["qk_p"][:, pr * 2 * c:(pr + 1) * 2 * c], gcs=g["gcs"][2 * pr:2 * pr + 2],
                              o_ref=g["o_ref"], rows=g["rows"], pair=pair))
        outs, states = _gdn_pair_steps(items, states)
        for it, o2 in zip(items, outs):
            it["o_ref"][0, it["rows"], it["pair"] * 2 * GDN_DV:(it["pair"] + 1) * 2 * GDN_DV] = o2.astype(BF16)
    for (d, pair), s2 in zip(keys, states):
        st_refs[d][0, pair] = s2


def _lane_param(vals, offset):
    flat = vals.reshape(-1).astype(F32)
    v = jnp.zeros((LANES,), F32).at[offset:offset + flat.shape[0]].set(flat)
    return v.reshape(1, LANES), v.reshape(LANES, 1)


def _small_rows(small, chunk):
    bsz, seq, _ = small.shape
    return jnp.swapaxes(small.reshape(bsz, seq // chunk, chunk, LANES), 2, 3)


def _gdn_scan(qkv, small, a_log, dt_bias, s0f, s0b):
    bsz, seq, _ = qkv.shape
    blk = min(seq, GDN_STEP_CHUNKS * GDN_CHUNK)
    nb = seq // blk
    rows = _small_rows(small, blk)
    alog_l, alog_s = _lane_param(a_log, SM_A)
    dtb_l, dtb_s = _lane_param(dt_bias, SM_A)
    hw = GDN_QK_DIM

    def block_specs(bmap):
        return [pl.BlockSpec((1, blk, hw), lambda b, i: (b, bmap(i), 0)),
                pl.BlockSpec((1, blk, hw), lambda b, i: (b, bmap(i), 1)),
                pl.BlockSpec((1, blk, hw), lambda b, i: (b, bmap(i), 2)),
                pl.BlockSpec((1, blk, LANES), lambda b, i: (b, bmap(i), 0)),
                pl.BlockSpec((1, 1, LANES, blk), lambda b, i: (b, bmap(i), 0, 0))]

    fw = lambda i: i
    bw = lambda i: nb - 1 - i
    vec_l = pl.BlockSpec((1, LANES), lambda b, i: (0, 0))
    vec_s = pl.BlockSpec((LANES, 1), lambda b, i: (0, 0))
    st_spec = pl.BlockSpec((1, GDN_PAIRS, GDN_DK, 2 * GDN_DV), lambda b, i: (b, 0, 0, 0))
    out_shape = [jax.ShapeDtypeStruct((bsz, seq, GDN_V_DIM), BF16)] * 2 + \
                [jax.ShapeDtypeStruct((bsz, GDN_PAIRS, GDN_DK, 2 * GDN_DV), F32)] * 2
    return pl.pallas_call(
        _gdn_kernel,
        grid=(bsz, nb),
        in_specs=block_specs(fw) + block_specs(bw) + [vec_l, vec_l, vec_s, vec_s, st_spec, st_spec],
        out_specs=[pl.BlockSpec((1, blk, GDN_V_DIM), lambda b, i: (b, i, 0)),
                   pl.BlockSpec((1, blk, GDN_V_DIM), lambda b, i: (b, nb - 1 - i, 0)),
                   st_spec, st_spec],
        out_shape=out_shape,
        compiler_params=_cparams(("arbitrary", "arbitrary")),
        name="gdn_scan",
    )(qkv, qkv, qkv, small, rows, qkv, qkv, qkv, small, rows, alog_l, dtb_l, alog_s, dtb_s, s0f, s0b)


def _ssd_gates(s_ref, r_ref, alog_l, dtb_l, alog_s, dtb_s, fwd):
    c = SSD_CHUNK
    da = _softplus(s_ref[0] + dtb_l) * (-jnp.exp(alog_l))
    dt_r = _softplus(r_ref[0, 0] + dtb_s)
    da_r = dt_r * (-jnp.exp(alog_s))
    low = _tri(c, True).astype(BF16)
    up = _tri(c, False).astype(BF16)
    if fwd:
        return _mm_sel_l(low, da), dt_r, _mm_sel_r(da_r, up)
    return _mm_sel_l(up, da), dt_r, _mm_sel_r(da_r, low)


def _ssd_kernel(xf_ref, bf_ref, cf_ref, sf_ref, rf_ref, xb_ref, bb_ref, cb_ref, sb_ref, rb_ref,
                alog_l_ref, dtb_l_ref, alog_s_ref, dtb_s_ref, s0f_ref, s0b_ref,
                yf_ref, yb_ref, stf_ref, stb_ref):
    @pl.when(pl.program_id(1) == 0)
    def _():
        stf_ref[...] = s0f_ref[...]
        stb_ref[...] = s0b_ref[...]

    c = SSD_CHUNK
    alog_l, dtb_l, alog_s, dtb_s = alog_l_ref[...], dtb_l_ref[...], alog_s_ref[...], dtb_s_ref[...]
    chains = []
    for d, (x_ref, b_ref, c_ref, s_ref, r_ref, y_ref, st_ref) in enumerate(
            ((xf_ref, bf_ref, cf_ref, sf_ref, rf_ref, yf_ref, stf_ref),
             (xb_ref, bb_ref, cb_ref, sb_ref, rb_ref, yb_ref, stb_ref))):
        fwd = d == 0
        incl = _tri(c, fwd)
        ac_all, dtr_all, ar_all = _ssd_gates(s_ref, r_ref, alog_l, dtb_l, alog_s, dtb_s, fwd)
        for g in range(SSD_GROUPS):
            bm = b_ref[0, :, g * SSD_STATE:(g + 1) * SSD_STATE].astype(F32)
            cm = c_ref[0, :, g * SSD_STATE:(g + 1) * SSD_STATE].astype(F32)
            cbm = _mm_nt(cm, bm)
            bm_t = bm.T
            for e in range(SSD_HPG):
                hid = g * SSD_HPG + e
                ln = SM_DT + d * SSD_HEADS + hid
                sl = slice(hid * SSD_HEAD_DIM, (hid + 1) * SSD_HEAD_DIM)
                ar = ar_all[ln:ln + 1, :]
                chains.append(dict(incl=incl, cm=cm, cbm=cbm, bm_t=bm_t, ar=ar, dtr=dtr_all[ln:ln + 1, :],
                                   acol=jnp.broadcast_to(ac_all[:, ln:ln + 1], (c, c)),
                                   a_last=ar[:, c - 1:c] if fwd else ar[:, 0:1],
                                   x=x_ref[0, :, sl], state=st_ref[0, hid],
                                   y_ref=y_ref, st_ref=st_ref, hid=hid, sl=sl))
    lhs = [jnp.concatenate(
        [(ch["cbm"] * jnp.exp(jnp.where(ch["incl"], ch["acol"] - ch["ar"], -jnp.inf)) * ch["dtr"]).astype(BF16),
         (ch["cm"] * jnp.exp(ch["acol"])).astype(BF16)], axis=-1) for ch in chains]
    rhs = [jnp.concatenate([ch["x"].astype(BF16), ch["state"].astype(BF16)], axis=0) for ch in chains]
    ys = [jnp.dot(l, r, preferred_element_type=F32) for l, r in zip(lhs, rhs)]
    sts = [ch["state"] * jnp.exp(ch["a_last"])
           + _mm(ch["bm_t"] * (ch["dtr"] * jnp.exp(ch["a_last"] - ch["ar"])), ch["x"]) for ch in chains]
    for ch, y, st in zip(chains, ys, sts):
        ch["y_ref"][0, :, ch["sl"]] = y.astype(BF16)
        ch["st_ref"][0, ch["hid"]] = st


def _ssd_scan(xbc, small, a_log, dt_bias, s0f, s0b):
    bsz, seq, _ = xbc.shape
    c = SSD_CHUNK
    nc = seq // c
    rows = _small_rows(small, c)
    alog_l, alog_s = _lane_param(a_log, SM_DT)
    dtb_l, dtb_s = _lane_param(dt_bias, SM_DT)
    gn = SSD_GROUPS * SSD_STATE

    def chunk_specs(cmap):
        return [pl.BlockSpec((1, c, SSD_D_INNER), lambda b, i: (b, cmap(i), 0)),
                pl.BlockSpec((1, c, gn), lambda b, i: (b, cmap(i), SSD_D_INNER // gn)),
                pl.BlockSpec((1, c, gn), lambda b, i: (b, cmap(i), SSD_D_INNER // gn + 1)),
                pl.BlockSpec((1, c, LANES), lambda b, i: (b, cmap(i), 0)),
                pl.BlockSpec((1, 1, LANES, c), lambda b, i: (b, cmap(i), 0, 0))]

    fw = lambda i: i
    bw = lambda i: nc - 1 - i
    vec_l = pl.BlockSpec((1, LANES), lambda b, i: (0, 0))
    vec_s = pl.BlockSpec((LANES, 1), lambda b, i: (0, 0))
    st_spec = pl.BlockSpec((1, SSD_HEADS, SSD_STATE, SSD_HEAD_DIM), lambda b, i: (b, 0, 0, 0))
    out_shape = [jax.ShapeDtypeStruct((bsz, seq, SSD_D_INNER), BF16)] * 2 + \
                [jax.ShapeDtypeStruct((bsz, SSD_HEADS, SSD_STATE, SSD_HEAD_DIM), F32)] * 2
    return pl.pallas_call(
        _ssd_kernel,
        grid=(bsz, nc),
        in_specs=chunk_specs(fw) + chunk_specs(bw) + [vec_l, vec_l, vec_s, vec_s, st_spec, st_spec],
        out_specs=[pl.BlockSpec((1, c, SSD_D_INNER), lambda b, i: (b, i, 0)),
                   pl.BlockSpec((1, c, SSD_D_INNER), lambda b, i: (b, nc - 1 - i, 0)),
                   st_spec, st_spec],
        out_shape=out_shape,
        compiler_params=_cparams(("arbitrary", "arbitrary")),
        name="ssd_scan",
    )(xbc, xbc, xbc, small, rows, xbc, xbc, xbc, small, rows, alog_l, dtb_l, alog_s, dtb_s, s0f, s0b)


def _kv_prep_kernel(k_ref, v_ref, nw_ref, cos_ref, sin_ref, ko_ref, vo_ref, *, rope):
    parts = []
    for h in range(ATT_KV_HEADS):
        kh = _rms(k_ref[0, :, h * ATT_HEAD_DIM:(h + 1) * ATT_HEAD_DIM].astype(F32), nw_ref[...])
        if rope:
            kh = _rope(kh, cos_ref[...], sin_ref[...])
        parts.append(kh)
    ko_ref[0] = jnp.concatenate(parts, axis=-1).astype(BF16)
    v = v_ref[0]
    ones_blk = (lax.broadcasted_iota(jnp.int32, (v.shape[0], ATT_HEAD_DIM), 1) == 0).astype(BF16)
    vparts = []
    for h in range(ATT_KV_HEADS):
        vparts += [v[:, h * ATT_HEAD_DIM:(h + 1) * ATT_HEAD_DIM].astype(BF16), ones_blk]
    vo_ref[0] = jnp.concatenate(vparts, axis=-1)


def _kv_prep(p, k_norm, cos, sin, rope, ts):
    bsz, seq, _ = p.shape
    kb = C_AKV // ATT_KV_DIM
    tab = pl.BlockSpec((ts, ATT_HEAD_DIM), lambda b, i: (i if rope else 0, 0))
    return pl.pallas_call(
        functools.partial(_kv_prep_kernel, rope=rope),
        grid=(bsz, seq // ts),
        in_specs=[pl.BlockSpec((1, ts, ATT_KV_DIM), lambda b, i: (b, i, kb)),
                  pl.BlockSpec((1, ts, ATT_KV_DIM), lambda b, i: (b, i, kb + 1)),
                  pl.BlockSpec((1, ATT_HEAD_DIM), lambda b, i: (0, 0)), tab, tab],
        out_specs=[pl.BlockSpec((1, ts, ATT_KV_DIM), lambda b, i: (b, i, 0)),
                   pl.BlockSpec((1, ts, 2 * ATT_KV_DIM), lambda b, i: (b, i, 0))],
        out_shape=[jax.ShapeDtypeStruct((bsz, seq, ATT_KV_DIM), BF16),
                   jax.ShapeDtypeStruct((bsz, seq, 2 * ATT_KV_DIM), BF16)],
        compiler_params=_cparams(("arbitrary", "arbitrary")),
        name="kv_prep",
    )(p, p, k_norm.reshape(1, ATT_HEAD_DIM), cos, sin)


def _attn_kernel(*refs, n_seg, rope):
    q_ref, nw_ref, cos_ref, sin_ref = refs[:4]
    kv_refs = refs[4:4 + 2 * n_seg]
    o_ref = refs[4 + 2 * n_seg]
    qscale = ATT_HEAD_DIM ** -0.5 * LOG2E
    heads = range(ATT_REP)
    qs = []
    for r in heads:
        qh = _rms(q_ref[0, :, r * ATT_HEAD_DIM:(r + 1) * ATT_HEAD_DIM].astype(F32), nw_ref[...])
        if rope:
            qh = _rope(qh, cos_ref[...], sin_ref[...])
        qs.append((qh * qscale).astype(BF16))
    scores = [[_mm_nt(qs[r], kv_refs[2 * s][0]) for s in range(n_seg)] for r in heads]
    maxes = [functools.reduce(jnp.maximum, [jnp.max(sc, axis=-1, keepdims=True) for sc in scores[r]]) for r in heads]
    accs = [functools.reduce(jnp.add, [_mm(jnp.exp2(scores[r][s] - maxes[r]), kv_refs[2 * s + 1][0])
                                        for s in range(n_seg)]) for r in heads]
    for r in heads:
        o_ref[0, :, r * ATT_HEAD_DIM:(r + 1) * ATT_HEAD_DIM] = (
            accs[r][:, :ATT_HEAD_DIM] / accs[r][:, ATT_HEAD_DIM:ATT_HEAD_DIM + 1]).astype(o_ref.dtype)


def _attention(p, q_norm, cos, sin, kv_segs, rope, tq):
    bsz, seq, _ = p.shape
    gw = ATT_REP * ATT_HEAD_DIM
    qb = C_AQ // gw
    n_seg = len(kv_segs)
    tab = pl.BlockSpec((tq, ATT_HEAD_DIM), lambda b, g, i: (i if rope else 0, 0))
    in_specs = [pl.BlockSpec((1, tq, gw), lambda b, g, i: (b, i, qb + g)),
                pl.BlockSpec((1, ATT_HEAD_DIM), lambda b, g, i: (0, 0)), tab, tab]
    args = [p, q_norm.reshape(1, ATT_HEAD_DIM), cos, sin]
    for k_arr, v_arr in kv_segs:
        lk = k_arr.shape[1]
        in_specs += [pl.BlockSpec((1, lk, ATT_HEAD_DIM), lambda b, g, i: (b, 0, g)),
                     pl.BlockSpec((1, lk, 2 * ATT_HEAD_DIM), lambda b, g, i: (b, 0, g))]
        args += [k_arr, v_arr]
    return pl.pallas_call(
        functools.partial(_attn_kernel, n_seg=n_seg, rope=rope),
        grid=(bsz, ATT_KV_HEADS, seq // tq),
        in_specs=in_specs,
        out_specs=pl.BlockSpec((1, tq, gw), lambda b, g, i: (b, i, g)),
        out_shape=jax.ShapeDtypeStruct((bsz, seq, ATT_Q_DIM), BF16),
        compiler_params=_cparams(("arbitrary", "arbitrary", "arbitrary")),
        name="attention",
    )(*args)


def _merge_kernel(x_ref, mod_ref, of_ref, ob_ref, gz_ref, yf_ref, yb_ref, xs_ref, sz_ref, att_ref, gate_ref,
                  gnw_ref, snw_ref, dsk_ref, wg_ref, ws_ref, wa_ref, wo_ref, o_ref):
    o = of_ref[0].astype(F32) + ob_ref[0].astype(F32)
    gz = gz_ref[0].astype(F32)
    parts = []
    for h in range(GDN_HEADS):
        sl = slice(h * GDN_DV, (h + 1) * GDN_DV)
        parts.append(_rms(o[:, sl], gnw_ref[...]) * _silu(gz[:, sl]))
    y_gdn = jnp.concatenate(parts, axis=-1)

    y = yf_ref[0].astype(F32) + yb_ref[0].astype(F32) + dsk_ref[...] * xs_ref[0].astype(F32)
    y = y * _silu(sz_ref[0].astype(F32))
    snw = snw_ref[...]
    gw = SSD_D_INNER // SSD_GROUPS
    y_ssd = jnp.concatenate([_rms(y[:, g * gw:(g + 1) * gw], snw[:, g * gw:(g + 1) * gw])
                             for g in range(SSD_GROUPS)], axis=-1)

    gates = _sigmoid(gate_ref[0].astype(F32))
    d = D_MODEL
    m = (gates[:, :d] * _mm(y_gdn, wg_ref[...]) + gates[:, d:2 * d] * _mm(y_ssd, ws_ref[...])
         + gates[:, 2 * d:] * _mm(att_ref[0], wa_ref[...]))
    g1 = mod_ref[0][2:3]
    o_ref[0] = x_ref[0] + g1 * _mm(m, wo_ref[...])


def _merge(x, mod_l, mod_row, o_f, o_b, y_f, y_b, xbc, att, p, gdn_norm, ssd_norm, ssd_d, wg, ws, wa, wo, layer, tm):
    bsz, seq, d = x.shape
    row = lambda cb: pl.BlockSpec((1, tm, d), lambda b, i: (b, i, cb))
    vec = lambda n: pl.BlockSpec((1, n), lambda b, i: (0, 0))
    wsp = pl.BlockSpec((None, d, d), lambda b, i: (layer, 0, 0), pipeline_mode=pl.Buffered(1))
    return pl.pallas_call(
        _merge_kernel,
        grid=(bsz, seq // tm),
        in_specs=[row(0), pl.BlockSpec((1, 6, d), lambda b, i: (mod_row(b), 0, 0)),
                  row(0), row(0), row(C_GZ // d), row(0), row(0), row(0), row(C_SZ // d), row(0),
                  pl.BlockSpec((1, tm, 3 * d), lambda b, i: (b, i, C_GATE // (3 * d))),
                  vec(GDN_DV), vec(d), vec(d), wsp, wsp, wsp, wsp],
        out_specs=row(0),
        out_shape=jax.ShapeDtypeStruct((bsz, seq, d), F32),
        compiler_params=_cparams(("arbitrary", "arbitrary")),
        name="merge",
    )(x, mod_l, o_f, o_b, p, y_f, y_b, xbc, p, att, p,
      gdn_norm.reshape(1, GDN_DV), ssd_norm.reshape(1, d),
      jnp.repeat(ssd_d, SSD_HEAD_DIM).reshape(1, d), wg, ws, wa, wo)


def _mlp_kernel(x_ref, mod_ref, nw_ref, w1_ref, w2_ref, o_ref, h_ref, acc_ref):
    k = pl.program_id(2)
    m = mod_ref[0]

    @pl.when(k == 0)
    def _():
        h_ref[...] = _norm_mod(x_ref[0], nw_ref[...], m[4:5], m[3:4]).astype(BF16)
        acc_ref[...] = jnp.zeros_like(acc_ref)

    a = jnp.maximum(jnp.dot(h_ref[...], w1_ref[...], preferred_element_type=F32), 0.0)
    acc_ref[...] += _mm(a * a, w2_ref[...])

    @pl.when(k == pl.num_programs(2) - 1)
    def _():
        o_ref[0] = x_ref[0] + m[5:6] * acc_ref[...]


def _mlp(x, mod_l, mod_row, nw, w1, w2, layer, tm, tf):
    bsz, seq, d = x.shape
    return pl.pallas_call(
        _mlp_kernel,
        grid=(bsz, seq // tm, D_FF // tf),
        in_specs=[pl.BlockSpec((1, tm, d), lambda b, i, k: (b, i, 0)),
                  pl.BlockSpec((1, 6, d), lambda b, i, k: (mod_row(b), 0, 0)),
                  pl.BlockSpec((1, d), lambda b, i, k: (0, 0)),
                  pl.BlockSpec((None, d, tf), lambda b, i, k: (layer, 0, k)),
                  pl.BlockSpec((None, tf, d), lambda b, i, k: (layer, k, 0))],
        out_specs=pl.BlockSpec((1, tm, d), lambda b, i, k: (b, i, 0)),
        out_shape=jax.ShapeDtypeStruct((bsz, seq, d), F32),
        scratch_shapes=[pltpu.VMEM((tm, d), BF16), pltpu.VMEM((tm, d), F32)],
        compiler_params=_cparams(("arbitrary", "arbitrary", "arbitrary")),
        name="mlp",
    )(x, mod_l, nw.reshape(1, d), w1, w2)


def _reorder_w_in(w_in):
    depth, d, _ = w_in.shape
    o = 0
    seg = {}
    for name, size in (("qkv", GDN_QKV), ("gz", GDN_V_DIM), ("beta", 2 * GDN_HEADS), ("a", 2 * GDN_HEADS),
                       ("sz", SSD_D_INNER), ("xbc", SSD_XBC), ("dt", 2 * SSD_HEADS),
                       ("aq", ATT_Q_DIM), ("akv", 2 * ATT_KV_DIM), ("gate", 3 * D_MODEL)):
        seg[name] = w_in[:, :, o:o + size]
        o += size
    pad = jnp.zeros((depth, d, NP_COLS - (C_SMALL + SM_DT + 2 * SSD_HEADS)), w_in.dtype)
    out = jnp.concatenate([seg["qkv"], seg["gz"], seg["sz"], seg["aq"], seg["gate"], seg["xbc"], seg["akv"],
                           seg["beta"], seg["a"], seg["dt"], pad], axis=-1)
    return out.astype(BF16)


def _rope_tables(seq):
    t = jnp.arange(seq, dtype=jnp.int32)
    q = ATT_HEAD_DIM // 4
    freqs = ROPE_THETA ** (-jnp.arange(q, dtype=F32) / q)
    ang_r = (t // GRID_W).astype(F32)[:, None] * freqs[None, :]
    ang_c = (t % GRID_W).astype(F32)[:, None] * freqs[None, :]
    cos = jnp.concatenate([jnp.cos(ang_r)] * 2 + [jnp.cos(ang_c)] * 2, axis=-1)
    sin = jnp.concatenate([-jnp.sin(ang_r), jnp.sin(ang_r), -jnp.sin(ang_c), jnp.sin(ang_c)], axis=-1)
    return cos, sin


def _tile(seq, want):
    return min(seq, want)


def kernel(x, c, ctx, c_ctx, w_mod, b_mod, norm_mix, norm_mlp, w_in, gdn_conv, gdn_a_log, gdn_dt_bias, gdn_norm,
           ssd_conv_w, ssd_conv_b, ssd_a_log, ssd_dt_bias, ssd_d, ssd_norm, att_q_norm, att_k_norm,
           w_br_gdn, w_br_ssd, w_br_att, w_out, w_ff1, w_ff2):
    bsz, seq, d = x.shape
    ctx_len = ctx.shape[1]
    depth = w_in.shape[0]
    assert bsz < MOD_ROWS and d == D_MODEL

    cc = jnp.zeros((MOD_ROWS, d), F32).at[:bsz].set(c).at[bsz].set(c_ctx)
    mod = _modulation(cc, w_mod, b_mod).reshape(depth, MOD_ROWS, 6, d)
    lat_row = lambda b: b
    ctx_row = lambda b: bsz

    w_in_r = _reorder_w_in(w_in)
    wg, ws, wa, wo = (w.astype(BF16) for w in (w_br_gdn, w_br_ssd, w_br_att, w_out))
    w1, w2 = w_ff1.astype(BF16), w_ff2.astype(BF16)
    cos, sin = _rope_tables(seq)

    zg = jnp.zeros((bsz, GDN_PAIRS, GDN_DK, 2 * GDN_DV), F32)
    zs = jnp.zeros((bsz, SSD_HEADS, SSD_STATE, SSD_HEAD_DIM), F32)
    xc = ctx
    for l in range(depth):
        last = l == depth - 1
        p_lat, sm_lat = _in_projection(x, mod[l], lat_row, norm_mix[l], w_in_r, l, _tile(seq, 1024))
        p_ctx, sm_ctx = _in_projection(xc, mod[l], ctx_row, norm_mix[l], w_in_r, l, _tile(ctx_len, 1024))

        qkv_c = _short_conv(p_ctx, C_QKV, GDN_QKV, GDN_QK_DIM, _tile(ctx_len, 256), gdn_conv[l], None)
        qkv_l = _short_conv(p_lat, C_QKV, GDN_QKV, GDN_QK_DIM, _tile(seq, 256), gdn_conv[l], None)
        xbc_c = _short_conv(p_ctx, C_XBC, SSD_XBC, 512, _tile(ctx_len, 512), ssd_conv_w[l], ssd_conv_b[l])
        xbc_l = _short_conv(p_lat, C_XBC, SSD_XBC, 512, _tile(seq, 512), ssd_conv_w[l], ssd_conv_b[l])

        ogf_c, ogb_c, sgf, sgb = _gdn_scan(qkv_c, sm_ctx, gdn_a_log[l], gdn_dt_bias[l], zg, zg)
        ogf_l, ogb_l, _, _ = _gdn_scan(qkv_l, sm_lat, gdn_a_log[l], gdn_dt_bias[l], sgf, sgb)
        ysf_c, ysb_c, ssf, ssb = _ssd_scan(xbc_c, sm_ctx, ssd_a_log[l], ssd_dt_bias[l], zs, zs)
        ysf_l, ysb_l, _, _ = _ssd_scan(xbc_l, sm_lat, ssd_a_log[l], ssd_dt_bias[l], ssf, ssb)

        k_c, v_c = _kv_prep(p_ctx, att_k_norm[l], cos, sin, False, _tile(ctx_len, 256))
        k_l, v_l = _kv_prep(p_lat, att_k_norm[l], cos, sin, True, _tile(seq, 512))
        att_l = _attention(p_lat, att_q_norm[l], cos, sin, [(k_c, v_c), (k_l, v_l)], True, _tile(seq, 256))

        x = _merge(x, mod[l], lat_row, ogf_l, ogb_l, ysf_l, ysb_l, xbc_l, att_l, p_lat,
                   gdn_norm[l], ssd_norm[l], ssd_d[l], wg, ws, wa, wo, l, _tile(seq, 512))
        x = _mlp(x, mod[l], lat_row, norm_mlp[l], w1, w2, l, _tile(seq, 1024), 1024)

        if not last:
            att_c = _attention(p_ctx, att_q_norm[l], cos, sin, [(k_c, v_c)], False, _tile(ctx_len, 256))
            xc = _merge(xc, mod[l], ctx_row, ogf_c, ogb_c, ysf_c, ysb_c, xbc_c, att_c, p_ctx,
                        gdn_norm[l], ssd_norm[l], ssd_d[l], wg, ws, wa, wo, l, _tile(ctx_len, 256))
            xc = _mlp(xc, mod[l], ctx_row, norm_mlp[l], w1, w2, l, _tile(ctx_len, 512), 1024)
    return x
```

```python
import functools

import jax
import jax.numpy as jnp
from jax import lax
from jax.experimental import pallas as pl
from jax.experimental.pallas import tpu as pltpu

F32 = jnp.float32
BF16 = jnp.bfloat16

D_MODEL = 1024
GRID_W = 64
EPS = 1e-6

GDN_HEADS = 8
GDN_DK = 128
GDN_DV = 128
GDN_CHUNK = 64
GDN_QK_DIM = GDN_HEADS * GDN_DK
GDN_V_DIM = GDN_HEADS * GDN_DV
GDN_QKV = 2 * GDN_QK_DIM + GDN_V_DIM

SSD_D_INNER = D_MODEL
SSD_HEAD_DIM = 64
SSD_HEADS = SSD_D_INNER // SSD_HEAD_DIM
SSD_GROUPS = 2
SSD_HPG = SSD_HEADS // SSD_GROUPS
SSD_STATE = 128
SSD_CHUNK = 128
SSD_XBC = SSD_D_INNER + 2 * SSD_GROUPS * SSD_STATE

ATT_HEADS = 8
ATT_KV_HEADS = 2
ATT_REP = ATT_HEADS // ATT_KV_HEADS
ATT_HEAD_DIM = 128
ATT_Q_DIM = ATT_HEADS * ATT_HEAD_DIM
ATT_KV_DIM = ATT_KV_HEADS * ATT_HEAD_DIM
ROPE_THETA = 10000.0
LOG2E = 1.4426950408889634
D_FF = 4 * D_MODEL

LANES = 128
HALO = 16
MOD_ROWS = 16

C_QKV = 0
C_XBC = C_QKV + GDN_QKV
C_AKV = C_XBC + SSD_XBC
C_GZ = C_AKV + 2 * ATT_KV_DIM
C_SZ = C_GZ + GDN_V_DIM
C_AQ = C_SZ + SSD_D_INNER
C_GATE = C_AQ + ATT_Q_DIM
C_SMALL = C_GATE + 3 * D_MODEL
NP_COLS = 11520
N_TILES = 5
CONV_TILES = 2
CONV_COLS = GDN_QKV + SSD_XBC
CONV_SUB = 256
SM_BETA = 0
SM_A = 2 * GDN_HEADS
SM_DT = 4 * GDN_HEADS

VMEM_LIMIT = 56 * 1024 * 1024


def _cparams(sem):
    return pltpu.CompilerParams(dimension_semantics=sem, vmem_limit_bytes=VMEM_LIMIT)


def _mm(a, b):
    return jnp.dot(a.astype(BF16), b.astype(BF16), preferred_element_type=F32)


def _mm_nt(a, b):
    return lax.dot_general(a.astype(BF16), b.astype(BF16), (((1,), (1,)), ((), ())),
                           preferred_element_type=F32)


def _mm_tn(a, b):
    return lax.dot_general(a.astype(BF16), b.astype(BF16), (((0,), (0,)), ((), ())),
                           preferred_element_type=F32)


def _split3(x):
    hi = x.astype(BF16)
    r = x - hi.astype(F32)
    mid = r.astype(BF16)
    lo = (r - mid.astype(F32)).astype(BF16)
    return hi, mid, lo


def _mm_sel_l(sel, x):
    hi, mid, lo = _split3(x)
    d = lambda p: jnp.dot(sel, p, preferred_element_type=F32)
    return (d(hi) + d(mid)) + d(lo)


def _mm_sel_r(x, sel):
    hi, mid, lo = _split3(x)
    d = lambda p: jnp.dot(p, sel, preferred_element_type=F32)
    return (d(hi) + d(mid)) + d(lo)


def _sigmoid(x):
    return 1.0 / (1.0 + jnp.exp(-x))


def _silu(x):
    return x * _sigmoid(x)


def _softplus(x):
    return jnp.maximum(x, 0.0) + jnp.log(1.0 + jnp.exp(-jnp.abs(x)))


def _rms(x, w):
    return x * lax.rsqrt(jnp.mean(x * x, axis=-1, keepdims=True) + EPS) * w


def _norm_mod(x, nw, scale, shift):
    return _rms(x, nw) * (1.0 + scale) + shift


def _tri(n, lower):
    i = lax.broadcasted_iota(jnp.int32, (n, n), 0)
    j = lax.broadcasted_iota(jnp.int32, (n, n), 1)
    return (i >= j) if lower else (i <= j)


def _rope(x, cos, sin):
    lane = lax.broadcasted_iota(jnp.int32, x.shape, 1)
    q = LANES // 4
    swapped = jnp.where((lane & q) == 0, pltpu.roll(x, LANES - q, 1), pltpu.roll(x, q, 1))
    return x * cos + swapped * sin


def _mod_kernel(c_ref, w_ref, b_ref, o_ref):
    o_ref[0] = _mm(_silu(c_ref[...]), w_ref[0]) + b_ref[0]


def _modulation(cc, w_mod, b_mod):
    depth = w_mod.shape[0]
    n = w_mod.shape[2]
    tn = D_MODEL
    return pl.pallas_call(
        _mod_kernel,
        grid=(depth, n // tn),
        in_specs=[pl.BlockSpec((MOD_ROWS, D_MODEL), lambda l, j: (0, 0)),
                  pl.BlockSpec((1, D_MODEL, tn), lambda l, j: (l, 0, j)),
                  pl.BlockSpec((1, 1, tn), lambda l, j: (l, 0, j))],
        out_specs=pl.BlockSpec((1, MOD_ROWS, tn), lambda l, j: (l, 0, j)),
        out_shape=jax.ShapeDtypeStruct((depth, MOD_ROWS, n), F32),
        compiler_params=_cparams(("arbitrary", "arbitrary")),
        name="modulation",
    )(cc, w_mod, b_mod.reshape(depth, 1, n))


def _conv_tile(h_ref, w_ref, cw_ref, cb_ref, o_ref, tm, first_tile, last_tile, l2_blocks):
    sub = min(tm, CONV_SUB)
    nsub = tm // sub
    tn = w_ref.shape[1]
    cw = cw_ref[...]
    rid = lax.broadcasted_iota(jnp.int32, (sub, tn), 0)
    for r in range(nsub):
        p = jnp.dot(h_ref[r * sub:r * sub + sub + 2 * HALO], w_ref[...], preferred_element_type=F32)
        n = sub + 2 * HALO
        xm1 = pltpu.roll(p, 1, 0)[HALO:HALO + sub]
        xp1 = pltpu.roll(p, n - 1, 0)[HALO:HALO + sub]
        if r == 0:
            xm1 = jnp.where((rid == 0) & first_tile, 0.0, xm1)
        if r == nsub - 1:
            xp1 = jnp.where((rid == sub - 1) & last_tile, 0.0, xp1)
        y = _silu(cw[0:1] * xm1 + cw[1:2] * p[HALO:HALO + sub] + cw[2:3] * xp1 + cb_ref[...])
        if l2_blocks:
            parts = []
            for blk in range(tn // LANES):
                yb = y[:, blk * LANES:(blk + 1) * LANES]
                if blk < l2_blocks:
                    inv = lax.rsqrt(jnp.sum(yb * yb, axis=-1, keepdims=True) + EPS)
                    yb = yb * (inv * GDN_DK ** -0.5 if blk < GDN_HEADS else inv)
                parts.append(yb)
            y = jnp.concatenate(parts, axis=-1)
        o_ref[0, r * sub:(r + 1) * sub, :] = y.astype(o_ref.dtype)


def _inproj_kernel(x_ref, xp_ref, xn_ref, mod_ref, nw_ref, w_ref, cw_ref, cb_ref, o_ref, sm_ref, h_ref):
    i = pl.program_id(1)
    j = pl.program_id(2)
    tm = x_ref.shape[1]

    @pl.when(j == 0)
    def _():
        m = mod_ref[0]
        norm = lambda x: _norm_mod(x, nw_ref[...], m[1:2], m[0:1]).astype(BF16)
        h_ref[0:HALO] = norm(xp_ref[0])
        h_ref[HALO:HALO + tm] = norm(x_ref[0])
        h_ref[HALO + tm:2 * HALO + tm] = norm(xn_ref[0])

    first_tile = i == 0
    last_tile = i == pl.num_programs(1) - 1

    @pl.when(j == 0)
    def _():
        _conv_tile(h_ref, w_ref, cw_ref, cb_ref, o_ref, tm, first_tile, last_tile, 2 * GDN_HEADS)

    @pl.when((j > 0) & (j < CONV_TILES))
    def _():
        _conv_tile(h_ref, w_ref, cw_ref, cb_ref, o_ref, tm, first_tile, last_tile, 0)

    @pl.when(j >= CONV_TILES)
    def _():
        p = jnp.dot(h_ref[HALO:HALO + tm], w_ref[...], preferred_element_type=F32)
        o_ref[0] = p.astype(o_ref.dtype)

        @pl.when(j == pl.num_programs(2) - 1)
        def _():
            off = C_SMALL - (NP_COLS // N_TILES) * (N_TILES - 1)
            sm_ref[0] = p[:, off:off + LANES]


def _in_projection(x, mod_l, mod_row, nw, w_r, conv_w, conv_b, layer, tm):
    bsz, seq, d = x.shape
    tn = NP_COLS // N_TILES
    assert CONV_COLS == CONV_TILES * tn and 2 * GDN_QK_DIM <= tn
    nrb = seq // HALO
    rpb = tm // HALO
    ct = lambda j: jnp.minimum(j, CONV_TILES - 1)
    return pl.pallas_call(
        _inproj_kernel,
        grid=(bsz, seq // tm, N_TILES),
        in_specs=[pl.BlockSpec((1, tm, d), lambda b, i, j: (b, i, 0)),
                  pl.BlockSpec((1, HALO, d), lambda b, i, j: (b, jnp.maximum(i * rpb - 1, 0), 0)),
                  pl.BlockSpec((1, HALO, d), lambda b, i, j: (b, jnp.minimum((i + 1) * rpb, nrb - 1), 0)),
                  pl.BlockSpec((1, 6, d), lambda b, i, j: (mod_row(b), 0, 0)),
                  pl.BlockSpec((1, d), lambda b, i, j: (0, 0)),
                  pl.BlockSpec((None, d, tn), lambda b, i, j: (layer, 0, j)),
                  pl.BlockSpec((3, tn), lambda b, i, j: (0, ct(j))),
                  pl.BlockSpec((1, tn), lambda b, i, j: (0, ct(j)))],
        out_specs=[pl.BlockSpec((1, tm, tn), lambda b, i, j: (b, i, j)),
                   pl.BlockSpec((1, tm, LANES), lambda b, i, j: (b, i, 0))],
        out_shape=[jax.ShapeDtypeStruct((bsz, seq, NP_COLS), BF16),
                   jax.ShapeDtypeStruct((bsz, seq, LANES), F32)],
        scratch_shapes=[pltpu.VMEM((tm + 2 * HALO, d), BF16)],
        compiler_params=_cparams(("arbitrary", "arbitrary", "arbitrary")),
        name="in_projection",
    )(x, x, x, mod_l, nw.reshape(1, d), w_r, conv_w, conv_b)


GDN_PACK = 4
GDN_STEP_CHUNKS = 4
GDN_PAIRS = GDN_HEADS // 2


def _bd4(x):
    x = x.astype(BF16)
    n = x.shape[0]
    lo = lax.broadcasted_iota(jnp.int32, (n, LANES), 1) < LANES // 2
    z = jnp.zeros((n, LANES), BF16)
    a, b = x[:, :LANES], x[:, LANES:]
    return jnp.concatenate([jnp.concatenate([jnp.where(lo, a, z), z], 1),
                            jnp.concatenate([jnp.where(lo, z, a), z], 1),
                            jnp.concatenate([z, jnp.where(lo, b, z)], 1),
                            jnp.concatenate([z, jnp.where(lo, z, b)], 1)], 0)


def _bd_blocks(blocks):
    z = jnp.zeros_like(blocks[0])
    n = len(blocks)
    return jnp.concatenate([jnp.concatenate([blk if j == i else z for j in range(n)], 1)
                            for i, blk in enumerate(blocks)], 0)


def _pack_cols(cols):
    n = cols[0].shape[0]
    lo = lax.broadcasted_iota(jnp.int32, (n, LANES), 1) < LANES // 2
    bc = [jnp.broadcast_to(col, (n, LANES)) for col in cols]
    return jnp.concatenate([jnp.where(lo, bc[0], bc[1]), jnp.where(lo, bc[2], bc[3])], 1)


def _chunk_tri(n, c, lower):
    i = lax.broadcasted_iota(jnp.int32, (n, n), 0)
    j = lax.broadcasted_iota(jnp.int32, (n, n), 1)
    same = (i // c) == (j // c)
    return (same & ((i >= j) if lower else (i <= j))).astype(BF16)


def _gdn_gates(s_ref, r_ref, alog_l, dtb_l, alog_s, dtb_s, fwd):
    raw = s_ref[0]
    n = raw.shape[0]
    lbeta = -_softplus(-raw)
    la = -jnp.exp(alog_l) * _softplus(raw + dtb_l)
    rraw = r_ref[0, 0]
    beta_r = _sigmoid(rraw)
    la_r = -jnp.exp(alog_s) * _softplus(rraw + dtb_s)
    low = _chunk_tri(n, GDN_CHUNK, True)
    up = _chunk_tri(n, GDN_CHUNK, False)
    if fwd:
        return lbeta, _mm_sel_l(low, la), beta_r, _mm_sel_r(la_r, up)
    return lbeta, _mm_sel_l(up, la), beta_r, _mm_sel_r(la_r, low)


def _gdn_pre(groups):
    c = GDN_CHUNK
    w = GDN_PACK * c
    ii = lax.broadcasted_iota(jnp.int32, (c, w), 0)
    jj = lax.broadcasted_iota(jnp.int32, (c, w), 1) & (c - 1)
    masks = {True: (ii >= jj, ii > jj), False: (ii <= jj, ii < jj)}
    for g in groups:
        g["gc_p"] = _pack_cols(g["gcs"])
        g["gcb_p"] = _pack_cols([gc + lb for gc, lb in zip(g["gcs"], g["lbs"])])
        g["gr_p"] = jnp.concatenate(g["grs"], 1)
        g["beta_rp"] = jnp.concatenate(g["brs"], 1)
        g["kbd"] = _bd_blocks([g["k4"][:, i * GDN_DK:(i + 1) * GDN_DK] for i in range(GDN_PACK)])
    for g in groups:
        incl, strict = masks[g["fwd"]]
        g["a_p"] = _mm_nt(g["k4"], g["kbd"]) * jnp.exp(jnp.where(strict, g["gcb_p"] - g["gr_p"], -jnp.inf))
        g["qk_p"] = _mm_nt(g["q4"], g["kbd"]) * jnp.exp(jnp.where(incl, g["gc_p"] - g["gr_p"], -jnp.inf))
    for level in range(c.bit_length() - 1):
        bi, bj = ii >> level, jj >> level
        pair = {True: ((bi & 1) == 1) & (bj == bi - 1), False: ((bi & 1) == 0) & (bj == bi + 1)}
        offs = [jnp.where(pair[g["fwd"]], g["a_p"], 0.0) for g in groups]
        if level == 0:
            eye = (ii == jj).astype(F32)
            for g, off in zip(groups, offs):
                g["inv"] = eye - off
        else:
            tmps = [jnp.dot(g["inv"].astype(BF16), _bd4(off), preferred_element_type=F32)
                    for g, off in zip(groups, offs)]
            for g, tmp in zip(groups, tmps):
                g["inv"] = g["inv"] - jnp.dot(tmp.astype(BF16), _bd4(g["inv"]), preferred_element_type=F32)
    for g in groups:
        vbd = _bd_blocks([g["v4"][:, i * GDN_DV:(i + 1) * GDN_DV] for i in range(GDN_PACK)])
        g["u4"] = _mm(g["inv"] * g["beta_rp"], vbd)
        g["w4"] = _mm(g["inv"] * (g["beta_rp"] * jnp.exp(g["gr_p"])), g["kbd"])


def _gdn_pair_steps(items, states):
    c = GDN_CHUNK
    rs = []
    for it, s2 in zip(items, states):
        sbd = _bd_blocks([s2[:, :GDN_DV].astype(BF16), s2[:, GDN_DV:].astype(BF16)])
        egs = [jnp.exp(g) for g in it["gcs"]]
        qd2 = jnp.concatenate([it["q2"][:, i * GDN_DK:(i + 1) * GDN_DK] * egs[i] for i in range(2)], 1)
        rs.append(_mm(jnp.concatenate([it["w2"], qd2], 0), sbd))
    outs, news = [], []
    for it, s2, r in zip(items, states, rs):
        g_lasts = [g[c - 1:c] if it["fwd"] else g[0:1] for g in it["gcs"]]
        vn2 = it["u2"] - r[:c]
        vnbd = _bd_blocks([vn2[:, :GDN_DV].astype(BF16), vn2[:, GDN_DV:].astype(BF16)])
        outs.append(r[c:] + _mm(it["qk2"], vnbd))
        kd = jnp.concatenate([it["k2"][:, i * GDN_DK:(i + 1) * GDN_DK] * jnp.exp(g_lasts[i] - it["gcs"][i])
                              for i in range(2)], 0)
        cd2 = jnp.concatenate([jnp.broadcast_to(jnp.exp(gl), (1, GDN_DV)) for gl in g_lasts], 1)
        news.append(s2 * cd2 + _mm_tn(kd, vnbd))
    return outs, news


def _gdn_kernel(qf_ref, kf_ref, vf_ref, sf_ref, rf_ref, qb_ref, kb_ref, vb_ref, sb_ref, rb_ref,
                alog_l_ref, dtb_l_ref, alog_s_ref, dtb_s_ref, s0f_ref, s0b_ref,
                of_ref, ob_ref, stf_ref, stb_ref):
    @pl.when(pl.program_id(1) == 0)
    def _():
        stf_ref[...] = s0f_ref[...]
        stb_ref[...] = s0b_ref[...]

    c = GDN_CHUNK
    nck = qf_ref.shape[1] // c
    alog_l, dtb_l, alog_s, dtb_s = alog_l_ref[...], dtb_l_ref[...], alog_s_ref[...], dtb_s_ref[...]
    groups = []
    for d, (q_ref, k_ref, v_ref, s_ref, r_ref, o_ref, st_ref) in enumerate(
            ((qf_ref, kf_ref, vf_ref, sf_ref, rf_ref, of_ref, stf_ref),
             (qb_ref, kb_ref, vb_ref, sb_ref, rb_ref, ob_ref, stb_ref))):
        fwd = d == 0
        lbeta_all, gc_all, beta_r_all, gr_all = _gdn_gates(s_ref, r_ref, alog_l, dtb_l, alog_s, dtb_s, fwd)
        for t in range(nck):
            ci = t if fwd else nck - 1 - t
            rows = slice(ci * c, (ci + 1) * c)
            for half in range(GDN_HEADS // GDN_PACK):
                h0 = half * GDN_PACK
                lanes_b = [SM_BETA + d * GDN_HEADS + h0 + i for i in range(GDN_PACK)]
                lanes_a = [SM_A + d * GDN_HEADS + h0 + i for i in range(GDN_PACK)]
                sl4 = slice(h0 * GDN_DK, (h0 + GDN_PACK) * GDN_DK)
                groups.append(dict(
                    fwd=fwd, t=t, rows=rows, h0=h0, o_ref=o_ref, st_ref=st_ref,
                    q4=q_ref[0, rows, sl4], k4=k_ref[0, rows, sl4], v4=v_ref[0, rows, sl4],
                    gcs=[gc_all[rows, la:la + 1] for la in lanes_a],
                    lbs=[lbeta_all[rows, lb:lb + 1] for lb in lanes_b],
                    grs=[gr_all[la:la + 1, rows] for la in lanes_a],
                    brs=[beta_r_all[lb:lb + 1, rows] for lb in lanes_b]))
    _gdn_pre(groups)

    keys = [(d, pair) for d in range(2) for pair in range(GDN_PAIRS)]
    st_refs = (stf_ref, stb_ref)
    states = [st_refs[d][0, pair] for d, pair in keys]
    for t in range(nck):
        items = []
        for d, pair in keys:
            g = next(g for g in groups if g["fwd"] == (d == 0) and g["t"] == t and g["h0"] == (pair // 2) * GDN_PACK)
            pr = pair % 2
            s2 = slice(pr * 2 * GDN_DK, (pr + 1) * 2 * GDN_DK)
            items.append(dict(fwd=g["fwd"], q2=g["q4"][:, s2], k2=g["k4"][:, s2], u2=g["u4"][:, s2], w2=g["w4"][:, s2],
                              qk2=g["qk_p"][:, pr * 2 * c:(pr + 1) * 2 * c], gcs=g["gcs"][2 * pr:2 * pr + 2],
                              o_ref=g["o_ref"], rows=g["rows"], pair=pair))
        outs, states = _gdn_pair_steps(items, states)
        for it, o2 in zip(items, outs):
            it["o_ref"][0, it["rows"], it["pair"] * 2 * GDN_DV:(it["pair"] + 1) * 2 * GDN_DV] = o2.astype(BF16)
    for (d, pair), s2 in zip(keys, states):
        st_refs[d][0, pair] = s2


def _lane_param(vals, offset):
    flat = vals.reshape(-1).astype(F32)
    v = jnp.zeros((LANES,), F32).at[offset:offset + flat.shape[0]].set(flat)
    return v.reshape(1, LANES), v.reshape(LANES, 1)


def _small_rows(small, chunk):
    bsz, seq, _ = small.shape
    return jnp.swapaxes(small.reshape(bsz, seq // chunk, chunk, LANES), 2, 3)


def _gdn_scan(qkv, small, a_log, dt_bias, s0f, s0b):
    bsz, seq, _ = qkv.shape
    blk = min(seq, GDN_STEP_CHUNKS * GDN_CHUNK)
    nb = seq // blk
    rows = _small_rows(small, blk)
    alog_l, alog_s = _lane_param(a_log, SM_A)
    dtb_l, dtb_s = _lane_param(dt_bias, SM_A)
    hw = GDN_QK_DIM

    def block_specs(bmap):
        return [pl.BlockSpec((1, blk, hw), lambda b, i: (b, bmap(i), 0)),
                pl.BlockSpec((1, blk, hw), lambda b, i: (b, bmap(i), 1)),
                pl.BlockSpec((1, blk, hw), lambda b, i: (b, bmap(i), 2)),
                pl.BlockSpec((1, blk, LANES), lambda b, i: (b, bmap(i), 0)),
                pl.BlockSpec((1, 1, LANES, blk), lambda b, i: (b, bmap(i), 0, 0))]

    fw = lambda i: i
    bw = lambda i: nb - 1 - i
    vec_l = pl.BlockSpec((1, LANES), lambda b, i: (0, 0))
    vec_s = pl.BlockSpec((LANES, 1), lambda b, i: (0, 0))
    st_spec = pl.BlockSpec((1, GDN_PAIRS, GDN_DK, 2 * GDN_DV), lambda b, i: (b, 0, 0, 0))
    out_shape = [jax.ShapeDtypeStruct((bsz, seq, GDN_V_DIM), BF16)] * 2 + \
                [jax.ShapeDtypeStruct((bsz, GDN_PAIRS, GDN_DK, 2 * GDN_DV), F32)] * 2
    return pl.pallas_call(
        _gdn_kernel,
        grid=(bsz, nb),
        in_specs=block_specs(fw) + block_specs(bw) + [vec_l, vec_l, vec_s, vec_s, st_spec, st_spec],
        out_specs=[pl.BlockSpec((1, blk, GDN_V_DIM), lambda b, i: (b, i, 0)),
                   pl.BlockSpec((1, blk, GDN_V_DIM), lambda b, i: (b, nb - 1 - i, 0)),
                   st_spec, st_spec],
        out_shape=out_shape,
        compiler_params=_cparams(("arbitrary", "arbitrary")),
        name="gdn_scan",
    )(qkv, qkv, qkv, small, rows, qkv, qkv, qkv, small, rows, alog_l, dtb_l, alog_s, dtb_s, s0f, s0b)


def _ssd_gates(s_ref, r_ref, alog_l, dtb_l, alog_s, dtb_s, fwd):
    c = SSD_CHUNK
    da = _softplus(s_ref[0] + dtb_l) * (-jnp.exp(alog_l))
    dt_r = _softplus(r_ref[0, 0] + dtb_s)
    da_r = dt_r * (-jnp.exp(alog_s))
    low = _tri(c, True).astype(BF16)
    up = _tri(c, False).astype(BF16)
    if fwd:
        return _mm_sel_l(low, da), dt_r, _mm_sel_r(da_r, up)
    return _mm_sel_l(up, da), dt_r, _mm_sel_r(da_r, low)


def _ssd_kernel(xf_ref, bf_ref, cf_ref, sf_ref, rf_ref, xb_ref, bb_ref, cb_ref, sb_ref, rb_ref,
                alog_l_ref, dtb_l_ref, alog_s_ref, dtb_s_ref, s0f_ref, s0b_ref,
                yf_ref, yb_ref, stf_ref, stb_ref):
    @pl.when(pl.program_id(1) == 0)
    def _():
        stf_ref[...] = s0f_ref[...]
        stb_ref[...] = s0b_ref[...]

    c = SSD_CHUNK
    alog_l, dtb_l, alog_s, dtb_s = alog_l_ref[...], dtb_l_ref[...], alog_s_ref[...], dtb_s_ref[...]
    chains = []
    for d, (x_ref, b_ref, c_ref, s_ref, r_ref, y_ref, st_ref) in enumerate(
            ((xf_ref, bf_ref, cf_ref, sf_ref, rf_ref, yf_ref, stf_ref),
             (xb_ref, bb_ref, cb_ref, sb_ref, rb_ref, yb_ref, stb_ref))):
        fwd = d == 0
        incl = _tri(c, fwd)
        ac_all, dtr_all, ar_all = _ssd_gates(s_ref, r_ref, alog_l, dtb_l, alog_s, dtb_s, fwd)
        for g in range(SSD_GROUPS):
            bm = b_ref[0, :, g * SSD_STATE:(g + 1) * SSD_STATE].astype(F32)
            cm = c_ref[0, :, g * SSD_STATE:(g + 1) * SSD_STATE].astype(F32)
            cbm = _mm_nt(cm, bm)
            bm_t = bm.T
            for e in range(SSD_HPG):
                hid = g * SSD_HPG + e
                ln = SM_DT + d * SSD_HEADS + hid
                sl = slice(hid * SSD_HEAD_DIM, (hid + 1) * SSD_HEAD_DIM)
                ar = ar_all[ln:ln + 1, :]
                chains.append(dict(incl=incl, cm=cm, cbm=cbm, bm_t=bm_t, ar=ar, dtr=dtr_all[ln:ln + 1, :],
                                   acol=jnp.broadcast_to(ac_all[:, ln:ln + 1], (c, c)),
                                   a_last=ar[:, c - 1:c] if fwd else ar[:, 0:1],
                                   x=x_ref[0, :, sl], state=st_ref[0, hid],
                                   y_ref=y_ref, st_ref=st_ref, hid=hid, sl=sl))
    lhs = [jnp.concatenate(
        [(ch["cbm"] * jnp.exp(jnp.where(ch["incl"], ch["acol"] - ch["ar"], -jnp.inf)) * ch["dtr"]).astype(BF16),
         (ch["cm"] * jnp.exp(ch["acol"])).astype(BF16)], axis=-1) for ch in chains]
    rhs = [jnp.concatenate([ch["x"].astype(BF16), ch["state"].astype(BF16)], axis=0) for ch in chains]
    ys = [jnp.dot(l, r, preferred_element_type=F32) for l, r in zip(lhs, rhs)]
    sts = [ch["state"] * jnp.exp(ch["a_last"])
           + _mm(ch["bm_t"] * (ch["dtr"] * jnp.exp(ch["a_last"] - ch["ar"])), ch["x"]) for ch in chains]
    for ch, y, st in zip(chains, ys, sts):
        ch["y_ref"][0, :, ch["sl"]] = y.astype(BF16)
        ch["st_ref"][0, ch["hid"]] = st


def _ssd_scan(xbc, small, a_log, dt_bias, s0f, s0b):
    bsz, seq, _ = xbc.shape
    c = SSD_CHUNK
    nc = seq // c
    rows = _small_rows(small, c)
    alog_l, alog_s = _lane_param(a_log, SM_DT)
    dtb_l, dtb_s = _lane_param(dt_bias, SM_DT)
    gn = SSD_GROUPS * SSD_STATE

    def chunk_specs(cmap):
        return [pl.BlockSpec((1, c, SSD_D_INNER), lambda b, i: (b, cmap(i), C_XBC // SSD_D_INNER)),
                pl.BlockSpec((1, c, gn), lambda b, i: (b, cmap(i), (C_XBC + SSD_D_INNER) // gn)),
                pl.BlockSpec((1, c, gn), lambda b, i: (b, cmap(i), (C_XBC + SSD_D_INNER) // gn + 1)),
                pl.BlockSpec((1, c, LANES), lambda b, i: (b, cmap(i), 0)),
                pl.BlockSpec((1, 1, LANES, c), lambda b, i: (b, cmap(i), 0, 0))]

    fw = lambda i: i
    bw = lambda i: nc - 1 - i
    vec_l = pl.BlockSpec((1, LANES), lambda b, i: (0, 0))
    vec_s = pl.BlockSpec((LANES, 1), lambda b, i: (0, 0))
    st_spec = pl.BlockSpec((1, SSD_HEADS, SSD_STATE, SSD_HEAD_DIM), lambda b, i: (b, 0, 0, 0))
    out_shape = [jax.ShapeDtypeStruct((bsz, seq, SSD_D_INNER), BF16)] * 2 + \
                [jax.ShapeDtypeStruct((bsz, SSD_HEADS, SSD_STATE, SSD_HEAD_DIM), F32)] * 2
    return pl.pallas_call(
        _ssd_kernel,
        grid=(bsz, nc),
        in_specs=chunk_specs(fw) + chunk_specs(bw) + [vec_l, vec_l, vec_s, vec_s, st_spec, st_spec],
        out_specs=[pl.BlockSpec((1, c, SSD_D_INNER), lambda b, i: (b, i, 0)),
                   pl.BlockSpec((1, c, SSD_D_INNER), lambda b, i: (b, nc - 1 - i, 0)),
                   st_spec, st_spec],
        out_shape=out_shape,
        compiler_params=_cparams(("arbitrary", "arbitrary")),
        name="ssd_scan",
    )(xbc, xbc, xbc, small, rows, xbc, xbc, xbc, small, rows, alog_l, dtb_l, alog_s, dtb_s, s0f, s0b)


def _kv_prep_kernel(k_ref, v_ref, nw_ref, cos_ref, sin_ref, ko_ref, vo_ref, *, rope):
    parts = []
    for h in range(ATT_KV_HEADS):
        kh = _rms(k_ref[0, :, h * ATT_HEAD_DIM:(h + 1) * ATT_HEAD_DIM].astype(F32), nw_ref[...])
        if rope:
            kh = _rope(kh, cos_ref[...], sin_ref[...])
        parts.append(kh)
    ko_ref[0] = jnp.concatenate(parts, axis=-1).astype(BF16)
    v = v_ref[0]
    ones_blk = (lax.broadcasted_iota(jnp.int32, (v.shape[0], ATT_HEAD_DIM), 1) == 0).astype(BF16)
    vparts = []
    for h in range(ATT_KV_HEADS):
        vparts += [v[:, h * ATT_HEAD_DIM:(h + 1) * ATT_HEAD_DIM].astype(BF16), ones_blk]
    vo_ref[0] = jnp.concatenate(vparts, axis=-1)


def _kv_prep(p, k_norm, cos, sin, rope, ts):
    bsz, seq, _ = p.shape
    kb = C_AKV // ATT_KV_DIM
    tab = pl.BlockSpec((ts, ATT_HEAD_DIM), lambda b, i: (i if rope else 0, 0))
    return pl.pallas_call(
        functools.partial(_kv_prep_kernel, rope=rope),
        grid=(bsz, seq // ts),
        in_specs=[pl.BlockSpec((1, ts, ATT_KV_DIM), lambda b, i: (b, i, kb)),
                  pl.BlockSpec((1, ts, ATT_KV_DIM), lambda b, i: (b, i, kb + 1)),
                  pl.BlockSpec((1, ATT_HEAD_DIM), lambda b, i: (0, 0)), tab, tab],
        out_specs=[pl.BlockSpec((1, ts, ATT_KV_DIM), lambda b, i: (b, i, 0)),
                   pl.BlockSpec((1, ts, 2 * ATT_KV_DIM), lambda b, i: (b, i, 0))],
        out_shape=[jax.ShapeDtypeStruct((bsz, seq, ATT_KV_DIM), BF16),
                   jax.ShapeDtypeStruct((bsz, seq, 2 * ATT_KV_DIM), BF16)],
        compiler_params=_cparams(("arbitrary", "arbitrary")),
        name="kv_prep",
    )(p, p, k_norm.reshape(1, ATT_HEAD_DIM), cos, sin)


def _attn_kernel(*refs, n_seg, rope):
    q_ref, nw_ref, cos_ref, sin_ref = refs[:4]
    kv_refs = refs[4:4 + 2 * n_seg]
    o_ref = refs[4 + 2 * n_seg]
    qscale = ATT_HEAD_DIM ** -0.5 * LOG2E
    heads = range(ATT_REP)
    qs = []
    for r in heads:
        qh = _rms(q_ref[0, :, r * ATT_HEAD_DIM:(r + 1) * ATT_HEAD_DIM].astype(F32), nw_ref[...])
        if rope:
            qh = _rope(qh, cos_ref[...], sin_ref[...])
        qs.append((qh * qscale).astype(BF16))
    scores = [[_mm_nt(qs[r], kv_refs[2 * s][0]) for s in range(n_seg)] for r in heads]
    maxes = [functools.reduce(jnp.maximum, [jnp.max(sc, axis=-1, keepdims=True) for sc in scores[r]]) for r in heads]
    accs = [functools.reduce(jnp.add, [_mm(jnp.exp2(scores[r][s] - maxes[r]), kv_refs[2 * s + 1][0])
                                        for s in range(n_seg)]) for r in heads]
    for r in heads:
        o_ref[0, :, r * ATT_HEAD_DIM:(r + 1) * ATT_HEAD_DIM] = (
            accs[r][:, :ATT_HEAD_DIM] / accs[r][:, ATT_HEAD_DIM:ATT_HEAD_DIM + 1]).astype(o_ref.dtype)


def _attention(p, q_norm, cos, sin, kv_segs, rope, tq):
    bsz, seq, _ = p.shape
    gw = ATT_REP * ATT_HEAD_DIM
    qb = C_AQ // gw
    n_seg = len(kv_segs)
    tab = pl.BlockSpec((tq, ATT_HEAD_DIM), lambda b, g, i: (i if rope else 0, 0))
    in_specs = [pl.BlockSpec((1, tq, gw), lambda b, g, i: (b, i, qb + g)),
                pl.BlockSpec((1, ATT_HEAD_DIM), lambda b, g, i: (0, 0)), tab, tab]
    args = [p, q_norm.reshape(1, ATT_HEAD_DIM), cos, sin]
    for k_arr, v_arr in kv_segs:
        lk = k_arr.shape[1]
        in_specs += [pl.BlockSpec((1, lk, ATT_HEAD_DIM), lambda b, g, i: (b, 0, g)),
                     pl.BlockSpec((1, lk, 2 * ATT_HEAD_DIM), lambda b, g, i: (b, 0, g))]
        args += [k_arr, v_arr]
    return pl.pallas_call(
        functools.partial(_attn_kernel, n_seg=n_seg, rope=rope),
        grid=(bsz, ATT_KV_HEADS, seq // tq),
        in_specs=in_specs,
        out_specs=pl.BlockSpec((1, tq, gw), lambda b, g, i: (b, i, g)),
        out_shape=jax.ShapeDtypeStruct((bsz, seq, ATT_Q_DIM), BF16),
        compiler_params=_cparams(("arbitrary", "arbitrary", "arbitrary")),
        name="attention",
    )(*args)


def _merge_kernel(x_ref, mod_ref, of_ref, ob_ref, gz_ref, yf_ref, yb_ref, xs_ref, sz_ref, att_ref,
                  gg_ref, gs_ref, ga_ref, gnw_ref, snw_ref, dsk_ref, wg_ref, ws_ref, wa_ref, wo_ref, o_ref):
    o = of_ref[0].astype(F32) + ob_ref[0].astype(F32)
    gz = gz_ref[0].astype(F32)
    parts = []
    for h in range(GDN_HEADS):
        sl = slice(h * GDN_DV, (h + 1) * GDN_DV)
        parts.append(_rms(o[:, sl], gnw_ref[...]) * _silu(gz[:, sl]))
    y_gdn = jnp.concatenate(parts, axis=-1)

    y = yf_ref[0].astype(F32) + yb_ref[0].astype(F32) + dsk_ref[...] * xs_ref[0].astype(F32)
    y = y * _silu(sz_ref[0].astype(F32))
    snw = snw_ref[...]
    gw = SSD_D_INNER // SSD_GROUPS
    y_ssd = jnp.concatenate([_rms(y[:, g * gw:(g + 1) * gw], snw[:, g * gw:(g + 1) * gw])
                             for g in range(SSD_GROUPS)], axis=-1)

    gate = lambda ref: _sigmoid(ref[0].astype(F32))
    m = (gate(gg_ref) * _mm(y_gdn, wg_ref[...]) + gate(gs_ref) * _mm(y_ssd, ws_ref[...])
         + gate(ga_ref) * _mm(att_ref[0], wa_ref[...]))
    g1 = mod_ref[0][2:3]
    o_ref[0] = x_ref[0] + g1 * _mm(m, wo_ref[...])


def _merge(x, mod_l, mod_row, o_f, o_b, y_f, y_b, att, p, gdn_norm, ssd_norm, ssd_d, wg, ws, wa, wo, layer, tm):
    bsz, seq, d = x.shape
    row = lambda cb: pl.BlockSpec((1, tm, d), lambda b, i: (b, i, cb))
    vec = lambda n: pl.BlockSpec((1, n), lambda b, i: (0, 0))
    wsp = pl.BlockSpec((None, d, d), lambda b, i: (layer, 0, 0), pipeline_mode=pl.Buffered(1))
    return pl.pallas_call(
        _merge_kernel,
        grid=(bsz, seq // tm),
        in_specs=[row(0), pl.BlockSpec((1, 6, d), lambda b, i: (mod_row(b), 0, 0)),
                  row(0), row(0), row(C_GZ // d), row(0), row(0), row(C_XBC // d), row(C_SZ // d), row(0),
                  row(C_GATE // d), row(C_GATE // d + 1), row(C_GATE // d + 2),
                  vec(GDN_DV), vec(d), vec(d), wsp, wsp, wsp, wsp],
        out_specs=row(0),
        out_shape=jax.ShapeDtypeStruct((bsz, seq, d), F32),
        compiler_params=_cparams(("arbitrary", "arbitrary")),
        name="merge",
    )(x, mod_l, o_f, o_b, p, y_f, y_b, p, p, att, p, p, p,
      gdn_norm.reshape(1, GDN_DV), ssd_norm.reshape(1, d),
      jnp.repeat(ssd_d, SSD_HEAD_DIM).reshape(1, d), wg, ws, wa, wo)


def _mlp_kernel(x_ref, mod_ref, nw_ref, w1_ref, w2_ref, o_ref, h_ref, acc_ref):
    k = pl.program_id(2)
    m = mod_ref[0]

    @pl.when(k == 0)
    def _():
        h_ref[...] = _norm_mod(x_ref[0], nw_ref[...], m[4:5], m[3:4]).astype(BF16)
        acc_ref[...] = jnp.zeros_like(acc_ref)

    a = jnp.maximum(jnp.dot(h_ref[...], w1_ref[...], preferred_element_type=F32), 0.0)
    acc_ref[...] += _mm(a * a, w2_ref[...])

    @pl.when(k == pl.num_programs(2) - 1)
    def _():
        o_ref[0] = x_ref[0] + m[5:6] * acc_ref[...]


def _mlp(x, mod_l, mod_row, nw, w1, w2, layer, tm, tf):
    bsz, seq, d = x.shape
    return pl.pallas_call(
        _mlp_kernel,
        grid=(bsz, seq // tm, D_FF // tf),
        in_specs=[pl.BlockSpec((1, tm, d), lambda b, i, k: (b, i, 0)),
                  pl.BlockSpec((1, 6, d), lambda b, i, k: (mod_row(b), 0, 0)),
                  pl.BlockSpec((1, d), lambda b, i, k: (0, 0)),
                  pl.BlockSpec((None, d, tf), lambda b, i, k: (layer, 0, k)),
                  pl.BlockSpec((None, tf, d), lambda b, i, k: (layer, k, 0))],
        out_specs=pl.BlockSpec((1, tm, d), lambda b, i, k: (b, i, 0)),
        out_shape=jax.ShapeDtypeStruct((bsz, seq, d), F32),
        scratch_shapes=[pltpu.VMEM((tm, d), BF16), pltpu.VMEM((tm, d), F32)],
        compiler_params=_cparams(("arbitrary", "arbitrary", "arbitrary")),
        name="mlp",
    )(x, mod_l, nw.reshape(1, d), w1, w2)


def _reorder_w_in(w_in):
    depth, d, _ = w_in.shape
    o = 0
    seg = {}
    for name, size in (("qkv", GDN_QKV), ("gz", GDN_V_DIM), ("beta", 2 * GDN_HEADS), ("a", 2 * GDN_HEADS),
                       ("sz", SSD_D_INNER), ("xbc", SSD_XBC), ("dt", 2 * SSD_HEADS),
                       ("aq", ATT_Q_DIM), ("akv", 2 * ATT_KV_DIM), ("gate", 3 * D_MODEL)):
        seg[name] = w_in[:, :, o:o + size]
        o += size
    pad = jnp.zeros((depth, d, NP_COLS - (C_SMALL + SM_DT + 2 * SSD_HEADS)), w_in.dtype)
    out = jnp.concatenate([seg["qkv"], seg["xbc"], seg["akv"], seg["gz"], seg["sz"], seg["aq"], seg["gate"],
                           seg["beta"], seg["a"], seg["dt"], pad], axis=-1)
    return out.astype(BF16)


def _rope_tables(seq):
    t = jnp.arange(seq, dtype=jnp.int32)
    q = ATT_HEAD_DIM // 4
    freqs = ROPE_THETA ** (-jnp.arange(q, dtype=F32) / q)
    ang_r = (t // GRID_W).astype(F32)[:, None] * freqs[None, :]
    ang_c = (t % GRID_W).astype(F32)[:, None] * freqs[None, :]
    cos = jnp.concatenate([jnp.cos(ang_r)] * 2 + [jnp.cos(ang_c)] * 2, axis=-1)
    sin = jnp.concatenate([-jnp.sin(ang_r), jnp.sin(ang_r), -jnp.sin(ang_c), jnp.sin(ang_c)], axis=-1)
    return cos, sin


def _tile(seq, want):
    return min(seq, want)


def kernel(x, c, ctx, c_ctx, w_mod, b_mod, norm_mix, norm_mlp, w_in, gdn_conv, gdn_a_log, gdn_dt_bias, gdn_norm,
           ssd_conv_w, ssd_conv_b, ssd_a_log, ssd_dt_bias, ssd_d, ssd_norm, att_q_norm, att_k_norm,
           w_br_gdn, w_br_ssd, w_br_att, w_out, w_ff1, w_ff2):
    bsz, seq, d = x.shape
    ctx_len = ctx.shape[1]
    depth = w_in.shape[0]
    assert bsz < MOD_ROWS and d == D_MODEL

    cc = jnp.zeros((MOD_ROWS, d), F32).at[:bsz].set(c).at[bsz].set(c_ctx)
    mod = _modulation(cc, w_mod, b_mod).reshape(depth, MOD_ROWS, 6, d)
    lat_row = lambda b: b
    ctx_row = lambda b: bsz

    w_in_r = _reorder_w_in(w_in)
    wg, ws, wa, wo = (w.astype(BF16) for w in (w_br_gdn, w_br_ssd, w_br_att, w_out))
    w1, w2 = w_ff1.astype(BF16), w_ff2.astype(BF16)
    cos, sin = _rope_tables(seq)

    zg = jnp.zeros((bsz, GDN_PAIRS, GDN_DK, 2 * GDN_DV), F32)
    zs = jnp.zeros((bsz, SSD_HEADS, SSD_STATE, SSD_HEAD_DIM), F32)
    conv_w = jnp.concatenate([gdn_conv, ssd_conv_w], axis=-1)
    conv_b = jnp.concatenate([jnp.zeros((depth, 1, GDN_QKV), F32), ssd_conv_b[:, None, :]], axis=-1)
    xc = ctx
    for l in range(depth):
        last = l == depth - 1
        p_lat, sm_lat = _in_projection(x, mod[l], lat_row, norm_mix[l], w_in_r, conv_w[l], conv_b[l], l,
                                       _tile(seq, 1024))
        p_ctx, sm_ctx = _in_projection(xc, mod[l], ctx_row, norm_mix[l], w_in_r, conv_w[l], conv_b[l], l,
                                       _tile(ctx_len, 1024))

        ogf_c, ogb_c, sgf, sgb = _gdn_scan(p_ctx, sm_ctx, gdn_a_log[l], gdn_dt_bias[l], zg, zg)
        ogf_l, ogb_l, _, _ = _gdn_scan(p_lat, sm_lat, gdn_a_log[l], gdn_dt_bias[l], sgf, sgb)
        ysf_c, ysb_c, ssf, ssb = _ssd_scan(p_ctx, sm_ctx, ssd_a_log[l], ssd_dt_bias[l], zs, zs)
        ysf_l, ysb_l, _, _ = _ssd_scan(p_lat, sm_lat, ssd_a_log[l], ssd_dt_bias[l], ssf, ssb)

        k_c, v_c = _kv_prep(p_ctx, att_k_norm[l], cos, sin, False, _tile(ctx_len, 256))
        k_l, v_l = _kv_prep(p_lat, att_k_norm[l], cos, sin, True, _tile(seq, 512))
        att_l = _attention(p_lat, att_q_norm[l], cos, sin, [(k_c, v_c), (k_l, v_l)], True, _tile(seq, 256))

        x = _merge(x, mod[l], lat_row, ogf_l, ogb_l, ysf_l, ysb_l, att_l, p_lat,
                   gdn_norm[l], ssd_norm[l], ssd_d[l], wg, ws, wa, wo, l, _tile(seq, 512))
        x = _mlp(x, mod[l], lat_row, norm_mlp[l], w1, w2, l, _tile(seq, 1024), 1024)

        if not last:
            att_c = _attention(p_ctx, att_q_norm[l], cos, sin, [(k_c, v_c)], False, _tile(ctx_len, 256))
            xc = _merge(xc, mod[l], ctx_row, ogf_c, ogb_c, ysf_c, ysb_c, att_c, p_ctx,
                        gdn_norm[l], ssd_norm[l], ssd_d[l], wg, ws, wa, wo, l, _tile(ctx_len, 256))
            xc = _mlp(xc, mod[l], ctx_row, norm_mlp[l], w1, w2, l, _tile(ctx_len, 512), 1024)
    return x
```

```python
import functools

import jax
import jax.numpy as jnp
from jax import lax
from jax.experimental import pallas as pl
from jax.experimental.pallas import tpu as pltpu

F32 = jnp.float32
BF16 = jnp.bfloat16

D_MODEL = 1024
GRID_W = 64
EPS = 1e-6

GDN_HEADS = 8
GDN_DK = 128
GDN_DV = 128
GDN_CHUNK = 64
GDN_QK_DIM = GDN_HEADS * GDN_DK
GDN_V_DIM = GDN_HEADS * GDN_DV
GDN_QKV = 2 * GDN_QK_DIM + GDN_V_DIM

SSD_D_INNER = D_MODEL
SSD_HEAD_DIM = 64
SSD_HEADS = SSD_D_INNER // SSD_HEAD_DIM
SSD_GROUPS = 2
SSD_HPG = SSD_HEADS // SSD_GROUPS
SSD_STATE = 128
SSD_CHUNK = 128
SSD_XBC = SSD_D_INNER + 2 * SSD_GROUPS * SSD_STATE

ATT_HEADS = 8
ATT_KV_HEADS = 2
ATT_REP = ATT_HEADS // ATT_KV_HEADS
ATT_HEAD_DIM = 128
ATT_Q_DIM = ATT_HEADS * ATT_HEAD_DIM
ATT_KV_DIM = ATT_KV_HEADS * ATT_HEAD_DIM
ROPE_THETA = 10000.0
LOG2E = 1.4426950408889634
D_FF = 4 * D_MODEL

LANES = 128
HALO = 16
MOD_ROWS = 16

C_QKV = 0
C_XBC = C_QKV + GDN_QKV
C_AKV = C_XBC + SSD_XBC
C_GZ = C_AKV + 2 * ATT_KV_DIM
C_SZ = C_GZ + GDN_V_DIM
C_AQ = C_SZ + SSD_D_INNER
C_GATE = C_AQ + ATT_Q_DIM
C_SMALL = C_GATE + 3 * D_MODEL
NP_COLS = 11520
N_TILES = 5
CONV_TILES = 2
CONV_COLS = GDN_QKV + SSD_XBC
CONV_SUB = 256
SM_BETA = 0
SM_A = 2 * GDN_HEADS
SM_DT = 4 * GDN_HEADS

VMEM_LIMIT = 56 * 1024 * 1024


def _cparams(sem):
    return pltpu.CompilerParams(dimension_semantics=sem, vmem_limit_bytes=VMEM_LIMIT)


def _mm(a, b):
    return jnp.dot(a.astype(BF16), b.astype(BF16), preferred_element_type=F32)


def _mm_nt(a, b):
    return lax.dot_general(a.astype(BF16), b.astype(BF16), (((1,), (1,)), ((), ())),
                           preferred_element_type=F32)


def _mm_tn(a, b):
    return lax.dot_general(a.astype(BF16), b.astype(BF16), (((0,), (0,)), ((), ())),
                           preferred_element_type=F32)


def _split3(x):
    hi = x.astype(BF16)
    r = x - hi.astype(F32)
    mid = r.astype(BF16)
    lo = (r - mid.astype(F32)).astype(BF16)
    return hi, mid, lo


def _mm_sel_l(sel, x):
    hi, mid, lo = _split3(x)
    d = lambda p: jnp.dot(sel, p, preferred_element_type=F32)
    return (d(hi) + d(mid)) + d(lo)


def _mm_sel_r(x, sel):
    hi, mid, lo = _split3(x)
    d = lambda p: jnp.dot(p, sel, preferred_element_type=F32)
    return (d(hi) + d(mid)) + d(lo)


def _sigmoid(x):
    return 1.0 / (1.0 + jnp.exp(-x))


def _silu(x):
    return x * _sigmoid(x)


def _softplus(x):
    return jnp.maximum(x, 0.0) + jnp.log(1.0 + jnp.exp(-jnp.abs(x)))


def _rms(x, w):
    return x * lax.rsqrt(jnp.mean(x * x, axis=-1, keepdims=True) + EPS) * w


def _norm_mod(x, nw, scale, shift):
    return _rms(x, nw) * (1.0 + scale) + shift


def _tri(n, lower):
    i = lax.broadcasted_iota(jnp.int32, (n, n), 0)
    j = lax.broadcasted_iota(jnp.int32, (n, n), 1)
    return (i >= j) if lower else (i <= j)


def _rope(x, cos, sin):
    lane = lax.broadcasted_iota(jnp.int32, x.shape, 1)
    q = LANES // 4
    swapped = jnp.where((lane & q) == 0, pltpu.roll(x, LANES - q, 1), pltpu.roll(x, q, 1))
    return x * cos + swapped * sin


def _mod_kernel(c_ref, w_ref, b_ref, o_ref):
    o_ref[0] = _mm(_silu(c_ref[...]), w_ref[0]) + b_ref[0]


def _modulation(cc, w_mod, b_mod):
    depth = w_mod.shape[0]
    n = w_mod.shape[2]
    tn = D_MODEL
    return pl.pallas_call(
        _mod_kernel,
        grid=(depth, n // tn),
        in_specs=[pl.BlockSpec((MOD_ROWS, D_MODEL), lambda l, j: (0, 0)),
                  pl.BlockSpec((1, D_MODEL, tn), lambda l, j: (l, 0, j)),
                  pl.BlockSpec((1, 1, tn), lambda l, j: (l, 0, j))],
        out_specs=pl.BlockSpec((1, MOD_ROWS, tn), lambda l, j: (l, 0, j)),
        out_shape=jax.ShapeDtypeStruct((depth, MOD_ROWS, n), F32),
        compiler_params=_cparams(("arbitrary", "arbitrary")),
        name="modulation",
    )(cc, w_mod, b_mod.reshape(depth, 1, n))


def _conv_tile(h_ref, w_ref, cw_ref, cb_ref, o_ref, tm, first_tile, last_tile, l2_blocks):
    sub = min(tm, CONV_SUB)
    nsub = tm // sub
    tn = w_ref.shape[1]
    cw = cw_ref[...]
    rid = lax.broadcasted_iota(jnp.int32, (sub, tn), 0)
    for r in range(nsub):
        p = jnp.dot(h_ref[r * sub:r * sub + sub + 2 * HALO], w_ref[...], preferred_element_type=F32)
        n = sub + 2 * HALO
        xm1 = pltpu.roll(p, 1, 0)[HALO:HALO + sub]
        xp1 = pltpu.roll(p, n - 1, 0)[HALO:HALO + sub]
        if r == 0:
            xm1 = jnp.where((rid == 0) & first_tile, 0.0, xm1)
        if r == nsub - 1:
            xp1 = jnp.where((rid == sub - 1) & last_tile, 0.0, xp1)
        y = _silu(cw[0:1] * xm1 + cw[1:2] * p[HALO:HALO + sub] + cw[2:3] * xp1 + cb_ref[...])
        if l2_blocks:
            parts = []
            for blk in range(tn // LANES):
                yb = y[:, blk * LANES:(blk + 1) * LANES]
                if blk < l2_blocks:
                    inv = lax.rsqrt(jnp.sum(yb * yb, axis=-1, keepdims=True) + EPS)
                    yb = yb * (inv * GDN_DK ** -0.5 if blk < GDN_HEADS else inv)
                parts.append(yb)
            y = jnp.concatenate(parts, axis=-1)
        o_ref[0, r * sub:(r + 1) * sub, :] = y.astype(o_ref.dtype)


def _inproj_kernel(x_ref, xp_ref, xn_ref, mod_ref, nw_ref, w_ref, cw_ref, cb_ref, o_ref, sm_ref, h_ref):
    i = pl.program_id(1)
    j = pl.program_id(2)
    tm = x_ref.shape[1]

    @pl.when(j == 0)
    def _():
        m = mod_ref[0]
        norm = lambda x: _norm_mod(x, nw_ref[...], m[1:2], m[0:1]).astype(BF16)
        h_ref[0:HALO] = norm(xp_ref[0])
        h_ref[HALO:HALO + tm] = norm(x_ref[0])
        h_ref[HALO + tm:2 * HALO + tm] = norm(xn_ref[0])

    first_tile = i == 0
    last_tile = i == pl.num_programs(1) - 1

    @pl.when(j == 0)
    def _():
        _conv_tile(h_ref, w_ref, cw_ref, cb_ref, o_ref, tm, first_tile, last_tile, 2 * GDN_HEADS)

    @pl.when((j > 0) & (j < CONV_TILES))
    def _():
        _conv_tile(h_ref, w_ref, cw_ref, cb_ref, o_ref, tm, first_tile, last_tile, 0)

    @pl.when(j >= CONV_TILES)
    def _():
        p = jnp.dot(h_ref[HALO:HALO + tm], w_ref[...], preferred_element_type=F32)
        o_ref[0] = p.astype(o_ref.dtype)

        @pl.when(j == pl.num_programs(2) - 1)
        def _():
            off = C_SMALL - (NP_COLS // N_TILES) * (N_TILES - 1)
            sm_ref[0] = p[:, off:off + LANES]


def _in_projection(x, mod_l, mod_row, nw, w_r, conv_w, conv_b, layer, tm):
    bsz, seq, d = x.shape
    tn = NP_COLS // N_TILES
    assert CONV_COLS == CONV_TILES * tn and 2 * GDN_QK_DIM <= tn
    nrb = seq // HALO
    rpb = tm // HALO
    ct = lambda j: jnp.minimum(j, CONV_TILES - 1)
    return pl.pallas_call(
        _inproj_kernel,
        grid=(bsz, seq // tm, N_TILES),
        in_specs=[pl.BlockSpec((1, tm, d), lambda b, i, j: (b, i, 0)),
                  pl.BlockSpec((1, HALO, d), lambda b, i, j: (b, jnp.maximum(i * rpb - 1, 0), 0)),
                  pl.BlockSpec((1, HALO, d), lambda b, i, j: (b, jnp.minimum((i + 1) * rpb, nrb - 1), 0)),
                  pl.BlockSpec((1, 6, d), lambda b, i, j: (mod_row(b), 0, 0)),
                  pl.BlockSpec((1, d), lambda b, i, j: (0, 0)),
                  pl.BlockSpec((None, d, tn), lambda b, i, j: (layer, 0, j)),
                  pl.BlockSpec((3, tn), lambda b, i, j: (0, ct(j))),
                  pl.BlockSpec((1, tn), lambda b, i, j: (0, ct(j)))],
        out_specs=[pl.BlockSpec((1, tm, tn), lambda b, i, j: (b, i, j)),
                   pl.BlockSpec((1, tm, LANES), lambda b, i, j: (b, i, 0))],
        out_shape=[jax.ShapeDtypeStruct((bsz, seq, NP_COLS), BF16),
                   jax.ShapeDtypeStruct((bsz, seq, LANES), F32)],
        scratch_shapes=[pltpu.VMEM((tm + 2 * HALO, d), BF16)],
        compiler_params=_cparams(("arbitrary", "arbitrary", "arbitrary")),
        name="in_projection",
    )(x, x, x, mod_l, nw.reshape(1, d), w_r, conv_w, conv_b)


GDN_PACK = 2
GDN_STEP_CHUNKS = 4
GDN_PAIRS = GDN_HEADS // 2


def _bd4(x):
    x = x.astype(BF16)
    n = x.shape[0]
    tiles = x.shape[1] // LANES
    lo = lax.broadcasted_iota(jnp.int32, (n, LANES), 1) < LANES // 2
    z = jnp.zeros((n, LANES), BF16)
    rows = []
    for t in range(tiles):
        a = x[:, t * LANES:(t + 1) * LANES]
        for half in (jnp.where(lo, a, z), jnp.where(lo, z, a)):
            rows.append(jnp.concatenate([half if tt == t else z for tt in range(tiles)], 1) if tiles > 1 else half)
    return jnp.concatenate(rows, 0)


def _bd_blocks(blocks):
    z = jnp.zeros_like(blocks[0])
    n = len(blocks)
    return jnp.concatenate([jnp.concatenate([blk if j == i else z for j in range(n)], 1)
                            for i, blk in enumerate(blocks)], 0)


def _pack_cols(cols):
    n = cols[0].shape[0]
    lo = lax.broadcasted_iota(jnp.int32, (n, LANES), 1) < LANES // 2
    bc = [jnp.broadcast_to(col, (n, LANES)) for col in cols]
    tiles = [jnp.where(lo, bc[2 * t], bc[2 * t + 1]) for t in range(len(cols) // 2)]
    return jnp.concatenate(tiles, 1) if len(tiles) > 1 else tiles[0]


def _chunk_tri(n, c, lower):
    i = lax.broadcasted_iota(jnp.int32, (n, n), 0)
    j = lax.broadcasted_iota(jnp.int32, (n, n), 1)
    same = (i // c) == (j // c)
    return (same & ((i >= j) if lower else (i <= j))).astype(BF16)


def _gdn_gates(s_ref, alog_l, dtb_l, alog_s, dtb_s, fwd):
    raw = s_ref[0]
    n = raw.shape[0]
    lbeta = -_softplus(-raw)
    la = -jnp.exp(alog_l) * _softplus(raw + dtb_l)
    rraw = raw.T
    beta_r = _sigmoid(rraw)
    la_r = -jnp.exp(alog_s) * _softplus(rraw + dtb_s)
    low = _chunk_tri(n, GDN_CHUNK, True)
    up = _chunk_tri(n, GDN_CHUNK, False)
    if fwd:
        return lbeta, _mm_sel_l(low, la), beta_r, _mm_sel_r(la_r, up)
    return lbeta, _mm_sel_l(up, la), beta_r, _mm_sel_r(la_r, low)


def _gdn_pre(groups):
    c = GDN_CHUNK
    w = GDN_PACK * c
    ii = lax.broadcasted_iota(jnp.int32, (c, w), 0)
    jj = lax.broadcasted_iota(jnp.int32, (c, w), 1) & (c - 1)
    masks = {True: (ii >= jj, ii > jj), False: (ii <= jj, ii < jj)}
    for g in groups:
        g["gc_p"] = _pack_cols(g["gcs"])
        g["gcb_p"] = _pack_cols([gc + lb for gc, lb in zip(g["gcs"], g["lbs"])])
        g["gr_p"] = jnp.concatenate(g["grs"], 1)
        g["beta_rp"] = jnp.concatenate(g["brs"], 1)
        g["kbd"] = _bd_blocks([g["k4"][:, i * GDN_DK:(i + 1) * GDN_DK] for i in range(GDN_PACK)])
    for g in groups:
        incl, strict = masks[g["fwd"]]
        g["a_p"] = _mm_nt(g["k4"], g["kbd"]) * jnp.exp(jnp.where(strict, g["gcb_p"] - g["gr_p"], -jnp.inf))
        g["qk_p"] = _mm_nt(g["q4"], g["kbd"]) * jnp.exp(jnp.where(incl, g["gc_p"] - g["gr_p"], -jnp.inf))
    for level in range(c.bit_length() - 1):
        bi, bj = ii >> level, jj >> level
        pair = {True: ((bi & 1) == 1) & (bj == bi - 1), False: ((bi & 1) == 0) & (bj == bi + 1)}
        offs = [jnp.where(pair[g["fwd"]], g["a_p"], 0.0) for g in groups]
        if level == 0:
            eye = (ii == jj).astype(F32)
            for g, off in zip(groups, offs):
                g["inv"] = eye - off
        else:
            tmps = [jnp.dot(g["inv"].astype(BF16), _bd4(off), preferred_element_type=F32)
                    for g, off in zip(groups, offs)]
            for g, tmp in zip(groups, tmps):
                g["inv"] = g["inv"] - jnp.dot(tmp.astype(BF16), _bd4(g["inv"]), preferred_element_type=F32)
    for g in groups:
        vbd = _bd_blocks([g["v4"][:, i * GDN_DV:(i + 1) * GDN_DV] for i in range(GDN_PACK)])
        g["u4"] = _mm(g["inv"] * g["beta_rp"], vbd)
        g["w4"] = _mm(g["inv"] * (g["beta_rp"] * jnp.exp(g["gr_p"])), g["kbd"])


def _gdn_pair_steps(items, states):
    c = GDN_CHUNK
    rs = []
    for it, s2 in zip(items, states):
        sbd = _bd_blocks([s2[:, :GDN_DV].astype(BF16), s2[:, GDN_DV:].astype(BF16)])
        egs = [jnp.exp(g) for g in it["gcs"]]
        qd2 = jnp.concatenate([it["q2"][:, i * GDN_DK:(i + 1) * GDN_DK] * egs[i] for i in range(2)], 1)
        rs.append(_mm(jnp.concatenate([it["w2"], qd2], 0), sbd))
    outs, news = [], []
    for it, s2, r in zip(items, states, rs):
        g_lasts = [g[c - 1:c] if it["fwd"] else g[0:1] for g in it["gcs"]]
        vn2 = it["u2"] - r[:c]
        vnbd = _bd_blocks([vn2[:, :GDN_DV].astype(BF16), vn2[:, GDN_DV:].astype(BF16)])
        outs.append(r[c:] + _mm(it["qk2"], vnbd))
        kd = jnp.concatenate([it["k2"][:, i * GDN_DK:(i + 1) * GDN_DK] * jnp.exp(g_lasts[i] - it["gcs"][i])
                              for i in range(2)], 0)
        cd2 = jnp.concatenate([jnp.broadcast_to(jnp.exp(gl), (1, GDN_DV)) for gl in g_lasts], 1)
        news.append(s2 * cd2 + _mm_tn(kd, vnbd))
    return outs, news


def _gdn_kernel(qf_ref, kf_ref, vf_ref, sf_ref, qb_ref, kb_ref, vb_ref, sb_ref,
                alog_l_ref, dtb_l_ref, alog_s_ref, dtb_s_ref, s0f_ref, s0b_ref,
                of_ref, ob_ref, stf_ref, stb_ref):
    @pl.when(pl.program_id(1) == 0)
    def _():
        stf_ref[...] = s0f_ref[...]
        stb_ref[...] = s0b_ref[...]

    c = GDN_CHUNK
    nck = qf_ref.shape[1] // c
    alog_l, dtb_l, alog_s, dtb_s = alog_l_ref[...], dtb_l_ref[...], alog_s_ref[...], dtb_s_ref[...]
    groups = []
    for d, (q_ref, k_ref, v_ref, s_ref, o_ref, st_ref) in enumerate(
            ((qf_ref, kf_ref, vf_ref, sf_ref, of_ref, stf_ref),
             (qb_ref, kb_ref, vb_ref, sb_ref, ob_ref, stb_ref))):
        fwd = d == 0
        lbeta_all, gc_all, beta_r_all, gr_all = _gdn_gates(s_ref, alog_l, dtb_l, alog_s, dtb_s, fwd)
        for t in range(nck):
            ci = t if fwd else nck - 1 - t
            rows = slice(ci * c, (ci + 1) * c)
            for half in range(GDN_HEADS // GDN_PACK):
                h0 = half * GDN_PACK
                lanes_b = [SM_BETA + d * GDN_HEADS + h0 + i for i in range(GDN_PACK)]
                lanes_a = [SM_A + d * GDN_HEADS + h0 + i for i in range(GDN_PACK)]
                sl4 = slice(h0 * GDN_DK, (h0 + GDN_PACK) * GDN_DK)
                groups.append(dict(
                    fwd=fwd, t=t, rows=rows, h0=h0, o_ref=o_ref, st_ref=st_ref,
                    q4=q_ref[0, rows, sl4], k4=k_ref[0, rows, sl4], v4=v_ref[0, rows, sl4],
                    gcs=[gc_all[rows, la:la + 1] for la in lanes_a],
                    lbs=[lbeta_all[rows, lb:lb + 1] for lb in lanes_b],
                    grs=[gr_all[la:la + 1, rows] for la in lanes_a],
                    brs=[beta_r_all[lb:lb + 1, rows] for lb in lanes_b]))
    _gdn_pre(groups)

    keys = [(d, pair) for d in range(2) for pair in range(GDN_PAIRS)]
    st_refs = (stf_ref, stb_ref)
    states = [st_refs[d][0, pair] for d, pair in keys]
    for t in range(nck):
        items = []
        for d, pair in keys:
            ppg = GDN_PACK // 2
            g = next(g for g in groups if g["fwd"] == (d == 0) and g["t"] == t and g["h0"] == (pair // ppg) * GDN_PACK)
            pr = pair % ppg
            s2 = slice(pr * 2 * GDN_DK, (pr + 1) * 2 * GDN_DK)
            items.append(dict(fwd=g["fwd"], q2=g["q4"][:, s2], k2=g["k4"][:, s2], u2=g["u4"][:, s2], w2=g["w4"][:, s2],
                              qk2=g["qk_p"][:, pr * 2 * c:(pr + 1) * 2 * c], gcs=g["gcs"][2 * pr:2 * pr + 2],
                              o_ref=g["o_ref"], rows=g["rows"], pair=pair))
        outs, states = _gdn_pair_steps(items, states)
        for it, o2 in zip(items, outs):
            it["o_ref"][0, it["rows"], it["pair"] * 2 * GDN_DV:(it["pair"] + 1) * 2 * GDN_DV] = o2.astype(BF16)
    for (d, pair), s2 in zip(keys, states):
        st_refs[d][0, pair] = s2


def _lane_param(vals, offset):
    flat = vals.reshape(-1).astype(F32)
    v = jnp.zeros((LANES,), F32).at[offset:offset + flat.shape[0]].set(flat)
    return v.reshape(1, LANES), v.reshape(LANES, 1)


def _gdn_scan(qkv, small, a_log, dt_bias, s0f, s0b):
    bsz, seq, _ = qkv.shape
    blk = min(seq, GDN_STEP_CHUNKS * GDN_CHUNK)
    nb = seq // blk
    alog_l, alog_s = _lane_param(a_log, SM_A)
    dtb_l, dtb_s = _lane_param(dt_bias, SM_A)
    hw = GDN_QK_DIM

    def block_specs(bmap):
        return [pl.BlockSpec((1, blk, hw), lambda b, i: (b, bmap(i), 0)),
                pl.BlockSpec((1, blk, hw), lambda b, i: (b, bmap(i), 1)),
                pl.BlockSpec((1, blk, hw), lambda b, i: (b, bmap(i), 2)),
                pl.BlockSpec((1, blk, LANES), lambda b, i: (b, bmap(i), 0))]

    fw = lambda i: i
    bw = lambda i: nb - 1 - i
    vec_l = pl.BlockSpec((1, LANES), lambda b, i: (0, 0))
    vec_s = pl.BlockSpec((LANES, 1), lambda b, i: (0, 0))
    st_spec = pl.BlockSpec((1, GDN_PAIRS, GDN_DK, 2 * GDN_DV), lambda b, i: (b, 0, 0, 0))
    out_shape = [jax.ShapeDtypeStruct((bsz, seq, GDN_V_DIM), BF16)] * 2 + \
                [jax.ShapeDtypeStruct((bsz, GDN_PAIRS, GDN_DK, 2 * GDN_DV), F32)] * 2
    return pl.pallas_call(
        _gdn_kernel,
        grid=(bsz, nb),
        in_specs=block_specs(fw) + block_specs(bw) + [vec_l, vec_l, vec_s, vec_s, st_spec, st_spec],
        out_specs=[pl.BlockSpec((1, blk, GDN_V_DIM), lambda b, i: (b, i, 0)),
                   pl.BlockSpec((1, blk, GDN_V_DIM), lambda b, i: (b, nb - 1 - i, 0)),
                   st_spec, st_spec],
        out_shape=out_shape,
        compiler_params=_cparams(("arbitrary", "arbitrary")),
        name="gdn_scan",
    )(qkv, qkv, qkv, small, qkv, qkv, qkv, small, alog_l, dtb_l, alog_s, dtb_s, s0f, s0b)


def _ssd_gates(s_ref, alog_l, dtb_l, alog_s, dtb_s, fwd):
    c = SSD_CHUNK
    raw = s_ref[0]
    da = _softplus(raw + dtb_l) * (-jnp.exp(alog_l))
    dt_r = _softplus(raw.T + dtb_s)
    da_r = dt_r * (-jnp.exp(alog_s))
    low = _tri(c, True).astype(BF16)
    up = _tri(c, False).astype(BF16)
    if fwd:
        return _mm_sel_l(low, da), dt_r, _mm_sel_r(da_r, up)
    return _mm_sel_l(up, da), dt_r, _mm_sel_r(da_r, low)


def _ssd_kernel(xf_ref, bf_ref, cf_ref, sf_ref, xb_ref, bb_ref, cb_ref, sb_ref,
                alog_l_ref, dtb_l_ref, alog_s_ref, dtb_s_ref, s0f_ref, s0b_ref,
                yf_ref, yb_ref, stf_ref, stb_ref):
    @pl.when(pl.program_id(1) == 0)
    def _():
        stf_ref[...] = s0f_ref[...]
        stb_ref[...] = s0b_ref[...]

    c = SSD_CHUNK
    hd = SSD_HEAD_DIM
    alog_l, dtb_l, alog_s, dtb_s = alog_l_ref[...], dtb_l_ref[...], alog_s_ref[...], dtb_s_ref[...]
    lo = lax.broadcasted_iota(jnp.int32, (c, 2 * hd), 1) < hd
    chains = []
    for d, (x_ref, b_ref, c_ref, s_ref, y_ref, st_ref) in enumerate(
            ((xf_ref, bf_ref, cf_ref, sf_ref, yf_ref, stf_ref),
             (xb_ref, bb_ref, cb_ref, sb_ref, yb_ref, stb_ref))):
        fwd = d == 0
        incl = _tri(c, fwd)
        ac_all, dtr_all, ar_all = _ssd_gates(s_ref, alog_l, dtb_l, alog_s, dtb_s, fwd)
        for g in range(SSD_GROUPS):
            bm = b_ref[0, :, g * SSD_STATE:(g + 1) * SSD_STATE].astype(F32)
            cm = c_ref[0, :, g * SSD_STATE:(g + 1) * SSD_STATE].astype(F32)
            cbm = _mm_nt(cm, bm)
            bm_t = bm.T
            for e in range(0, SSD_HPG, 2):
                pair = (g * SSD_HPG + e) // 2
                lns = [SM_DT + d * SSD_HEADS + g * SSD_HPG + e + i for i in range(2)]
                ars = [ar_all[ln:ln + 1, :] for ln in lns]
                chains.append(dict(
                    incl=incl, cm=cm, cbm=cbm, bm_t=bm_t, ars=ars, dtrs=[dtr_all[ln:ln + 1, :] for ln in lns],
                    acols=[jnp.broadcast_to(ac_all[:, ln:ln + 1], (c, c)) for ln in lns],
                    a_lasts=[ar[:, c - 1:c] if fwd else ar[:, 0:1] for ar in ars],
                    x2=x_ref[0, :, pair * 2 * hd:(pair + 1) * 2 * hd], state=st_ref[0, pair],
                    y_ref=y_ref, st_ref=st_ref, pair=pair))
    ys, sts = [], []
    for ch in chains:
        x2 = ch["x2"]
        zx = jnp.zeros_like(x2)
        xs = [jnp.where(lo, x2, zx), jnp.where(lo, zx, x2)]
        s2 = ch["state"].astype(BF16)
        zs = jnp.zeros_like(s2)
        ss = [jnp.where(lo, s2, zs), jnp.where(lo, zs, s2)]
        lhs, rhs, upd = [], [], []
        for i in range(2):
            lmat = jnp.exp(jnp.where(ch["incl"], ch["acols"][i] - ch["ars"][i], -jnp.inf))
            lhs += [(ch["cbm"] * lmat * ch["dtrs"][i]).astype(BF16), (ch["cm"] * jnp.exp(ch["acols"][i])).astype(BF16)]
            rhs += [xs[i], ss[i]]
            upd.append((ch["bm_t"] * (ch["dtrs"][i] * jnp.exp(ch["a_lasts"][i] - ch["ars"][i]))).astype(BF16))
        ys.append(jnp.dot(jnp.concatenate(lhs, 1), jnp.concatenate(rhs, 0), preferred_element_type=F32))
        decay2 = jnp.concatenate([jnp.broadcast_to(jnp.exp(al), (1, hd)) for al in ch["a_lasts"]], 1)
        sts.append(ch["state"] * decay2
                   + jnp.dot(jnp.concatenate(upd, 1), jnp.concatenate(xs, 0), preferred_element_type=F32))
    for ch, y, st in zip(chains, ys, sts):
        ch["y_ref"][0, :, ch["pair"] * 2 * hd:(ch["pair"] + 1) * 2 * hd] = y.astype(BF16)
        ch["st_ref"][0, ch["pair"]] = st


def _ssd_scan(xbc, small, a_log, dt_bias, s0f, s0b):
    bsz, seq, _ = xbc.shape
    c = SSD_CHUNK
    nc = seq // c
    alog_l, alog_s = _lane_param(a_log, SM_DT)
    dtb_l, dtb_s = _lane_param(dt_bias, SM_DT)
    gn = SSD_GROUPS * SSD_STATE

    def chunk_specs(cmap):
        return [pl.BlockSpec((1, c, SSD_D_INNER), lambda b, i: (b, cmap(i), C_XBC // SSD_D_INNER)),
                pl.BlockSpec((1, c, gn), lambda b, i: (b, cmap(i), (C_XBC + SSD_D_INNER) // gn)),
                pl.BlockSpec((1, c, gn), lambda b, i: (b, cmap(i), (C_XBC + SSD_D_INNER) // gn + 1)),
                pl.BlockSpec((1, c, LANES), lambda b, i: (b, cmap(i), 0))]

    fw = lambda i: i
    bw = lambda i: nc - 1 - i
    vec_l = pl.BlockSpec((1, LANES), lambda b, i: (0, 0))
    vec_s = pl.BlockSpec((LANES, 1), lambda b, i: (0, 0))
    st_spec = pl.BlockSpec((1, SSD_HEADS // 2, SSD_STATE, 2 * SSD_HEAD_DIM), lambda b, i: (b, 0, 0, 0))
    out_shape = [jax.ShapeDtypeStruct((bsz, seq, SSD_D_INNER), BF16)] * 2 + \
                [jax.ShapeDtypeStruct((bsz, SSD_HEADS // 2, SSD_STATE, 2 * SSD_HEAD_DIM), F32)] * 2
    return pl.pallas_call(
        _ssd_kernel,
        grid=(bsz, nc),
        in_specs=chunk_specs(fw) + chunk_specs(bw) + [vec_l, vec_l, vec_s, vec_s, st_spec, st_spec],
        out_specs=[pl.BlockSpec((1, c, SSD_D_INNER), lambda b, i: (b, i, 0)),
                   pl.BlockSpec((1, c, SSD_D_INNER), lambda b, i: (b, nc - 1 - i, 0)),
                   st_spec, st_spec],
        out_shape=out_shape,
        compiler_params=_cparams(("arbitrary", "arbitrary")),
        name="ssd_scan",
    )(xbc, xbc, xbc, small, xbc, xbc, xbc, small, alog_l, dtb_l, alog_s, dtb_s, s0f, s0b)


def _kv_prep_kernel(k_ref, v_ref, nw_ref, cos_ref, sin_ref, ko_ref, vo_ref, *, rope):
    parts = []
    for h in range(ATT_KV_HEADS):
        kh = _rms(k_ref[0, :, h * ATT_HEAD_DIM:(h + 1) * ATT_HEAD_DIM].astype(F32), nw_ref[...])
        if rope:
            kh = _rope(kh, cos_ref[...], sin_ref[...])
        parts.append(kh)
    ko_ref[0] = jnp.concatenate(parts, axis=-1).astype(BF16)
    v = v_ref[0]
    ones_blk = (lax.broadcasted_iota(jnp.int32, (v.shape[0], ATT_HEAD_DIM), 1) == 0).astype(BF16)
    vparts = []
    for h in range(ATT_KV_HEADS):
        vparts += [v[:, h * ATT_HEAD_DIM:(h + 1) * ATT_HEAD_DIM].astype(BF16), ones_blk]
    vo_ref[0] = jnp.concatenate(vparts, axis=-1)


def _kv_prep(p, k_norm, cos, sin, rope, ts):
    bsz, seq, _ = p.shape
    kb = C_AKV // ATT_KV_DIM
    tab = pl.BlockSpec((ts, ATT_HEAD_DIM), lambda b, i: (i if rope else 0, 0))
    return pl.pallas_call(
        functools.partial(_kv_prep_kernel, rope=rope),
        grid=(bsz, seq // ts),
        in_specs=[pl.BlockSpec((1, ts, ATT_KV_DIM), lambda b, i: (b, i, kb)),
                  pl.BlockSpec((1, ts, ATT_KV_DIM), lambda b, i: (b, i, kb + 1)),
                  pl.BlockSpec((1, ATT_HEAD_DIM), lambda b, i: (0, 0)), tab, tab],
        out_specs=[pl.BlockSpec((1, ts, ATT_KV_DIM), lambda b, i: (b, i, 0)),
                   pl.BlockSpec((1, ts, 2 * ATT_KV_DIM), lambda b, i: (b, i, 0))],
        out_shape=[jax.ShapeDtypeStruct((bsz, seq, ATT_KV_DIM), BF16),
                   jax.ShapeDtypeStruct((bsz, seq, 2 * ATT_KV_DIM), BF16)],
        compiler_params=_cparams(("arbitrary", "arbitrary")),
        name="kv_prep",
    )(p, p, k_norm.reshape(1, ATT_HEAD_DIM), cos, sin)


def _attn_kernel(*refs, n_seg, rope):
    q_ref, nw_ref, cos_ref, sin_ref = refs[:4]
    kv_refs = refs[4:4 + 2 * n_seg]
    o_ref = refs[4 + 2 * n_seg]
    qscale = ATT_HEAD_DIM ** -0.5 * LOG2E
    heads = range(ATT_REP)
    qs = []
    for r in heads:
        qh = _rms(q_ref[0, :, r * ATT_HEAD_DIM:(r + 1) * ATT_HEAD_DIM].astype(F32), nw_ref[...])
        if rope:
            qh = _rope(qh, cos_ref[...], sin_ref[...])
        qs.append((qh * qscale).astype(BF16))
    scores = [[_mm_nt(qs[r], kv_refs[2 * s][0]) for s in range(n_seg)] for r in heads]
    maxes = [functools.reduce(jnp.maximum, [jnp.max(sc, axis=-1, keepdims=True) for sc in scores[r]]) for r in heads]
    accs = [functools.reduce(jnp.add, [_mm(jnp.exp2(scores[r][s] - maxes[r]), kv_refs[2 * s + 1][0])
                                        for s in range(n_seg)]) for r in heads]
    for r in heads:
        o_ref[0, :, r * ATT_HEAD_DIM:(r + 1) * ATT_HEAD_DIM] = (
            accs[r][:, :ATT_HEAD_DIM] / accs[r][:, ATT_HEAD_DIM:ATT_HEAD_DIM + 1]).astype(o_ref.dtype)


def _attention(p, q_norm, cos, sin, kv_segs, rope, tq):
    bsz, seq, _ = p.shape
    gw = ATT_REP * ATT_HEAD_DIM
    qb = C_AQ // gw
    n_seg = len(kv_segs)
    tab = pl.BlockSpec((tq, ATT_HEAD_DIM), lambda b, g, i: (i if rope else 0, 0))
    in_specs = [pl.BlockSpec((1, tq, gw), lambda b, g, i: (b, i, qb + g)),
                pl.BlockSpec((1, ATT_HEAD_DIM), lambda b, g, i: (0, 0)), tab, tab]
    args = [p, q_norm.reshape(1, ATT_HEAD_DIM), cos, sin]
    for k_arr, v_arr in kv_segs:
        lk = k_arr.shape[1]
        in_specs += [pl.BlockSpec((1, lk, ATT_HEAD_DIM), lambda b, g, i: (b, 0, g)),
                     pl.BlockSpec((1, lk, 2 * ATT_HEAD_DIM), lambda b, g, i: (b, 0, g))]
        args += [k_arr, v_arr]
    return pl.pallas_call(
        functools.partial(_attn_kernel, n_seg=n_seg, rope=rope),
        grid=(bsz, ATT_KV_HEADS, seq // tq),
        in_specs=in_specs,
        out_specs=pl.BlockSpec((1, tq, gw), lambda b, g, i: (b, i, g)),
        out_shape=jax.ShapeDtypeStruct((bsz, seq, ATT_Q_DIM), BF16),
        compiler_params=_cparams(("arbitrary", "arbitrary", "arbitrary")),
        name="attention",
    )(*args)


def _merge_kernel(x_ref, mod_ref, of_ref, ob_ref, gz_ref, yf_ref, yb_ref, xs_ref, sz_ref, att_ref,
                  gg_ref, gs_ref, ga_ref, gnw_ref, snw_ref, dsk_ref, wg_ref, ws_ref, wa_ref, wo_ref, o_ref):
    o = of_ref[0].astype(F32) + ob_ref[0].astype(F32)
    gz = gz_ref[0].astype(F32)
    parts = []
    for h in range(GDN_HEADS):
        sl = slice(h * GDN_DV, (h + 1) * GDN_DV)
        parts.append(_rms(o[:, sl], gnw_ref[...]) * _silu(gz[:, sl]))
    y_gdn = jnp.concatenate(parts, axis=-1)

    y = yf_ref[0].astype(F32) + yb_ref[0].astype(F32) + dsk_ref[...] * xs_ref[0].astype(F32)
    y = y * _silu(sz_ref[0].astype(F32))
    snw = snw_ref[...]
    gw = SSD_D_INNER // SSD_GROUPS
    y_ssd = jnp.concatenate([_rms(y[:, g * gw:(g + 1) * gw], snw[:, g * gw:(g + 1) * gw])
                             for g in range(SSD_GROUPS)], axis=-1)

    gate = lambda ref: _sigmoid(ref[0].astype(F32))
    m = (gate(gg_ref) * _mm(y_gdn, wg_ref[...]) + gate(gs_ref) * _mm(y_ssd, ws_ref[...])
         + gate(ga_ref) * _mm(att_ref[0], wa_ref[...]))
    g1 = mod_ref[0][2:3]
    o_ref[0] = x_ref[0] + g1 * _mm(m, wo_ref[...])


def _merge(x, mod_l, mod_row, o_f, o_b, y_f, y_b, att, p, gdn_norm, ssd_norm, ssd_d, wg, ws, wa, wo, layer, tm):
    bsz, seq, d = x.shape
    row = lambda cb: pl.BlockSpec((1, tm, d), lambda b, i: (b, i, cb))
    vec = lambda n: pl.BlockSpec((1, n), lambda b, i: (0, 0))
    wsp = pl.BlockSpec((None, d, d), lambda b, i: (layer, 0, 0), pipeline_mode=pl.Buffered(1))
    return pl.pallas_call(
        _merge_kernel,
        grid=(bsz, seq // tm),
        in_specs=[row(0), pl.BlockSpec((1, 6, d), lambda b, i: (mod_row(b), 0, 0)),
                  row(0), row(0), row(C_GZ // d), row(0), row(0), row(C_XBC // d), row(C_SZ // d), row(0),
                  row(C_GATE // d), row(C_GATE // d + 1), row(C_GATE // d + 2),
                  vec(GDN_DV), vec(d), vec(d), wsp, wsp, wsp, wsp],
        out_specs=row(0),
        out_shape=jax.ShapeDtypeStruct((bsz, seq, d), F32),
        compiler_params=_cparams(("arbitrary", "arbitrary")),
        name="merge",
    )(x, mod_l, o_f, o_b, p, y_f, y_b, p, p, att, p, p, p,
      gdn_norm.reshape(1, GDN_DV), ssd_norm.reshape(1, d),
      jnp.repeat(ssd_d, SSD_HEAD_DIM).reshape(1, d), wg, ws, wa, wo)


def _mlp_kernel(x_ref, mod_ref, nw_ref, w1_ref, w2_ref, o_ref, h_ref, acc_ref):
    k = pl.program_id(2)
    m = mod_ref[0]

    @pl.when(k == 0)
    def _():
        h_ref[...] = _norm_mod(x_ref[0], nw_ref[...], m[4:5], m[3:4]).astype(BF16)
        acc_ref[...] = jnp.zeros_like(acc_ref)

    a = jnp.maximum(jnp.dot(h_ref[...], w1_ref[...], preferred_element_type=F32), 0.0)
    acc_ref[...] += _mm(a * a, w2_ref[...])

    @pl.when(k == pl.num_programs(2) - 1)
    def _():
        o_ref[0] = x_ref[0] + m[5:6] * acc_ref[...]


def _mlp(x, mod_l, mod_row, nw, w1, w2, layer, tm, tf):
    bsz, seq, d = x.shape
    return pl.pallas_call(
        _mlp_kernel,
        grid=(bsz, seq // tm, D_FF // tf),
        in_specs=[pl.BlockSpec((1, tm, d), lambda b, i, k: (b, i, 0)),
                  pl.BlockSpec((1, 6, d), lambda b, i, k: (mod_row(b), 0, 0)),
                  pl.BlockSpec((1, d), lambda b, i, k: (0, 0)),
                  pl.BlockSpec((None, d, tf), lambda b, i, k: (layer, 0, k)),
                  pl.BlockSpec((None, tf, d), lambda b, i, k: (layer, k, 0))],
        out_specs=pl.BlockSpec((1, tm, d), lambda b, i, k: (b, i, 0)),
        out_shape=jax.ShapeDtypeStruct((bsz, seq, d), F32),
        scratch_shapes=[pltpu.VMEM((tm, d), BF16), pltpu.VMEM((tm, d), F32)],
        compiler_params=_cparams(("arbitrary", "arbitrary", "arbitrary")),
        name="mlp",
    )(x, mod_l, nw.reshape(1, d), w1, w2)


def _reorder_w_in(w_in):
    depth, d, _ = w_in.shape
    o = 0
    seg = {}
    for name, size in (("qkv", GDN_QKV), ("gz", GDN_V_DIM), ("beta", 2 * GDN_HEADS), ("a", 2 * GDN_HEADS),
                       ("sz", SSD_D_INNER), ("xbc", SSD_XBC), ("dt", 2 * SSD_HEADS),
                       ("aq", ATT_Q_DIM), ("akv", 2 * ATT_KV_DIM), ("gate", 3 * D_MODEL)):
        seg[name] = w_in[:, :, o:o + size]
        o += size
    pad = jnp.zeros((depth, d, NP_COLS - (C_SMALL + SM_DT + 2 * SSD_HEADS)), w_in.dtype)
    out = jnp.concatenate([seg["qkv"], seg["xbc"], seg["akv"], seg["gz"], seg["sz"], seg["aq"], seg["gate"],
                           seg["beta"], seg["a"], seg["dt"], pad], axis=-1)
    return out.astype(BF16)


def _rope_tables(seq):
    t = jnp.arange(seq, dtype=jnp.int32)
    q = ATT_HEAD_DIM // 4
    freqs = ROPE_THETA ** (-jnp.arange(q, dtype=F32) / q)
    ang_r = (t // GRID_W).astype(F32)[:, None] * freqs[None, :]
    ang_c = (t % GRID_W).astype(F32)[:, None] * freqs[None, :]
    cos = jnp.concatenate([jnp.cos(ang_r)] * 2 + [jnp.cos(ang_c)] * 2, axis=-1)
    sin = jnp.concatenate([-jnp.sin(ang_r), jnp.sin(ang_r), -jnp.sin(ang_c), jnp.sin(ang_c)], axis=-1)
    return cos, sin


def _tile(seq, want):
    return min(seq, want)


def kernel(x, c, ctx, c_ctx, w_mod, b_mod, norm_mix, norm_mlp, w_in, gdn_conv, gdn_a_log, gdn_dt_bias, gdn_norm,
           ssd_conv_w, ssd_conv_b, ssd_a_log, ssd_dt_bias, ssd_d, ssd_norm, att_q_norm, att_k_norm,
           w_br_gdn, w_br_ssd, w_br_att, w_out, w_ff1, w_ff2):
    bsz, seq, d = x.shape
    ctx_len = ctx.shape[1]
    depth = w_in.shape[0]
    assert bsz < MOD_ROWS and d == D_MODEL

    cc = jnp.zeros((MOD_ROWS, d), F32).at[:bsz].set(c).at[bsz].set(c_ctx)
    mod = _modulation(cc, w_mod, b_mod).reshape(depth, MOD_ROWS, 6, d)
    lat_row = lambda b: b
    ctx_row = lambda b: bsz

    w_in_r = _reorder_w_in(w_in)
    wg, ws, wa, wo = (w.astype(BF16) for w in (w_br_gdn, w_br_ssd, w_br_att, w_out))
    w1, w2 = w_ff1.astype(BF16), w_ff2.astype(BF16)
    cos, sin = _rope_tables(seq)

    zg = jnp.zeros((bsz, GDN_PAIRS, GDN_DK, 2 * GDN_DV), F32)
    zs = jnp.zeros((bsz, SSD_HEADS // 2, SSD_STATE, 2 * SSD_HEAD_DIM), F32)
    conv_w = jnp.concatenate([gdn_conv, ssd_conv_w], axis=-1)
    conv_b = jnp.concatenate([jnp.zeros((depth, 1, GDN_QKV), F32), ssd_conv_b[:, None, :]], axis=-1)
    xc = ctx
    for l in range(depth):
        last = l == depth - 1
        p_lat, sm_lat = _in_projection(x, mod[l], lat_row, norm_mix[l], w_in_r, conv_w[l], conv_b[l], l,
                                       _tile(seq, 1024))
        p_ctx, sm_ctx = _in_projection(xc, mod[l], ctx_row, norm_mix[l], w_in_r, conv_w[l], conv_b[l], l,
                                       _tile(ctx_len, 1024))

        ogf_c, ogb_c, sgf, sgb = _gdn_scan(p_ctx, sm_ctx, gdn_a_log[l], gdn_dt_bias[l], zg, zg)
        ogf_l, ogb_l, _, _ = _gdn_scan(p_lat, sm_lat, gdn_a_log[l], gdn_dt_bias[l], sgf, sgb)
        ysf_c, ysb_c, ssf, ssb = _ssd_scan(p_ctx, sm_ctx, ssd_a_log[l], ssd_dt_bias[l], zs, zs)
        ysf_l, ysb_l, _, _ = _ssd_scan(p_lat, sm_lat, ssd_a_log[l], ssd_dt_bias[l], ssf, ssb)

        k_c, v_c = _kv_prep(p_ctx, att_k_norm[l], cos, sin, False, _tile(ctx_len, 256))
        k_l, v_l = _kv_prep(p_lat, att_k_norm[l], cos, sin, True, _tile(seq, 512))
        att_l = _attention(p_lat, att_q_norm[l], cos, sin, [(k_c, v_c), (k_l, v_l)], True, _tile(seq, 256))

        x = _merge(x, mod[l], lat_row, ogf_l, ogb_l, ysf_l, ysb_l, att_l, p_lat,
                   gdn_norm[l], ssd_norm[l], ssd_d[l], wg, ws, wa, wo, l, _tile(seq, 512))
        x = _mlp(x, mod[l], lat_row, norm_mlp[l], w1, w2, l, _tile(seq, 1024), 1024)

        if not last:
            att_c = _attention(p_ctx, att_q_norm[l], cos, sin, [(k_c, v_c)], False, _tile(ctx_len, 256))
            xc = _merge(xc, mod[l], ctx_row, ogf_c, ogb_c, ysf_c, ysb_c, att_c, p_ctx,
                        gdn_norm[l], ssd_norm[l], ssd_d[l], wg, ws, wa, wo, l, _tile(ctx_len, 256))
            xc = _mlp(xc, mod[l], ctx_row, norm_mlp[l], w1, w2, l, _tile(ctx_len, 512), 1024)
    return x
```

```python
import functools

import jax
import jax.numpy as jnp
from jax import lax
from jax.experimental import pallas as pl
from jax.experimental.pallas import tpu as pltpu

F32 = jnp.float32
BF16 = jnp.bfloat16

D_MODEL = 1024
GRID_W = 64
EPS = 1e-6

GDN_HEADS = 8
GDN_DK = 128
GDN_DV = 128
GDN_CHUNK = 64
GDN_QK_DIM = GDN_HEADS * GDN_DK
GDN_V_DIM = GDN_HEADS * GDN_DV
GDN_QKV = 2 * GDN_QK_DIM + GDN_V_DIM

SSD_D_INNER = D_MODEL
SSD_HEAD_DIM = 64
SSD_HEADS = SSD_D_INNER // SSD_HEAD_DIM
SSD_GROUPS = 2
SSD_HPG = SSD_HEADS // SSD_GROUPS
SSD_STATE = 128
SSD_CHUNK = 128
SSD_XBC = SSD_D_INNER + 2 * SSD_GROUPS * SSD_STATE

ATT_HEADS = 8
ATT_KV_HEADS = 2
ATT_REP = ATT_HEADS // ATT_KV_HEADS
ATT_HEAD_DIM = 128
ATT_Q_DIM = ATT_HEADS * ATT_HEAD_DIM
ATT_KV_DIM = ATT_KV_HEADS * ATT_HEAD_DIM
ROPE_THETA = 10000.0
LOG2E = 1.4426950408889634
D_FF = 4 * D_MODEL

LANES = 128
HALO = 16
MOD_ROWS = 16

C_QKV = 0
C_XBC = C_QKV + GDN_QKV
C_AKV = C_XBC + SSD_XBC
C_GZ = C_AKV + 2 * ATT_KV_DIM
C_SZ = C_GZ + GDN_V_DIM
C_AQ = C_SZ + SSD_D_INNER
C_GATE = C_AQ + ATT_Q_DIM
C_SMALL = C_GATE + 3 * D_MODEL
NP_COLS = 11520
N_TILES = 5
CONV_TILES = 2
CONV_COLS = GDN_QKV + SSD_XBC
CONV_SUB = 256
SM_BETA = 0
SM_A = 2 * GDN_HEADS
SM_DT = 4 * GDN_HEADS

VMEM_LIMIT = 56 * 1024 * 1024


def _cparams(sem):
    return pltpu.CompilerParams(dimension_semantics=sem, vmem_limit_bytes=VMEM_LIMIT)


def _mm(a, b):
    return jnp.dot(a.astype(BF16), b.astype(BF16), preferred_element_type=F32)


def _mm_nt(a, b):
    return lax.dot_general(a.astype(BF16), b.astype(BF16), (((1,), (1,)), ((), ())),
                           preferred_element_type=F32)


def _mm_tn(a, b):
    return lax.dot_general(a.astype(BF16), b.astype(BF16), (((0,), (0,)), ((), ())),
                           preferred_element_type=F32)


def _split3(x):
    hi = x.astype(BF16)
    r = x - hi.astype(F32)
    mid = r.astype(BF16)
    lo = (r - mid.astype(F32)).astype(BF16)
    return hi, mid, lo


def _mm_sel_l(sel, x):
    hi, mid, lo = _split3(x)
    d = lambda p: jnp.dot(sel, p, preferred_element_type=F32)
    return (d(hi) + d(mid)) + d(lo)


def _mm_sel_r(x, sel):
    hi, mid, lo = _split3(x)
    d = lambda p: jnp.dot(p, sel, preferred_element_type=F32)
    return (d(hi) + d(mid)) + d(lo)


def _sigmoid(x):
    return 1.0 / (1.0 + jnp.exp(-x))


def _silu(x):
    return x * _sigmoid(x)


def _softplus(x):
    return jnp.maximum(x, 0.0) + jnp.log(1.0 + jnp.exp(-jnp.abs(x)))


def _rms(x, w):
    return x * lax.rsqrt(jnp.mean(x * x, axis=-1, keepdims=True) + EPS) * w


def _norm_mod(x, nw, scale, shift):
    return _rms(x, nw) * (1.0 + scale) + shift


def _tri(n, lower):
    i = lax.broadcasted_iota(jnp.int32, (n, n), 0)
    j = lax.broadcasted_iota(jnp.int32, (n, n), 1)
    return (i >= j) if lower else (i <= j)


def _rope(x, cos, sin):
    lane = lax.broadcasted_iota(jnp.int32, x.shape, 1)
    q = LANES // 4
    swapped = jnp.where((lane & q) == 0, pltpu.roll(x, LANES - q, 1), pltpu.roll(x, q, 1))
    return x * cos + swapped * sin


def _mod_kernel(c_ref, w_ref, b_ref, o_ref):
    o_ref[0] = _mm(_silu(c_ref[...]), w_ref[0]) + b_ref[0]


def _modulation(cc, w_mod, b_mod):
    depth = w_mod.shape[0]
    n = w_mod.shape[2]
    tn = D_MODEL
    return pl.pallas_call(
        _mod_kernel,
        grid=(depth, n // tn),
        in_specs=[pl.BlockSpec((MOD_ROWS, D_MODEL), lambda l, j: (0, 0)),
                  pl.BlockSpec((1, D_MODEL, tn), lambda l, j: (l, 0, j)),
                  pl.BlockSpec((1, 1, tn), lambda l, j: (l, 0, j))],
        out_specs=pl.BlockSpec((1, MOD_ROWS, tn), lambda l, j: (l, 0, j)),
        out_shape=jax.ShapeDtypeStruct((depth, MOD_ROWS, n), F32),
        compiler_params=_cparams(("arbitrary", "arbitrary")),
        name="modulation",
    )(cc, w_mod, b_mod.reshape(depth, 1, n))


def _conv_tile(h_ref, w_ref, cw_ref, cb_ref, o_ref, tm, first_tile, last_tile, l2_blocks, with_bias):
    sub = min(tm, CONV_SUB)
    nsub = tm // sub
    tn = w_ref.shape[1]
    cw = cw_ref[...]
    rid = lax.broadcasted_iota(jnp.int32, (sub, tn), 0)
    for r in range(nsub):
        p = jnp.dot(h_ref[r * sub:r * sub + sub + 2 * HALO], w_ref[...], preferred_element_type=F32)
        n = sub + 2 * HALO
        xm1 = pltpu.roll(p, 1, 0)[HALO:HALO + sub]
        xp1 = pltpu.roll(p, n - 1, 0)[HALO:HALO + sub]
        if r == 0:
            xm1 = jnp.where((rid == 0) & first_tile, 0.0, xm1)
        if r == nsub - 1:
            xp1 = jnp.where((rid == sub - 1) & last_tile, 0.0, xp1)
        y = cw[0:1] * xm1 + cw[1:2] * p[HALO:HALO + sub] + cw[2:3] * xp1
        y = _silu(y + cb_ref[...] if with_bias else y)
        if l2_blocks:
            parts = []
            for blk in range(tn // LANES):
                yb = y[:, blk * LANES:(blk + 1) * LANES]
                if blk < l2_blocks:
                    inv = lax.rsqrt(jnp.sum(yb * yb, axis=-1, keepdims=True) + EPS)
                    yb = yb * (inv * GDN_DK ** -0.5 if blk < GDN_HEADS else inv)
                parts.append(yb)
            y = jnp.concatenate(parts, axis=-1)
        o_ref[0, r * sub:(r + 1) * sub, :] = y.astype(o_ref.dtype)


def _inproj_kernel(x_ref, xp_ref, xn_ref, mod_ref, nw_ref, w_ref, cw_ref, cb_ref, o_ref, sm_ref, h_ref):
    i = pl.program_id(1)
    j = pl.program_id(2)
    tm = x_ref.shape[1]

    @pl.when(j == 0)
    def _():
        m = mod_ref[0]
        norm = lambda x: _norm_mod(x, nw_ref[...], m[1:2], m[0:1]).astype(BF16)
        h_ref[0:HALO] = norm(xp_ref[0])
        h_ref[HALO:HALO + tm] = norm(x_ref[0])
        h_ref[HALO + tm:2 * HALO + tm] = norm(xn_ref[0])

    first_tile = i == 0
    last_tile = i == pl.num_programs(1) - 1

    @pl.when(j == 0)
    def _():
        _conv_tile(h_ref, w_ref, cw_ref, cb_ref, o_ref, tm, first_tile, last_tile, 2 * GDN_HEADS, False)

    @pl.when((j > 0) & (j < CONV_TILES))
    def _():
        _conv_tile(h_ref, w_ref, cw_ref, cb_ref, o_ref, tm, first_tile, last_tile, 0, True)

    @pl.when(j >= CONV_TILES)
    def _():
        p = jnp.dot(h_ref[HALO:HALO + tm], w_ref[...], preferred_element_type=F32)
        o_ref[0] = p.astype(o_ref.dtype)

        @pl.when(j == pl.num_programs(2) - 1)
        def _():
            off = C_SMALL - (NP_COLS // N_TILES) * (N_TILES - 1)
            sm_ref[0] = p[:, off:off + LANES]


def _in_projection(x, mod_l, mod_row, nw, w_r, conv_w, conv_b, layer, tm):
    bsz, seq, d = x.shape
    tn = NP_COLS // N_TILES
    assert CONV_COLS == CONV_TILES * tn and 2 * GDN_QK_DIM <= tn <= GDN_QKV
    nrb = seq // HALO
    rpb = tm // HALO
    ct = lambda j: jnp.minimum(j, CONV_TILES - 1)
    return pl.pallas_call(
        _inproj_kernel,
        grid=(bsz, seq // tm, N_TILES),
        in_specs=[pl.BlockSpec((1, tm, d), lambda b, i, j: (b, i, 0)),
                  pl.BlockSpec((1, HALO, d), lambda b, i, j: (b, jnp.maximum(i * rpb - 1, 0), 0)),
                  pl.BlockSpec((1, HALO, d), lambda b, i, j: (b, jnp.minimum((i + 1) * rpb, nrb - 1), 0)),
                  pl.BlockSpec((1, 6, d), lambda b, i, j: (mod_row(b), 0, 0)),
                  pl.BlockSpec((1, d), lambda b, i, j: (0, 0)),
                  pl.BlockSpec((None, d, tn), lambda b, i, j: (layer, 0, j)),
                  pl.BlockSpec((3, tn), lambda b, i, j: (0, ct(j))),
                  pl.BlockSpec((1, tn), lambda b, i, j: (0, ct(j)))],
        out_specs=[pl.BlockSpec((1, tm, tn), lambda b, i, j: (b, i, j)),
                   pl.BlockSpec((1, tm, LANES), lambda b, i, j: (b, i, 0))],
        out_shape=[jax.ShapeDtypeStruct((bsz, seq, NP_COLS), BF16),
                   jax.ShapeDtypeStruct((bsz, seq, LANES), F32)],
        scratch_shapes=[pltpu.VMEM((tm + 2 * HALO, d), BF16)],
        compiler_params=_cparams(("arbitrary", "arbitrary", "arbitrary")),
        name="in_projection",
    )(x, x, x, mod_l, nw.reshape(1, d), w_r, conv_w, conv_b)


GDN_PACK = 2
GDN_STEP_CHUNKS = 4
GDN_PAIRS = GDN_HEADS // 2


def _bd4(x):
    x = x.astype(BF16)
    n = x.shape[0]
    tiles = x.shape[1] // LANES
    lo = lax.broadcasted_iota(jnp.int32, (n, LANES), 1) < LANES // 2
    z = jnp.zeros((n, LANES), BF16)
    rows = []
    for t in range(tiles):
        a = x[:, t * LANES:(t + 1) * LANES]
        for half in (jnp.where(lo, a, z), jnp.where(lo, z, a)):
            rows.append(jnp.concatenate([half if tt == t else z for tt in range(tiles)], 1) if tiles > 1 else half)
    return jnp.concatenate(rows, 0)


def _bd_blocks(blocks):
    z = jnp.zeros_like(blocks[0])
    n = len(blocks)
    return jnp.concatenate([jnp.concatenate([blk if j == i else z for j in range(n)], 1)
                            for i, blk in enumerate(blocks)], 0)


def _pack_cols(cols):
    n = cols[0].shape[0]
    lo = lax.broadcasted_iota(jnp.int32, (n, LANES), 1) < LANES // 2
    bc = [jnp.broadcast_to(col, (n, LANES)) for col in cols]
    tiles = [jnp.where(lo, bc[2 * t], bc[2 * t + 1]) for t in range(len(cols) // 2)]
    return jnp.concatenate(tiles, 1) if len(tiles) > 1 else tiles[0]


def _chunk_tri(n, c, lower):
    i = lax.broadcasted_iota(jnp.int32, (n, n), 0)
    j = lax.broadcasted_iota(jnp.int32, (n, n), 1)
    same = (i // c) == (j // c)
    return (same & ((i >= j) if lower else (i <= j))).astype(BF16)


def _gdn_gates(s_ref, alog_l, dtb_l, alog_s, dtb_s, fwd):
    raw = s_ref[0]
    n = raw.shape[0]
    lbeta = -_softplus(-raw)
    la = -jnp.exp(alog_l) * _softplus(raw + dtb_l)
    rraw = raw.T
    beta_r = _sigmoid(rraw)
    la_r = -jnp.exp(alog_s) * _softplus(rraw + dtb_s)
    low = _chunk_tri(n, GDN_CHUNK, True)
    up = _chunk_tri(n, GDN_CHUNK, False)
    if fwd:
        return lbeta, _mm_sel_l(low, la), beta_r, _mm_sel_r(la_r, up)
    return lbeta, _mm_sel_l(up, la), beta_r, _mm_sel_r(la_r, low)


def _gdn_pre(groups):
    c = GDN_CHUNK
    w = GDN_PACK * c
    ii = lax.broadcasted_iota(jnp.int32, (c, w), 0)
    jj = lax.broadcasted_iota(jnp.int32, (c, w), 1) & (c - 1)
    masks = {True: (ii >= jj, ii > jj), False: (ii <= jj, ii < jj)}
    for g in groups:
        g["gc_p"] = _pack_cols(g["gcs"])
        g["gcb_p"] = _pack_cols([gc + lb for gc, lb in zip(g["gcs"], g["lbs"])])
        g["gr_p"] = jnp.concatenate(g["grs"], 1)
        g["beta_rp"] = jnp.concatenate(g["brs"], 1)
        g["kbd"] = _bd_blocks([g["k4"][:, i * GDN_DK:(i + 1) * GDN_DK] for i in range(GDN_PACK)])
    for g in groups:
        incl, strict = masks[g["fwd"]]
        g["a_p"] = _mm_nt(g["k4"], g["kbd"]) * jnp.exp(jnp.where(strict, g["gcb_p"] - g["gr_p"], -jnp.inf))
        g["qk_p"] = _mm_nt(g["q4"], g["kbd"]) * jnp.exp(jnp.where(incl, g["gc_p"] - g["gr_p"], -jnp.inf))
    for level in range(c.bit_length() - 1):
        bi, bj = ii >> level, jj >> level
        pair = {True: ((bi & 1) == 1) & (bj == bi - 1), False: ((bi & 1) == 0) & (bj == bi + 1)}
        offs = [jnp.where(pair[g["fwd"]], g["a_p"], 0.0) for g in groups]
        if level == 0:
            eye = (ii == jj).astype(F32)
            for g, off in zip(groups, offs):
                g["inv"] = eye - off
        else:
            tmps = [jnp.dot(g["inv"].astype(BF16), _bd4(off), preferred_element_type=F32)
                    for g, off in zip(groups, offs)]
            for g, tmp in zip(groups, tmps):
                g["inv"] = g["inv"] - jnp.dot(tmp.astype(BF16), _bd4(g["inv"]), preferred_element_type=F32)
    for g in groups:
        vbd = _bd_blocks([g["v4"][:, i * GDN_DV:(i + 1) * GDN_DV] for i in range(GDN_PACK)])
        g["u4"] = _mm(g["inv"] * g["beta_rp"], vbd)
        g["w4"] = _mm(g["inv"] * (g["beta_rp"] * jnp.exp(g["gr_p"])), g["kbd"])


def _gdn_pair_steps(items, states):
    c = GDN_CHUNK
    rs = []
    for it, s2 in zip(items, states):
        sbd = _bd_blocks([s2[:, :GDN_DV].astype(BF16), s2[:, GDN_DV:].astype(BF16)])
        egs = [jnp.exp(g) for g in it["gcs"]]
        qd2 = jnp.concatenate([it["q2"][:, i * GDN_DK:(i + 1) * GDN_DK] * egs[i] for i in range(2)], 1)
        rs.append(_mm(jnp.concatenate([it["w2"], qd2], 0), sbd))
    outs, news = [], []
    for it, s2, r in zip(items, states, rs):
        g_lasts = [g[c - 1:c] if it["fwd"] else g[0:1] for g in it["gcs"]]
        vn2 = it["u2"] - r[:c]
        vnbd = _bd_blocks([vn2[:, :GDN_DV].astype(BF16), vn2[:, GDN_DV:].astype(BF16)])
        outs.append(r[c:] + _mm(it["qk2"], vnbd))
        kd = jnp.concatenate([it["k2"][:, i * GDN_DK:(i + 1) * GDN_DK] * jnp.exp(g_lasts[i] - it["gcs"][i])
                              for i in range(2)], 0)
        cd2 = jnp.concatenate([jnp.broadcast_to(jnp.exp(gl), (1, GDN_DV)) for gl in g_lasts], 1)
        news.append(s2 * cd2 + _mm_tn(kd, vnbd))
    return outs, news


def _gdn_kernel(qf_ref, kf_ref, vf_ref, sf_ref, qb_ref, kb_ref, vb_ref, sb_ref,
                alog_l_ref, dtb_l_ref, alog_s_ref, dtb_s_ref, s0f_ref, s0b_ref,
                of_ref, ob_ref, stf_ref, stb_ref):
    @pl.when(pl.program_id(1) == 0)
    def _():
        stf_ref[...] = s0f_ref[...]
        stb_ref[...] = s0b_ref[...]

    c = GDN_CHUNK
    nck = qf_ref.shape[1] // c
    alog_l, dtb_l, alog_s, dtb_s = alog_l_ref[...], dtb_l_ref[...], alog_s_ref[...], dtb_s_ref[...]
    groups = []
    for d, (q_ref, k_ref, v_ref, s_ref, o_ref, st_ref) in enumerate(
            ((qf_ref, kf_ref, vf_ref, sf_ref, of_ref, stf_ref),
             (qb_ref, kb_ref, vb_ref, sb_ref, ob_ref, stb_ref))):
        fwd = d == 0
        lbeta_all, gc_all, beta_r_all, gr_all = _gdn_gates(s_ref, alog_l, dtb_l, alog_s, dtb_s, fwd)
        for t in range(nck):
            ci = t if fwd else nck - 1 - t
            rows = slice(ci * c, (ci + 1) * c)
            for half in range(GDN_HEADS // GDN_PACK):
                h0 = half * GDN_PACK
                lanes_b = [SM_BETA + d * GDN_HEADS + h0 + i for i in range(GDN_PACK)]
                lanes_a = [SM_A + d * GDN_HEADS + h0 + i for i in range(GDN_PACK)]
                sl4 = slice(h0 * GDN_DK, (h0 + GDN_PACK) * GDN_DK)
                groups.append(dict(
                    fwd=fwd, t=t, rows=rows, h0=h0, o_ref=o_ref, st_ref=st_ref,
                    q4=q_ref[0, rows, sl4], k4=k_ref[0, rows, sl4], v4=v_ref[0, rows, sl4],
                    gcs=[gc_all[rows, la:la + 1] for la in lanes_a],
                    lbs=[lbeta_all[rows, lb:lb + 1] for lb in lanes_b],
                    grs=[gr_all[la:la + 1, rows] for la in lanes_a],
                    brs=[beta_r_all[lb:lb + 1, rows] for lb in lanes_b]))
    _gdn_pre(groups)

    keys = [(d, pair) for d in range(2) for pair in range(GDN_PAIRS)]
    st_refs = (stf_ref, stb_ref)
    states = [st_refs[d][0, pair] for d, pair in keys]
    for t in range(nck):
        items = []
        for d, pair in keys:
            ppg = GDN_PACK // 2
            g = next(g for g in groups if g["fwd"] == (d == 0) and g["t"] == t and g["h0"] == (pair // ppg) * GDN_PACK)
            pr = pair % ppg
            s2 = slice(pr * 2 * GDN_DK, (pr + 1) * 2 * GDN_DK)
            items.append(dict(fwd=g["fwd"], q2=g["q4"][:, s2], k2=g["k4"][:, s2], u2=g["u4"][:, s2], w2=g["w4"][:, s2],
                              qk2=g["qk_p"][:, pr * 2 * c:(pr + 1) * 2 * c], gcs=g["gcs"][2 * pr:2 * pr + 2],
                              o_ref=g["o_ref"], rows=g["rows"], pair=pair))
        outs, states = _gdn_pair_steps(items, states)
        for it, o2 in zip(items, outs):
            it["o_ref"][0, it["rows"], it["pair"] * 2 * GDN_DV:(it["pair"] + 1) * 2 * GDN_DV] = o2.astype(BF16)
    for (d, pair), s2 in zip(keys, states):
        st_refs[d][0, pair] = s2


def _lane_param(vals, offset):
    flat = vals.reshape(-1).astype(F32)
    v = jnp.zeros((LANES,), F32).at[offset:offset + flat.shape[0]].set(flat)
    return v.reshape(1, LANES), v.reshape(LANES, 1)


def _gdn_scan(qkv, small, a_log, dt_bias, s0f, s0b):
    bsz, seq, _ = qkv.shape
    blk = min(seq, GDN_STEP_CHUNKS * GDN_CHUNK)
    nb = seq // blk
    alog_l, alog_s = _lane_param(a_log, SM_A)
    dtb_l, dtb_s = _lane_param(dt_bias, SM_A)
    hw = GDN_QK_DIM

    def block_specs(bmap):
        return [pl.BlockSpec((1, blk, hw), lambda b, i: (b, bmap(i), 0)),
                pl.BlockSpec((1, blk, hw), lambda b, i: (b, bmap(i), 1)),
                pl.BlockSpec((1, blk, hw), lambda b, i: (b, bmap(i), 2)),
                pl.BlockSpec((1, blk, LANES), lambda b, i: (b, bmap(i), 0))]

    fw = lambda i: i
    bw = lambda i: nb - 1 - i
    vec_l = pl.BlockSpec((1, LANES), lambda b, i: (0, 0))
    vec_s = pl.BlockSpec((LANES, 1), lambda b, i: (0, 0))
    st_spec = pl.BlockSpec((1, GDN_PAIRS, GDN_DK, 2 * GDN_DV), lambda b, i: (b, 0, 0, 0))
    out_shape = [jax.ShapeDtypeStruct((bsz, seq, GDN_V_DIM), BF16)] * 2 + \
                [jax.ShapeDtypeStruct((bsz, GDN_PAIRS, GDN_DK, 2 * GDN_DV), F32)] * 2
    return pl.pallas_call(
        _gdn_kernel,
        grid=(bsz, nb),
        in_specs=block_specs(fw) + block_specs(bw) + [vec_l, vec_l, vec_s, vec_s, st_spec, st_spec],
        out_specs=[pl.BlockSpec((1, blk, GDN_V_DIM), lambda b, i: (b, i, 0)),
                   pl.BlockSpec((1, blk, GDN_V_DIM), lambda b, i: (b, nb - 1 - i, 0)),
                   st_spec, st_spec],
        out_shape=out_shape,
        compiler_params=_cparams(("arbitrary", "arbitrary")),
        name="gdn_scan",
    )(qkv, qkv, qkv, small, qkv, qkv, qkv, small, alog_l, dtb_l, alog_s, dtb_s, s0f, s0b)


def _ssd_gates(s_ref, alog_l, dtb_l, alog_s, dtb_s, fwd):
    c = SSD_CHUNK
    raw = s_ref[0]
    da = _softplus(raw + dtb_l) * (-jnp.exp(alog_l))
    dt_r = _softplus(raw.T + dtb_s)
    da_r = dt_r * (-jnp.exp(alog_s))
    low = _tri(c, True).astype(BF16)
    up = _tri(c, False).astype(BF16)
    if fwd:
        return _mm_sel_l(low, da), dt_r, _mm_sel_r(da_r, up)
    return _mm_sel_l(up, da), dt_r, _mm_sel_r(da_r, low)


def _ssd_kernel(xf_ref, bf_ref, cf_ref, sf_ref, xb_ref, bb_ref, cb_ref, sb_ref,
                alog_l_ref, dtb_l_ref, alog_s_ref, dtb_s_ref, s0f_ref, s0b_ref,
                yf_ref, yb_ref, stf_ref, stb_ref):
    @pl.when(pl.program_id(1) == 0)
    def _():
        stf_ref[...] = s0f_ref[...]
        stb_ref[...] = s0b_ref[...]

    c = SSD_CHUNK
    hd = SSD_HEAD_DIM
    alog_l, dtb_l, alog_s, dtb_s = alog_l_ref[...], dtb_l_ref[...], alog_s_ref[...], dtb_s_ref[...]
    lo = lax.broadcasted_iota(jnp.int32, (c, 2 * hd), 1) < hd
    chains = []
    for d, (x_ref, b_ref, c_ref, s_ref, y_ref, st_ref) in enumerate(
            ((xf_ref, bf_ref, cf_ref, sf_ref, yf_ref, stf_ref),
             (xb_ref, bb_ref, cb_ref, sb_ref, yb_ref, stb_ref))):
        fwd = d == 0
        incl = _tri(c, fwd)
        ac_all, dtr_all, ar_all = _ssd_gates(s_ref, alog_l, dtb_l, alog_s, dtb_s, fwd)
        for g in range(SSD_GROUPS):
            bm = b_ref[0, :, g * SSD_STATE:(g + 1) * SSD_STATE].astype(F32)
            cm = c_ref[0, :, g * SSD_STATE:(g + 1) * SSD_STATE].astype(F32)
            cbm = _mm_nt(cm, bm)
            bm_t = bm.T
            for e in range(0, SSD_HPG, 2):
                pair = (g * SSD_HPG + e) // 2
                lns = [SM_DT + d * SSD_HEADS + g * SSD_HPG + e + i for i in range(2)]
                ars = [ar_all[ln:ln + 1, :] for ln in lns]
                chains.append(dict(
                    incl=incl, cm=cm, cbm=cbm, bm_t=bm_t, ars=ars, dtrs=[dtr_all[ln:ln + 1, :] for ln in lns],
                    acols=[jnp.broadcast_to(ac_all[:, ln:ln + 1], (c, c)) for ln in lns],
                    a_lasts=[ar[:, c - 1:c] if fwd else ar[:, 0:1] for ar in ars],
                    x2=x_ref[0, :, pair * 2 * hd:(pair + 1) * 2 * hd], state=st_ref[0, pair],
                    y_ref=y_ref, st_ref=st_ref, pair=pair))
    ys, sts = [], []
    for ch in chains:
        x2 = ch["x2"]
        zx = jnp.zeros_like(x2)
        xs = [jnp.where(lo, x2, zx), jnp.where(lo, zx, x2)]
        s2 = ch["state"].astype(BF16)
        zs = jnp.zeros_like(s2)
        ss = [jnp.where(lo, s2, zs), jnp.where(lo, zs, s2)]
        lhs, rhs, upd = [], [], []
        for i in range(2):
            lmat = jnp.exp(jnp.where(ch["incl"], ch["acols"][i] - ch["ars"][i], -jnp.inf))
            lhs += [(ch["cbm"] * lmat * ch["dtrs"][i]).astype(BF16), (ch["cm"] * jnp.exp(ch["acols"][i])).astype(BF16)]
            rhs += [xs[i], ss[i]]
            upd.append((ch["bm_t"] * (ch["dtrs"][i] * jnp.exp(ch["a_lasts"][i] - ch["ars"][i]))).astype(BF16))
        ys.append(jnp.dot(jnp.concatenate(lhs, 1), jnp.concatenate(rhs, 0), preferred_element_type=F32))
        decay2 = jnp.concatenate([jnp.broadcast_to(jnp.exp(al), (1, hd)) for al in ch["a_lasts"]], 1)
        sts.append(ch["state"] * decay2
                   + jnp.dot(jnp.concatenate(upd, 1), jnp.concatenate(xs, 0), preferred_element_type=F32))
    for ch, y, st in zip(chains, ys, sts):
        ch["y_ref"][0, :, ch["pair"] * 2 * hd:(ch["pair"] + 1) * 2 * hd] = y.astype(BF16)
        ch["st_ref"][0, ch["pair"]] = st


def _ssd_scan(xbc, small, a_log, dt_bias, s0f, s0b):
    bsz, seq, _ = xbc.shape
    c = SSD_CHUNK
    nc = seq // c
    alog_l, alog_s = _lane_param(a_log, SM_DT)
    dtb_l, dtb_s = _lane_param(dt_bias, SM_DT)
    gn = SSD_GROUPS * SSD_STATE

    def chunk_specs(cmap):
        return [pl.BlockSpec((1, c, SSD_D_INNER), lambda b, i: (b, cmap(i), C_XBC // SSD_D_INNER)),
                pl.BlockSpec((1, c, gn), lambda b, i: (b, cmap(i), (C_XBC + SSD_D_INNER) // gn)),
                pl.BlockSpec((1, c, gn), lambda b, i: (b, cmap(i), (C_XBC + SSD_D_INNER) // gn + 1)),
                pl.BlockSpec((1, c, LANES), lambda b, i: (b, cmap(i), 0))]

    fw = lambda i: i
    bw = lambda i: nc - 1 - i
    vec_l = pl.BlockSpec((1, LANES), lambda b, i: (0, 0))
    vec_s = pl.BlockSpec((LANES, 1), lambda b, i: (0, 0))
    st_spec = pl.BlockSpec((1, SSD_HEADS // 2, SSD_STATE, 2 * SSD_HEAD_DIM), lambda b, i: (b, 0, 0, 0))
    out_shape = [jax.ShapeDtypeStruct((bsz, seq, SSD_D_INNER), BF16)] * 2 + \
                [jax.ShapeDtypeStruct((bsz, SSD_HEADS // 2, SSD_STATE, 2 * SSD_HEAD_DIM), F32)] * 2
    return pl.pallas_call(
        _ssd_kernel,
        grid=(bsz, nc),
        in_specs=chunk_specs(fw) + chunk_specs(bw) + [vec_l, vec_l, vec_s, vec_s, st_spec, st_spec],
        out_specs=[pl.BlockSpec((1, c, SSD_D_INNER), lambda b, i: (b, i, 0)),
                   pl.BlockSpec((1, c, SSD_D_INNER), lambda b, i: (b, nc - 1 - i, 0)),
                   st_spec, st_spec],
        out_shape=out_shape,
        compiler_params=_cparams(("arbitrary", "arbitrary")),
        name="ssd_scan",
    )(xbc, xbc, xbc, small, xbc, xbc, xbc, small, alog_l, dtb_l, alog_s, dtb_s, s0f, s0b)


def _kv_prep_kernel(k_ref, v_ref, nw_ref, cos_ref, sin_ref, ko_ref, vo_ref, *, rope):
    parts = []
    for h in range(ATT_KV_HEADS):
        kh = _rms(k_ref[0, :, h * ATT_HEAD_DIM:(h + 1) * ATT_HEAD_DIM].astype(F32), nw_ref[...])
        if rope:
            kh = _rope(kh, cos_ref[...], sin_ref[...])
        parts.append(kh)
    ko_ref[0] = jnp.concatenate(parts, axis=-1).astype(BF16)
    v = v_ref[0]
    ones_blk = (lax.broadcasted_iota(jnp.int32, (v.shape[0], ATT_HEAD_DIM), 1) == 0).astype(BF16)
    vparts = []
    for h in range(ATT_KV_HEADS):
        vparts += [v[:, h * ATT_HEAD_DIM:(h + 1) * ATT_HEAD_DIM].astype(BF16), ones_blk]
    vo_ref[0] = jnp.concatenate(vparts, axis=-1)


def _kv_prep(p, k_norm, cos, sin, rope, ts):
    bsz, seq, _ = p.shape
    kb = C_AKV // ATT_KV_DIM
    tab = pl.BlockSpec((ts, ATT_HEAD_DIM), lambda b, i: (i if rope else 0, 0))
    return pl.pallas_call(
        functools.partial(_kv_prep_kernel, rope=rope),
        grid=(bsz, seq // ts),
        in_specs=[pl.BlockSpec((1, ts, ATT_KV_DIM), lambda b, i: (b, i, kb)),
                  pl.BlockSpec((1, ts, ATT_KV_DIM), lambda b, i: (b, i, kb + 1)),
                  pl.BlockSpec((1, ATT_HEAD_DIM), lambda b, i: (0, 0)), tab, tab],
        out_specs=[pl.BlockSpec((1, ts, ATT_KV_DIM), lambda b, i: (b, i, 0)),
                   pl.BlockSpec((1, ts, 2 * ATT_KV_DIM), lambda b, i: (b, i, 0))],
        out_shape=[jax.ShapeDtypeStruct((bsz, seq, ATT_KV_DIM), BF16),
                   jax.ShapeDtypeStruct((bsz, seq, 2 * ATT_KV_DIM), BF16)],
        compiler_params=_cparams(("arbitrary", "arbitrary")),
        name="kv_prep",
    )(p, p, k_norm.reshape(1, ATT_HEAD_DIM), cos, sin)


def _attn_kernel(*refs, n_seg, rope):
    q_ref, qn_ref, nw_ref, cos_ref, sin_ref, cosn_ref, sinn_ref = refs[:7]
    kv_refs = refs[7:7 + 2 * n_seg]
    o_ref = refs[7 + 2 * n_seg]
    qs_ref = refs[8 + 2 * n_seg]
    qscale = ATT_HEAD_DIM ** -0.5 * LOG2E
    heads = range(ATT_REP)

    def prepare(src_ref, c_ref, s_ref):
        for r in heads:
            qh = _rms(src_ref[0, :, r * ATT_HEAD_DIM:(r + 1) * ATT_HEAD_DIM].astype(F32), nw_ref[...])
            if rope:
                qh = _rope(qh, c_ref[...], s_ref[...])
            qs_ref[r] = (qh * qscale).astype(BF16)

    @pl.when(pl.program_id(2) == 0)
    def _():
        prepare(q_ref, cos_ref, sin_ref)

    qs = [qs_ref[r] for r in heads]
    scores = [[_mm_nt(qs[r], kv_refs[2 * s][0]) for s in range(n_seg)] for r in heads]
    prepare(qn_ref, cosn_ref, sinn_ref)
    maxes = [functools.reduce(jnp.maximum, [jnp.max(sc, axis=-1, keepdims=True) for sc in scores[r]]) for r in heads]
    accs = [functools.reduce(jnp.add, [_mm(jnp.exp2(scores[r][s] - maxes[r]), kv_refs[2 * s + 1][0])
                                        for s in range(n_seg)]) for r in heads]
    for r in heads:
        o_ref[0, :, r * ATT_HEAD_DIM:(r + 1) * ATT_HEAD_DIM] = (
            accs[r][:, :ATT_HEAD_DIM] / accs[r][:, ATT_HEAD_DIM:ATT_HEAD_DIM + 1]).astype(o_ref.dtype)


def _attention(p, q_norm, cos, sin, kv_segs, rope, tq):
    bsz, seq, _ = p.shape
    gw = ATT_REP * ATT_HEAD_DIM
    qb = C_AQ // gw
    n_seg = len(kv_segs)
    nq = seq // tq
    nxt = lambda i: jnp.minimum(i + 1, nq - 1)
    tab = lambda f: pl.BlockSpec((tq, ATT_HEAD_DIM), lambda b, g, i: (f(i) if rope else 0, 0))
    cur = lambda i: i
    in_specs = [pl.BlockSpec((1, tq, gw), lambda b, g, i: (b, i, qb + g)),
                pl.BlockSpec((1, tq, gw), lambda b, g, i: (b, nxt(i), qb + g)),
                pl.BlockSpec((1, ATT_HEAD_DIM), lambda b, g, i: (0, 0)), tab(cur), tab(cur), tab(nxt), tab(nxt)]
    args = [p, p, q_norm.reshape(1, ATT_HEAD_DIM), cos, sin, cos, sin]
    for k_arr, v_arr in kv_segs:
        lk = k_arr.shape[1]
        in_specs += [pl.BlockSpec((1, lk, ATT_HEAD_DIM), lambda b, g, i: (b, 0, g)),
                     pl.BlockSpec((1, lk, 2 * ATT_HEAD_DIM), lambda b, g, i: (b, 0, g))]
        args += [k_arr, v_arr]
    return pl.pallas_call(
        functools.partial(_attn_kernel, n_seg=n_seg, rope=rope),
        grid=(bsz, ATT_KV_HEADS, nq),
        in_specs=in_specs,
        out_specs=pl.BlockSpec((1, tq, gw), lambda b, g, i: (b, i, g)),
        out_shape=jax.ShapeDtypeStruct((bsz, seq, ATT_Q_DIM), BF16),
        scratch_shapes=[pltpu.VMEM((ATT_REP, tq, ATT_HEAD_DIM), BF16)],
        compiler_params=_cparams(("arbitrary", "arbitrary", "arbitrary")),
        name="attention",
    )(*args)


def _merge_kernel(x_ref, mod_ref, of_ref, ob_ref, gz_ref, yf_ref, yb_ref, xs_ref, sz_ref, att_ref,
                  gg_ref, gs_ref, ga_ref, gnw_ref, snw_ref, dsk_ref, wg_ref, ws_ref, wa_ref, wo_ref, o_ref):
    o = of_ref[0].astype(F32) + ob_ref[0].astype(F32)
    gz = gz_ref[0].astype(F32)
    parts = []
    for h in range(GDN_HEADS):
        sl = slice(h * GDN_DV, (h + 1) * GDN_DV)
        parts.append(_rms(o[:, sl], gnw_ref[...]) * _silu(gz[:, sl]))
    y_gdn = jnp.concatenate(parts, axis=-1)

    y = yf_ref[0].astype(F32) + yb_ref[0].astype(F32) + dsk_ref[...] * xs_ref[0].astype(F32)
    y = y * _silu(sz_ref[0].astype(F32))
    snw = snw_ref[...]
    gw = SSD_D_INNER // SSD_GROUPS
    y_ssd = jnp.concatenate([_rms(y[:, g * gw:(g + 1) * gw], snw[:, g * gw:(g + 1) * gw])
                             for g in range(SSD_GROUPS)], axis=-1)

    gate = lambda ref: _sigmoid(ref[0].astype(F32))
    m = (gate(gg_ref) * _mm(y_gdn, wg_ref[...]) + gate(gs_ref) * _mm(y_ssd, ws_ref[...])
         + gate(ga_ref) * _mm(att_ref[0], wa_ref[...]))
    g1 = mod_ref[0][2:3]
    o_ref[0] = x_ref[0] + g1 * _mm(m, wo_ref[...])


def _merge(x, mod_l, mod_row, o_f, o_b, y_f, y_b, att, p, gdn_norm, ssd_norm, ssd_d, wg, ws, wa, wo, layer, tm):
    bsz, seq, d = x.shape
    row = lambda cb: pl.BlockSpec((1, tm, d), lambda b, i: (b, i, cb))
    vec = lambda n: pl.BlockSpec((1, n), lambda b, i: (0, 0))
    wsp = pl.BlockSpec((None, d, d), lambda b, i: (layer, 0, 0), pipeline_mode=pl.Buffered(1))
    return pl.pallas_call(
        _merge_kernel,
        grid=(bsz, seq // tm),
        in_specs=[row(0), pl.BlockSpec((1, 6, d), lambda b, i: (mod_row(b), 0, 0)),
                  row(0), row(0), row(C_GZ // d), row(0), row(0), row(C_XBC // d), row(C_SZ // d), row(0),
                  row(C_GATE // d), row(C_GATE // d + 1), row(C_GATE // d + 2),
                  vec(GDN_DV), vec(d), vec(d), wsp, wsp, wsp, wsp],
        out_specs=row(0),
        out_shape=jax.ShapeDtypeStruct((bsz, seq, d), F32),
        compiler_params=_cparams(("arbitrary", "arbitrary")),
        name="merge",
    )(x, mod_l, o_f, o_b, p, y_f, y_b, p, p, att, p, p, p,
      gdn_norm.reshape(1, GDN_DV), ssd_norm.reshape(1, d),
      jnp.repeat(ssd_d, SSD_HEAD_DIM).reshape(1, d), wg, ws, wa, wo)


def _mlp_kernel(x_ref, mod_ref, nw_ref, w1_ref, w2_ref, o_ref, h_ref, acc_ref):
    k = pl.program_id(2)
    m = mod_ref[0]

    @pl.when(k == 0)
    def _():
        h_ref[...] = _norm_mod(x_ref[0], nw_ref[...], m[4:5], m[3:4]).astype(BF16)
        acc_ref[...] = jnp.zeros_like(acc_ref)

    a = jnp.maximum(jnp.dot(h_ref[...], w1_ref[...], preferred_element_type=F32), 0.0)
    acc_ref[...] += _mm(a * a, w2_ref[...])

    @pl.when(k == pl.num_programs(2) - 1)
    def _():
        o_ref[0] = x_ref[0] + m[5:6] * acc_ref[...]


def _mlp(x, mod_l, mod_row, nw, w1, w2, layer, tm, tf):
    bsz, seq, d = x.shape
    return pl.pallas_call(
        _mlp_kernel,
        grid=(bsz, seq // tm, D_FF // tf),
        in_specs=[pl.BlockSpec((1, tm, d), lambda b, i, k: (b, i, 0)),
                  pl.BlockSpec((1, 6, d), lambda b, i, k: (mod_row(b), 0, 0)),
                  pl.BlockSpec((1, d), lambda b, i, k: (0, 0)),
                  pl.BlockSpec((None, d, tf), lambda b, i, k: (layer, 0, k)),
                  pl.BlockSpec((None, tf, d), lambda b, i, k: (layer, k, 0))],
        out_specs=pl.BlockSpec((1, tm, d), lambda b, i, k: (b, i, 0)),
        out_shape=jax.ShapeDtypeStruct((bsz, seq, d), F32),
        scratch_shapes=[pltpu.VMEM((tm, d), BF16), pltpu.VMEM((tm, d), F32)],
        compiler_params=_cparams(("arbitrary", "arbitrary", "arbitrary")),
        name="mlp",
    )(x, mod_l, nw.reshape(1, d), w1, w2)


def _reorder_w_in(w_in):
    depth, d, _ = w_in.shape
    o = 0
    seg = {}
    for name, size in (("qkv", GDN_QKV), ("gz", GDN_V_DIM), ("beta", 2 * GDN_HEADS), ("a", 2 * GDN_HEADS),
                       ("sz", SSD_D_INNER), ("xbc", SSD_XBC), ("dt", 2 * SSD_HEADS),
                       ("aq", ATT_Q_DIM), ("akv", 2 * ATT_KV_DIM), ("gate", 3 * D_MODEL)):
        seg[name] = w_in[:, :, o:o + size]
        o += size
    pad = jnp.zeros((depth, d, NP_COLS - (C_SMALL + SM_DT + 2 * SSD_HEADS)), w_in.dtype)
    out = jnp.concatenate([seg["qkv"], seg["xbc"], seg["akv"], seg["gz"], seg["sz"], seg["aq"], seg["gate"],
                           seg["beta"], seg["a"], seg["dt"], pad], axis=-1)
    return out.astype(BF16)


def _rope_tables(seq):
    t = jnp.arange(seq, dtype=jnp.int32)
    q = ATT_HEAD_DIM // 4
    freqs = ROPE_THETA ** (-jnp.arange(q, dtype=F32) / q)
    ang_r = (t // GRID_W).astype(F32)[:, None] * freqs[None, :]
    ang_c = (t % GRID_W).astype(F32)[:, None] * freqs[None, :]
    cos = jnp.concatenate([jnp.cos(ang_r)] * 2 + [jnp.cos(ang_c)] * 2, axis=-1)
    sin = jnp.concatenate([-jnp.sin(ang_r), jnp.sin(ang_r), -jnp.sin(ang_c), jnp.sin(ang_c)], axis=-1)
    return cos, sin


def _tile(seq, want):
    return min(seq, want)


def kernel(x, c, ctx, c_ctx, w_mod, b_mod, norm_mix, norm_mlp, w_in, gdn_conv, gdn_a_log, gdn_dt_bias, gdn_norm,
           ssd_conv_w, ssd_conv_b, ssd_a_log, ssd_dt_bias, ssd_d, ssd_norm, att_q_norm, att_k_norm,
           w_br_gdn, w_br_ssd, w_br_att, w_out, w_ff1, w_ff2):
    bsz, seq, d = x.shape
    ctx_len = ctx.shape[1]
    depth = w_in.shape[0]
    assert bsz < MOD_ROWS and d == D_MODEL

    cc = jnp.zeros((MOD_ROWS, d), F32).at[:bsz].set(c).at[bsz].set(c_ctx)
    mod = _modulation(cc, w_mod, b_mod).reshape(depth, MOD_ROWS, 6, d)
    lat_row = lambda b: b
    ctx_row = lambda b: bsz

    w_in_r = _reorder_w_in(w_in)
    wg, ws, wa, wo = (w.astype(BF16) for w in (w_br_gdn, w_br_ssd, w_br_att, w_out))
    w1, w2 = w_ff1.astype(BF16), w_ff2.astype(BF16)
    cos, sin = _rope_tables(seq)

    zg = jnp.zeros((bsz, GDN_PAIRS, GDN_DK, 2 * GDN_DV), F32)
    zs = jnp.zeros((bsz, SSD_HEADS // 2, SSD_STATE, 2 * SSD_HEAD_DIM), F32)
    conv_w = jnp.concatenate([gdn_conv, ssd_conv_w], axis=-1)
    conv_b = jnp.concatenate([jnp.zeros((depth, 1, GDN_QKV), F32), ssd_conv_b[:, None, :]], axis=-1)
    xc = ctx
    for l in range(depth):
        last = l == depth - 1
        p_lat, sm_lat = _in_projection(x, mod[l], lat_row, norm_mix[l], w_in_r, conv_w[l], conv_b[l], l,
                                       _tile(seq, 1024))
        p_ctx, sm_ctx = _in_projection(xc, mod[l], ctx_row, norm_mix[l], w_in_r, conv_w[l], conv_b[l], l,
                                       _tile(ctx_len, 1024))

        ogf_c, ogb_c, sgf, sgb = _gdn_scan(p_ctx, sm_ctx, gdn_a_log[l], gdn_dt_bias[l], zg, zg)
        ogf_l, ogb_l, _, _ = _gdn_scan(p_lat, sm_lat, gdn_a_log[l], gdn_dt_bias[l], sgf, sgb)
        ysf_c, ysb_c, ssf, ssb = _ssd_scan(p_ctx, sm_ctx, ssd_a_log[l], ssd_dt_bias[l], zs, zs)
        ysf_l, ysb_l, _, _ = _ssd_scan(p_lat, sm_lat, ssd_a_log[l], ssd_dt_bias[l], ssf, ssb)

        k_c, v_c = _kv_prep(p_ctx, att_k_norm[l], cos, sin, False, _tile(ctx_len, 256))
        k_l, v_l = _kv_prep(p_lat, att_k_norm[l], cos, sin, True, _tile(seq, 512))
        att_l = _attention(p_lat, att_q_norm[l], cos, sin, [(k_c, v_c), (k_l, v_l)], True, _tile(seq, 256))

        x = _merge(x, mod[l], lat_row, ogf_l, ogb_l, ysf_l, ysb_l, att_l, p_lat,
                   gdn_norm[l], ssd_norm[l], ssd_d[l], wg, ws, wa, wo, l, _tile(seq, 512))
        x = _mlp(x, mod[l], lat_row, norm_mlp[l], w1, w2, l, _tile(seq, 1024), 1024)

        if not last:
            att_c = _attention(p_ctx, att_q_norm[l], cos, sin, [(k_c, v_c)], False, _tile(ctx_len, 256))
            xc = _merge(xc, mod[l], ctx_row, ogf_c, ogb_c, ysf_c, ysb_c, att_c, p_ctx,
                        gdn_norm[l], ssd_norm[l], ssd_d[l], wg, ws, wa, wo, l, _tile(ctx_len, 256))
            xc = _mlp(xc, mod[l], ctx_row, norm_mlp[l], w1, w2, l, _tile(ctx_len, 512), 1024)
    return x
```

```python
import functools

import jax
import jax.numpy as jnp
from jax import lax
from jax.experimental import pallas as pl
from jax.experimental.pallas import tpu as pltpu

F32 = jnp.float32
BF16 = jnp.bfloat16

D_MODEL = 1024
GRID_W = 64
EPS = 1e-6

GDN_HEADS = 8
GDN_DK = 128
GDN_DV = 128
GDN_CHUNK = 64
GDN_QK_DIM = GDN_HEADS * GDN_DK
GDN_V_DIM = GDN_HEADS * GDN_DV
GDN_QKV = 2 * GDN_QK_DIM + GDN_V_DIM

SSD_D_INNER = D_MODEL
SSD_HEAD_DIM = 64
SSD_HEADS = SSD_D_INNER // SSD_HEAD_DIM
SSD_GROUPS = 2
SSD_HPG = SSD_HEADS // SSD_GROUPS
SSD_STATE = 128
SSD_CHUNK = 128
SSD_XBC = SSD_D_INNER + 2 * SSD_GROUPS * SSD_STATE

ATT_HEADS = 8
ATT_KV_HEADS = 2
ATT_REP = ATT_HEADS // ATT_KV_HEADS
ATT_HEAD_DIM = 128
ATT_Q_DIM = ATT_HEADS * ATT_HEAD_DIM
ATT_KV_DIM = ATT_KV_HEADS * ATT_HEAD_DIM
ROPE_THETA = 10000.0
LOG2E = 1.4426950408889634
D_FF = 4 * D_MODEL

LANES = 128
HALO = 16
MOD_ROWS = 16

C_QKV = 0
C_XBC = C_QKV + GDN_QKV
C_AKV = C_XBC + SSD_XBC
C_GZ = C_AKV + 2 * ATT_KV_DIM
C_SZ = C_GZ + GDN_V_DIM
C_AQ = C_SZ + SSD_D_INNER
C_GATE = C_AQ + ATT_Q_DIM
C_SMALL = C_GATE + 3 * D_MODEL
NP_COLS = 11520
N_TILES = 5
CONV_TILES = 2
CONV_COLS = GDN_QKV + SSD_XBC
CONV_SUB = 256
SM_BETA = 0
SM_A = 2 * GDN_HEADS
SM_DT = 4 * GDN_HEADS

VMEM_LIMIT = 56 * 1024 * 1024


def _cparams(sem):
    return pltpu.CompilerParams(dimension_semantics=sem, vmem_limit_bytes=VMEM_LIMIT)


def _mm(a, b):
    return jnp.dot(a.astype(BF16), b.astype(BF16), preferred_element_type=F32)


def _mm_nt(a, b):
    return lax.dot_general(a.astype(BF16), b.astype(BF16), (((1,), (1,)), ((), ())),
                           preferred_element_type=F32)


def _mm_tn(a, b):
    return lax.dot_general(a.astype(BF16), b.astype(BF16), (((0,), (0,)), ((), ())),
                           preferred_element_type=F32)


def _split3(x):
    hi = x.astype(BF16)
    r = x - hi.astype(F32)
    mid = r.astype(BF16)
    lo = (r - mid.astype(F32)).astype(BF16)
    return hi, mid, lo


def _mm_sel_l(sel, x):
    hi, mid, lo = _split3(x)
    d = lambda p: jnp.dot(sel, p, preferred_element_type=F32)
    return (d(hi) + d(mid)) + d(lo)


def _mm_sel_r(x, sel):
    hi, mid, lo = _split3(x)
    d = lambda p: jnp.dot(p, sel, preferred_element_type=F32)
    return (d(hi) + d(mid)) + d(lo)


def _sigmoid(x):
    return 1.0 / (1.0 + jnp.exp(-x))


def _silu(x):
    return x * _sigmoid(x)


def _softplus(x):
    return jnp.maximum(x, 0.0) + jnp.log(1.0 + jnp.exp(-jnp.abs(x)))


def _rms(x, w):
    return x * lax.rsqrt(jnp.mean(x * x, axis=-1, keepdims=True) + EPS) * w


def _norm_mod(x, nw, scale, shift):
    return _rms(x, nw) * (1.0 + scale) + shift


def _tri(n, lower):
    i = lax.broadcasted_iota(jnp.int32, (n, n), 0)
    j = lax.broadcasted_iota(jnp.int32, (n, n), 1)
    return (i >= j) if lower else (i <= j)


def _rope(x, cos, sin):
    lane = lax.broadcasted_iota(jnp.int32, x.shape, 1)
    q = LANES // 4
    swapped = jnp.where((lane & q) == 0, pltpu.roll(x, LANES - q, 1), pltpu.roll(x, q, 1))
    return x * cos + swapped * sin


def _mod_kernel(c_ref, w_ref, b_ref, o_ref):
    o_ref[0] = _mm(_silu(c_ref[...]), w_ref[0]) + b_ref[0]


def _modulation(cc, w_mod, b_mod):
    depth = w_mod.shape[0]
    n = w_mod.shape[2]
    tn = D_MODEL
    return pl.pallas_call(
        _mod_kernel,
        grid=(depth, n // tn),
        in_specs=[pl.BlockSpec((MOD_ROWS, D_MODEL), lambda l, j: (0, 0)),
                  pl.BlockSpec((1, D_MODEL, tn), lambda l, j: (l, 0, j)),
                  pl.BlockSpec((1, 1, tn), lambda l, j: (l, 0, j))],
        out_specs=pl.BlockSpec((1, MOD_ROWS, tn), lambda l, j: (l, 0, j)),
        out_shape=jax.ShapeDtypeStruct((depth, MOD_ROWS, n), F32),
        compiler_params=_cparams(("arbitrary", "arbitrary")),
        name="modulation",
    )(cc, w_mod, b_mod.reshape(depth, 1, n))


def _conv_tile(h_ref, w_ref, cw_ref, cb_ref, o_ref, tm, first_tile, last_tile, l2_blocks, with_bias):
    sub = min(tm, CONV_SUB)
    nsub = tm // sub
    tn = w_ref.shape[1]
    cw = cw_ref[...]
    rid = lax.broadcasted_iota(jnp.int32, (sub, tn), 0)
    for r in range(nsub):
        p = jnp.dot(h_ref[r * sub:r * sub + sub + 2 * HALO], w_ref[...], preferred_element_type=F32)
        n = sub + 2 * HALO
        xm1 = pltpu.roll(p, 1, 0)[HALO:HALO + sub]
        xp1 = pltpu.roll(p, n - 1, 0)[HALO:HALO + sub]
        if r == 0:
            xm1 = jnp.where((rid == 0) & first_tile, 0.0, xm1)
        if r == nsub - 1:
            xp1 = jnp.where((rid == sub - 1) & last_tile, 0.0, xp1)
        y = cw[0:1] * xm1 + cw[1:2] * p[HALO:HALO + sub] + cw[2:3] * xp1
        y = _silu(y + cb_ref[...] if with_bias else y)
        if l2_blocks:
            parts = []
            for blk in range(tn // LANES):
                yb = y[:, blk * LANES:(blk + 1) * LANES]
                if blk < l2_blocks:
                    inv = lax.rsqrt(jnp.sum(yb * yb, axis=-1, keepdims=True) + EPS)
                    yb = yb * (inv * GDN_DK ** -0.5 if blk < GDN_HEADS else inv)
                parts.append(yb)
            y = jnp.concatenate(parts, axis=-1)
        o_ref[0, r * sub:(r + 1) * sub, :] = y.astype(o_ref.dtype)


def _inproj_kernel(x_ref, xp_ref, xn_ref, mod_ref, nw_ref, w_ref, cw_ref, cb_ref, o_ref, sm_ref, h_ref):
    i = pl.program_id(1)
    j = pl.program_id(2)
    tm = x_ref.shape[1]

    @pl.when(j == 0)
    def _():
        m = mod_ref[0]
        norm = lambda x: _norm_mod(x, nw_ref[...], m[1:2], m[0:1]).astype(BF16)
        h_ref[0:HALO] = norm(xp_ref[0])
        h_ref[HALO:HALO + tm] = norm(x_ref[0])
        h_ref[HALO + tm:2 * HALO + tm] = norm(xn_ref[0])

    first_tile = i == 0
    last_tile = i == pl.num_programs(1) - 1

    @pl.when(j == 0)
    def _():
        _conv_tile(h_ref, w_ref, cw_ref, cb_ref, o_ref, tm, first_tile, last_tile, 2 * GDN_HEADS, False)

    @pl.when((j > 0) & (j < CONV_TILES))
    def _():
        _conv_tile(h_ref, w_ref, cw_ref, cb_ref, o_ref, tm, first_tile, last_tile, 0, True)

    @pl.when(j >= CONV_TILES)
    def _():
        p = jnp.dot(h_ref[HALO:HALO + tm], w_ref[...], preferred_element_type=F32)
        o_ref[0] = p.astype(o_ref.dtype)

        @pl.when(j == pl.num_programs(2) - 1)
        def _():
            off = C_SMALL - (NP_COLS // N_TILES) * (N_TILES - 1)
            sm_ref[0] = p[:, off:off + LANES]


def _in_projection(x, mod_l, mod_row, nw, w_r, conv_w, conv_b, layer, tm):
    bsz, seq, d = x.shape
    tn = NP_COLS // N_TILES
    assert CONV_COLS == CONV_TILES * tn and 2 * GDN_QK_DIM <= tn <= GDN_QKV
    nrb = seq // HALO
    rpb = tm // HALO
    ct = lambda j: jnp.minimum(j, CONV_TILES - 1)
    return pl.pallas_call(
        _inproj_kernel,
        grid=(bsz, seq // tm, N_TILES),
        in_specs=[pl.BlockSpec((1, tm, d), lambda b, i, j: (b, i, 0)),
                  pl.BlockSpec((1, HALO, d), lambda b, i, j: (b, jnp.maximum(i * rpb - 1, 0), 0)),
                  pl.BlockSpec((1, HALO, d), lambda b, i, j: (b, jnp.minimum((i + 1) * rpb, nrb - 1), 0)),
                  pl.BlockSpec((1, 6, d), lambda b, i, j: (mod_row(b), 0, 0)),
                  pl.BlockSpec((1, d), lambda b, i, j: (0, 0)),
                  pl.BlockSpec((None, d, tn), lambda b, i, j: (layer, 0, j)),
                  pl.BlockSpec((3, tn), lambda b, i, j: (0, ct(j))),
                  pl.BlockSpec((1, tn), lambda b, i, j: (0, ct(j)))],
        out_specs=[pl.BlockSpec((1, tm, tn), lambda b, i, j: (b, i, j)),
                   pl.BlockSpec((1, tm, LANES), lambda b, i, j: (b, i, 0))],
        out_shape=[jax.ShapeDtypeStruct((bsz, seq, NP_COLS), BF16),
                   jax.ShapeDtypeStruct((bsz, seq, LANES), F32)],
        scratch_shapes=[pltpu.VMEM((tm + 2 * HALO, d), BF16)],
        compiler_params=_cparams(("arbitrary", "arbitrary", "arbitrary")),
        name="in_projection",
    )(x, x, x, mod_l, nw.reshape(1, d), w_r, conv_w, conv_b)


GDN_PACK = 2
GDN_STEP_CHUNKS = 4
GDN_PAIRS = GDN_HEADS // 2


def _bd4(x):
    x = x.astype(BF16)
    n = x.shape[0]
    tiles = x.shape[1] // LANES
    lo = lax.broadcasted_iota(jnp.int32, (n, LANES), 1) < LANES // 2
    z = jnp.zeros((n, LANES), BF16)
    rows = []
    for t in range(tiles):
        a = x[:, t * LANES:(t + 1) * LANES]
        for half in (jnp.where(lo, a, z), jnp.where(lo, z, a)):
            rows.append(jnp.concatenate([half if tt == t else z for tt in range(tiles)], 1) if tiles > 1 else half)
    return jnp.concatenate(rows, 0)


def _bd_blocks(blocks):
    z = jnp.zeros_like(blocks[0])
    n = len(blocks)
    return jnp.concatenate([jnp.concatenate([blk if j == i else z for j in range(n)], 1)
                            for i, blk in enumerate(blocks)], 0)


def _pack_cols(cols):
    n = cols[0].shape[0]
    lo = lax.broadcasted_iota(jnp.int32, (n, LANES), 1) < LANES // 2
    bc = [jnp.broadcast_to(col, (n, LANES)) for col in cols]
    tiles = [jnp.where(lo, bc[2 * t], bc[2 * t + 1]) for t in range(len(cols) // 2)]
    return jnp.concatenate(tiles, 1) if len(tiles) > 1 else tiles[0]


def _chunk_tri(n, c, lower):
    i = lax.broadcasted_iota(jnp.int32, (n, n), 0)
    j = lax.broadcasted_iota(jnp.int32, (n, n), 1)
    same = (i // c) == (j // c)
    return (same & ((i >= j) if lower else (i <= j))).astype(BF16)


def _gdn_gates(s_ref, alog_l, dtb_l, alog_s, dtb_s, fwd):
    raw = s_ref[0]
    n = raw.shape[0]
    lbeta = -_softplus(-raw)
    la = -jnp.exp(alog_l) * _softplus(raw + dtb_l)
    rraw = raw.T
    beta_r = _sigmoid(rraw)
    la_r = -jnp.exp(alog_s) * _softplus(rraw + dtb_s)
    low = _chunk_tri(n, GDN_CHUNK, True)
    up = _chunk_tri(n, GDN_CHUNK, False)
    if fwd:
        return lbeta, _mm_sel_l(low, la), beta_r, _mm_sel_r(la_r, up)
    return lbeta, _mm_sel_l(up, la), beta_r, _mm_sel_r(la_r, low)


def _gdn_pre(groups):
    c = GDN_CHUNK
    w = GDN_PACK * c
    ii = lax.broadcasted_iota(jnp.int32, (c, w), 0)
    jj = lax.broadcasted_iota(jnp.int32, (c, w), 1) & (c - 1)
    masks = {True: (ii >= jj, ii > jj), False: (ii <= jj, ii < jj)}
    for g in groups:
        g["gc_p"] = _pack_cols(g["gcs"])
        g["gcb_p"] = _pack_cols([gc + lb for gc, lb in zip(g["gcs"], g["lbs"])])
        g["gr_p"] = jnp.concatenate(g["grs"], 1)
        g["beta_rp"] = jnp.concatenate(g["brs"], 1)
        g["kbd"] = _bd_blocks([g["k4"][:, i * GDN_DK:(i + 1) * GDN_DK] for i in range(GDN_PACK)])
    prods = [_mm_nt(jnp.concatenate([g["k4"], g["q4"]], 0), g["kbd"]) for g in groups]
    for g, kq in zip(groups, prods):
        incl, strict = masks[g["fwd"]]
        g["a_p"] = kq[:c] * jnp.exp(jnp.where(strict, g["gcb_p"] - g["gr_p"], -jnp.inf))
        g["qk_p"] = kq[c:] * jnp.exp(jnp.where(incl, g["gc_p"] - g["gr_p"], -jnp.inf))
    for level in range(c.bit_length() - 1):
        bi, bj = ii >> level, jj >> level
        pair = {True: ((bi & 1) == 1) & (bj == bi - 1), False: ((bi & 1) == 0) & (bj == bi + 1)}
        offs = [jnp.where(pair[g["fwd"]], g["a_p"], 0.0) for g in groups]
        if level == 0:
            eye = (ii == jj).astype(F32)
            for g, off in zip(groups, offs):
                g["inv"] = eye - off
        else:
            tmps = [jnp.dot(g["inv"].astype(BF16), _bd4(off), preferred_element_type=F32)
                    for g, off in zip(groups, offs)]
            for g, tmp in zip(groups, tmps):
                g["inv"] = g["inv"] - jnp.dot(tmp.astype(BF16), _bd4(g["inv"]), preferred_element_type=F32)
    for g in groups:
        vbd = _bd_blocks([g["v4"][:, i * GDN_DV:(i + 1) * GDN_DV] for i in range(GDN_PACK)])
        g["u4"] = _mm(g["inv"] * g["beta_rp"], vbd)
        g["w4"] = _mm(g["inv"] * (g["beta_rp"] * jnp.exp(g["gr_p"])), g["kbd"])


def _gdn_pair_steps(items, states):
    c = GDN_CHUNK
    rs = []
    for it, s2 in zip(items, states):
        sbd = _bd_blocks([s2[:, :GDN_DV].astype(BF16), s2[:, GDN_DV:].astype(BF16)])
        egs = [jnp.exp(g) for g in it["gcs"]]
        qd2 = jnp.concatenate([it["q2"][:, i * GDN_DK:(i + 1) * GDN_DK] * egs[i] for i in range(2)], 1)
        rs.append(_mm(jnp.concatenate([it["w2"], qd2], 0), sbd))
    outs, news = [], []
    for it, s2, r in zip(items, states, rs):
        g_lasts = [g[c - 1:c] if it["fwd"] else g[0:1] for g in it["gcs"]]
        vn2 = it["u2"] - r[:c]
        vnbd = _bd_blocks([vn2[:, :GDN_DV].astype(BF16), vn2[:, GDN_DV:].astype(BF16)])
        outs.append(r[c:] + _mm(it["qk2"], vnbd))
        kd = jnp.concatenate([it["k2"][:, i * GDN_DK:(i + 1) * GDN_DK] * jnp.exp(g_lasts[i] - it["gcs"][i])
                              for i in range(2)], 0)
        cd2 = jnp.concatenate([jnp.broadcast_to(jnp.exp(gl), (1, GDN_DV)) for gl in g_lasts], 1)
        news.append(s2 * cd2 + _mm_tn(kd, vnbd))
    return outs, news


def _gdn_kernel(qf_ref, kf_ref, vf_ref, sf_ref, qb_ref, kb_ref, vb_ref, sb_ref,
                alog_l_ref, dtb_l_ref, alog_s_ref, dtb_s_ref, s0f_ref, s0b_ref,
                of_ref, ob_ref, stf_ref, stb_ref):
    @pl.when(pl.program_id(1) == 0)
    def _():
        stf_ref[...] = s0f_ref[...]
        stb_ref[...] = s0b_ref[...]

    c = GDN_CHUNK
    nck = qf_ref.shape[1] // c
    alog_l, dtb_l, alog_s, dtb_s = alog_l_ref[...], dtb_l_ref[...], alog_s_ref[...], dtb_s_ref[...]
    groups = []
    for d, (q_ref, k_ref, v_ref, s_ref, o_ref, st_ref) in enumerate(
            ((qf_ref, kf_ref, vf_ref, sf_ref, of_ref, stf_ref),
             (qb_ref, kb_ref, vb_ref, sb_ref, ob_ref, stb_ref))):
        fwd = d == 0
        lbeta_all, gc_all, beta_r_all, gr_all = _gdn_gates(s_ref, alog_l, dtb_l, alog_s, dtb_s, fwd)
        for t in range(nck):
            ci = t if fwd else nck - 1 - t
            rows = slice(ci * c, (ci + 1) * c)
            for half in range(GDN_HEADS // GDN_PACK):
                h0 = half * GDN_PACK
                lanes_b = [SM_BETA + d * GDN_HEADS + h0 + i for i in range(GDN_PACK)]
                lanes_a = [SM_A + d * GDN_HEADS + h0 + i for i in range(GDN_PACK)]
                sl4 = slice(h0 * GDN_DK, (h0 + GDN_PACK) * GDN_DK)
                groups.append(dict(
                    fwd=fwd, t=t, rows=rows, h0=h0, o_ref=o_ref, st_ref=st_ref,
                    q4=q_ref[0, rows, sl4], k4=k_ref[0, rows, sl4], v4=v_ref[0, rows, sl4],
                    gcs=[gc_all[rows, la:la + 1] for la in lanes_a],
                    lbs=[lbeta_all[rows, lb:lb + 1] for lb in lanes_b],
                    grs=[gr_all[la:la + 1, rows] for la in lanes_a],
                    brs=[beta_r_all[lb:lb + 1, rows] for lb in lanes_b]))
    _gdn_pre(groups)

    keys = [(d, pair) for d in range(2) for pair in range(GDN_PAIRS)]
    st_refs = (stf_ref, stb_ref)
    states = [st_refs[d][0, pair] for d, pair in keys]
    for t in range(nck):
        items = []
        for d, pair in keys:
            ppg = GDN_PACK // 2
            g = next(g for g in groups if g["fwd"] == (d == 0) and g["t"] == t and g["h0"] == (pair // ppg) * GDN_PACK)
            pr = pair % ppg
            s2 = slice(pr * 2 * GDN_DK, (pr + 1) * 2 * GDN_DK)
            items.append(dict(fwd=g["fwd"], q2=g["q4"][:, s2], k2=g["k4"][:, s2], u2=g["u4"][:, s2], w2=g["w4"][:, s2],
                              qk2=g["qk_p"][:, pr * 2 * c:(pr + 1) * 2 * c], gcs=g["gcs"][2 * pr:2 * pr + 2],
                              o_ref=g["o_ref"], rows=g["rows"], pair=pair))
        outs, states = _gdn_pair_steps(items, states)
        for it, o2 in zip(items, outs):
            it["o_ref"][0, it["rows"], it["pair"] * 2 * GDN_DV:(it["pair"] + 1) * 2 * GDN_DV] = o2.astype(BF16)
    for (d, pair), s2 in zip(keys, states):
        st_refs[d][0, pair] = s2


def _lane_param(vals, offset):
    flat = vals.reshape(-1).astype(F32)
    v = jnp.zeros((LANES,), F32).at[offset:offset + flat.shape[0]].set(flat)
    return v.reshape(1, LANES), v.reshape(LANES, 1)


def _gdn_scan(qkv, small, a_log, dt_bias, s0f, s0b):
    bsz, seq, _ = qkv.shape
    blk = min(seq, GDN_STEP_CHUNKS * GDN_CHUNK)
    nb = seq // blk
    alog_l, alog_s = _lane_param(a_log, SM_A)
    dtb_l, dtb_s = _lane_param(dt_bias, SM_A)
    hw = GDN_QK_DIM

    def block_specs(bmap):
        return [pl.BlockSpec((1, blk, hw), lambda b, i: (b, bmap(i), 0)),
                pl.BlockSpec((1, blk, hw), lambda b, i: (b, bmap(i), 1)),
                pl.BlockSpec((1, blk, hw), lambda b, i: (b, bmap(i), 2)),
                pl.BlockSpec((1, blk, LANES), lambda b, i: (b, bmap(i), 0))]

    fw = lambda i: i
    bw = lambda i: nb - 1 - i
    vec_l = pl.BlockSpec((1, LANES), lambda b, i: (0, 0))
    vec_s = pl.BlockSpec((LANES, 1), lambda b, i: (0, 0))
    st_spec = pl.BlockSpec((1, GDN_PAIRS, GDN_DK, 2 * GDN_DV), lambda b, i: (b, 0, 0, 0))
    out_shape = [jax.ShapeDtypeStruct((bsz, seq, GDN_V_DIM), BF16)] * 2 + \
                [jax.ShapeDtypeStruct((bsz, GDN_PAIRS, GDN_DK, 2 * GDN_DV), F32)] * 2
    return pl.pallas_call(
        _gdn_kernel,
        grid=(bsz, nb),
        in_specs=block_specs(fw) + block_specs(bw) + [vec_l, vec_l, vec_s, vec_s, st_spec, st_spec],
        out_specs=[pl.BlockSpec((1, blk, GDN_V_DIM), lambda b, i: (b, i, 0)),
                   pl.BlockSpec((1, blk, GDN_V_DIM), lambda b, i: (b, nb - 1 - i, 0)),
                   st_spec, st_spec],
        out_shape=out_shape,
        compiler_params=_cparams(("arbitrary", "arbitrary")),
        name="gdn_scan",
    )(qkv, qkv, qkv, small, qkv, qkv, qkv, small, alog_l, dtb_l, alog_s, dtb_s, s0f, s0b)


def _ssd_gates(s_ref, alog_l, dtb_l, alog_s, dtb_s, fwd):
    c = SSD_CHUNK
    raw = s_ref[0]
    da = _softplus(raw + dtb_l) * (-jnp.exp(alog_l))
    dt_r = _softplus(raw.T + dtb_s)
    da_r = dt_r * (-jnp.exp(alog_s))
    low = _tri(c, True).astype(BF16)
    up = _tri(c, False).astype(BF16)
    if fwd:
        return _mm_sel_l(low, da), dt_r, _mm_sel_r(da_r, up)
    return _mm_sel_l(up, da), dt_r, _mm_sel_r(da_r, low)


def _ssd_kernel(xf_ref, bf_ref, cf_ref, sf_ref, xb_ref, bb_ref, cb_ref, sb_ref,
                alog_l_ref, dtb_l_ref, alog_s_ref, dtb_s_ref, s0f_ref, s0b_ref,
                yf_ref, yb_ref, stf_ref, stb_ref):
    @pl.when(pl.program_id(1) == 0)
    def _():
        stf_ref[...] = s0f_ref[...]
        stb_ref[...] = s0b_ref[...]

    c = SSD_CHUNK
    hd = SSD_HEAD_DIM
    alog_l, dtb_l, alog_s, dtb_s = alog_l_ref[...], dtb_l_ref[...], alog_s_ref[...], dtb_s_ref[...]
    lo = lax.broadcasted_iota(jnp.int32, (c, 2 * hd), 1) < hd
    chains = []
    for d, (x_ref, b_ref, c_ref, s_ref, y_ref, st_ref) in enumerate(
            ((xf_ref, bf_ref, cf_ref, sf_ref, yf_ref, stf_ref),
             (xb_ref, bb_ref, cb_ref, sb_ref, yb_ref, stb_ref))):
        fwd = d == 0
        incl = _tri(c, fwd)
        ac_all, dtr_all, ar_all = _ssd_gates(s_ref, alog_l, dtb_l, alog_s, dtb_s, fwd)
        for g in range(SSD_GROUPS):
            bm = b_ref[0, :, g * SSD_STATE:(g + 1) * SSD_STATE].astype(F32)
            cm = c_ref[0, :, g * SSD_STATE:(g + 1) * SSD_STATE].astype(F32)
            cbm = _mm_nt(cm, bm)
            bm_t = bm.T
            for e in range(0, SSD_HPG, 2):
                pair = (g * SSD_HPG + e) // 2
                lns = [SM_DT + d * SSD_HEADS + g * SSD_HPG + e + i for i in range(2)]
                ars = [ar_all[ln:ln + 1, :] for ln in lns]
                chains.append(dict(
                    incl=incl, cm=cm, cbm=cbm, bm_t=bm_t, ars=ars, dtrs=[dtr_all[ln:ln + 1, :] for ln in lns],
                    acols=[jnp.broadcast_to(ac_all[:, ln:ln + 1], (c, c)) for ln in lns],
                    a_lasts=[ar[:, c - 1:c] if fwd else ar[:, 0:1] for ar in ars],
                    x2=x_ref[0, :, pair * 2 * hd:(pair + 1) * 2 * hd], state=st_ref[0, pair],
                    y_ref=y_ref, st_ref=st_ref, pair=pair))
    ys, sts = [], []
    for ch in chains:
        x2 = ch["x2"]
        zx = jnp.zeros_like(x2)
        xs = [jnp.where(lo, x2, zx), jnp.where(lo, zx, x2)]
        s2 = ch["state"].astype(BF16)
        zs = jnp.zeros_like(s2)
        ss = [jnp.where(lo, s2, zs), jnp.where(lo, zs, s2)]
        lhs, rhs, upd = [], [], []
        for i in range(2):
            lmat = jnp.exp(jnp.where(ch["incl"], ch["acols"][i] - ch["ars"][i], -jnp.inf))
            lhs += [(ch["cbm"] * lmat * ch["dtrs"][i]).astype(BF16), (ch["cm"] * jnp.exp(ch["acols"][i])).astype(BF16)]
            rhs += [xs[i], ss[i]]
            upd.append((ch["bm_t"] * (ch["dtrs"][i] * jnp.exp(ch["a_lasts"][i] - ch["ars"][i]))).astype(BF16))
        ys.append(jnp.dot(jnp.concatenate(lhs, 1), jnp.concatenate(rhs, 0), preferred_element_type=F32))
        decay2 = jnp.concatenate([jnp.broadcast_to(jnp.exp(al), (1, hd)) for al in ch["a_lasts"]], 1)
        sts.append(ch["state"] * decay2
                   + jnp.dot(jnp.concatenate(upd, 1), jnp.concatenate(xs, 0), preferred_element_type=F32))
    for ch, y, st in zip(chains, ys, sts):
        ch["y_ref"][0, :, ch["pair"] * 2 * hd:(ch["pair"] + 1) * 2 * hd] = y.astype(BF16)
        ch["st_ref"][0, ch["pair"]] = st


def _ssd_scan(xbc, small, a_log, dt_bias, s0f, s0b):
    bsz, seq, _ = xbc.shape
    c = SSD_CHUNK
    nc = seq // c
    alog_l, alog_s = _lane_param(a_log, SM_DT)
    dtb_l, dtb_s = _lane_param(dt_bias, SM_DT)
    gn = SSD_GROUPS * SSD_STATE

    def chunk_specs(cmap):
        return [pl.BlockSpec((1, c, SSD_D_INNER), lambda b, i: (b, cmap(i), C_XBC // SSD_D_INNER)),
                pl.BlockSpec((1, c, gn), lambda b, i: (b, cmap(i), (C_XBC + SSD_D_INNER) // gn)),
                pl.BlockSpec((1, c, gn), lambda b, i: (b, cmap(i), (C_XBC + SSD_D_INNER) // gn + 1)),
                pl.BlockSpec((1, c, LANES), lambda b, i: (b, cmap(i), 0))]

    fw = lambda i: i
    bw = lambda i: nc - 1 - i
    vec_l = pl.BlockSpec((1, LANES), lambda b, i: (0, 0))
    vec_s = pl.BlockSpec((LANES, 1), lambda b, i: (0, 0))
    st_spec = pl.BlockSpec((1, SSD_HEADS // 2, SSD_STATE, 2 * SSD_HEAD_DIM), lambda b, i: (b, 0, 0, 0))
    out_shape = [jax.ShapeDtypeStruct((bsz, seq, SSD_D_INNER), BF16)] * 2 + \
                [jax.ShapeDtypeStruct((bsz, SSD_HEADS // 2, SSD_STATE, 2 * SSD_HEAD_DIM), F32)] * 2
    return pl.pallas_call(
        _ssd_kernel,
        grid=(bsz, nc),
        in_specs=chunk_specs(fw) + chunk_specs(bw) + [vec_l, vec_l, vec_s, vec_s, st_spec, st_spec],
        out_specs=[pl.BlockSpec((1, c, SSD_D_INNER), lambda b, i: (b, i, 0)),
                   pl.BlockSpec((1, c, SSD_D_INNER), lambda b, i: (b, nc - 1 - i, 0)),
                   st_spec, st_spec],
        out_shape=out_shape,
        compiler_params=_cparams(("arbitrary", "arbitrary")),
        name="ssd_scan",
    )(xbc, xbc, xbc, small, xbc, xbc, xbc, small, alog_l, dtb_l, alog_s, dtb_s, s0f, s0b)


def _kv_prep_kernel(k_ref, v_ref, nw_ref, cos_ref, sin_ref, ko_ref, vo_ref, *, rope):
    parts = []
    for h in range(ATT_KV_HEADS):
        kh = _rms(k_ref[0, :, h * ATT_HEAD_DIM:(h + 1) * ATT_HEAD_DIM].astype(F32), nw_ref[...])
        if rope:
            kh = _rope(kh, cos_ref[...], sin_ref[...])
        parts.append(kh)
    ko_ref[0] = jnp.concatenate(parts, axis=-1).astype(BF16)
    v = v_ref[0]
    ones_blk = (lax.broadcasted_iota(jnp.int32, (v.shape[0], ATT_HEAD_DIM), 1) == 0).astype(BF16)
    vparts = []
    for h in range(ATT_KV_HEADS):
        vparts += [v[:, h * ATT_HEAD_DIM:(h + 1) * ATT_HEAD_DIM].astype(BF16), ones_blk]
    vo_ref[0] = jnp.concatenate(vparts, axis=-1)


def _kv_prep(p, k_norm, cos, sin, rope, ts):
    bsz, seq, _ = p.shape
    kb = C_AKV // ATT_KV_DIM
    tab = pl.BlockSpec((ts, ATT_HEAD_DIM), lambda b, i: (i if rope else 0, 0))
    return pl.pallas_call(
        functools.partial(_kv_prep_kernel, rope=rope),
        grid=(bsz, seq // ts),
        in_specs=[pl.BlockSpec((1, ts, ATT_KV_DIM), lambda b, i: (b, i, kb)),
                  pl.BlockSpec((1, ts, ATT_KV_DIM), lambda b, i: (b, i, kb + 1)),
                  pl.BlockSpec((1, ATT_HEAD_DIM), lambda b, i: (0, 0)), tab, tab],
        out_specs=[pl.BlockSpec((1, ts, ATT_KV_DIM), lambda b, i: (b, i, 0)),
                   pl.BlockSpec((1, ts, 2 * ATT_KV_DIM), lambda b, i: (b, i, 0))],
        out_shape=[jax.ShapeDtypeStruct((bsz, seq, ATT_KV_DIM), BF16),
                   jax.ShapeDtypeStruct((bsz, seq, 2 * ATT_KV_DIM), BF16)],
        compiler_params=_cparams(("arbitrary", "arbitrary")),
        name="kv_prep",
    )(p, p, k_norm.reshape(1, ATT_HEAD_DIM), cos, sin)


def _attn_kernel(*refs, n_seg, rope):
    q_ref, qn_ref, nw_ref, cos_ref, sin_ref, cosn_ref, sinn_ref = refs[:7]
    kv_refs = refs[7:7 + 2 * n_seg]
    o_ref = refs[7 + 2 * n_seg]
    qs_ref = refs[8 + 2 * n_seg]
    qscale = ATT_HEAD_DIM ** -0.5 * LOG2E
    heads = range(ATT_REP)

    def prepare(src_ref, c_ref, s_ref):
        for r in heads:
            qh = _rms(src_ref[0, :, r * ATT_HEAD_DIM:(r + 1) * ATT_HEAD_DIM].astype(F32), nw_ref[...])
            if rope:
                qh = _rope(qh, c_ref[...], s_ref[...])
            qs_ref[r] = (qh * qscale).astype(BF16)

    @pl.when(pl.program_id(2) == 0)
    def _():
        prepare(q_ref, cos_ref, sin_ref)

    qs = [qs_ref[r] for r in heads]
    scores = [[_mm_nt(qs[r], kv_refs[2 * s][0]) for s in range(n_seg)] for r in heads]
    prepare(qn_ref, cosn_ref, sinn_ref)
    maxes = [functools.reduce(jnp.maximum, [jnp.max(sc, axis=-1, keepdims=True) for sc in scores[r]]) for r in heads]
    accs = [functools.reduce(jnp.add, [_mm(jnp.exp2(scores[r][s] - maxes[r]), kv_refs[2 * s + 1][0])
                                        for s in range(n_seg)]) for r in heads]
    for r in heads:
        o_ref[0, :, r * ATT_HEAD_DIM:(r + 1) * ATT_HEAD_DIM] = (
            accs[r][:, :ATT_HEAD_DIM] / accs[r][:, ATT_HEAD_DIM:ATT_HEAD_DIM + 1]).astype(o_ref.dtype)


def _attention(p, q_norm, cos, sin, kv_segs, rope, tq):
    bsz, seq, _ = p.shape
    gw = ATT_REP * ATT_HEAD_DIM
    qb = C_AQ // gw
    n_seg = len(kv_segs)
    nq = seq // tq
    nxt = lambda i: jnp.minimum(i + 1, nq - 1)
    tab = lambda f: pl.BlockSpec((tq, ATT_HEAD_DIM), lambda b, g, i: (f(i) if rope else 0, 0))
    cur = lambda i: i
    in_specs = [pl.BlockSpec((1, tq, gw), lambda b, g, i: (b, i, qb + g)),
                pl.BlockSpec((1, tq, gw), lambda b, g, i: (b, nxt(i), qb + g)),
                pl.BlockSpec((1, ATT_HEAD_DIM), lambda b, g, i: (0, 0)), tab(cur), tab(cur), tab(nxt), tab(nxt)]
    args = [p, p, q_norm.reshape(1, ATT_HEAD_DIM), cos, sin, cos, sin]
    for k_arr, v_arr in kv_segs:
        lk = k_arr.shape[1]
        in_specs += [pl.BlockSpec((1, lk, ATT_HEAD_DIM), lambda b, g, i: (b, 0, g)),
                     pl.BlockSpec((1, lk, 2 * ATT_HEAD_DIM), lambda b, g, i: (b, 0, g))]
        args += [k_arr, v_arr]
    return pl.pallas_call(
        functools.partial(_attn_kernel, n_seg=n_seg, rope=rope),
        grid=(bsz, ATT_KV_HEADS, nq),
        in_specs=in_specs,
        out_specs=pl.BlockSpec((1, tq, gw), lambda b, g, i: (b, i, g)),
        out_shape=jax.ShapeDtypeStruct((bsz, seq, ATT_Q_DIM), BF16),
        scratch_shapes=[pltpu.VMEM((ATT_REP, tq, ATT_HEAD_DIM), BF16)],
        compiler_params=_cparams(("arbitrary", "arbitrary", "arbitrary")),
        name="attention",
    )(*args)


def _merge_kernel(x_ref, mod_ref, of_ref, ob_ref, gz_ref, yf_ref, yb_ref, xs_ref, sz_ref, att_ref,
                  gg_ref, gs_ref, ga_ref, gnw_ref, snw_ref, dsk_ref, wg_ref, ws_ref, wa_ref, wo_ref, o_ref):
    o = of_ref[0].astype(F32) + ob_ref[0].astype(F32)
    gz = gz_ref[0].astype(F32)
    parts = []
    for h in range(GDN_HEADS):
        sl = slice(h * GDN_DV, (h + 1) * GDN_DV)
        parts.append(_rms(o[:, sl], gnw_ref[...]) * _silu(gz[:, sl]))
    y_gdn = jnp.concatenate(parts, axis=-1)

    y = yf_ref[0].astype(F32) + yb_ref[0].astype(F32) + dsk_ref[...] * xs_ref[0].astype(F32)
    y = y * _silu(sz_ref[0].astype(F32))
    snw = snw_ref[...]
    gw = SSD_D_INNER // SSD_GROUPS
    y_ssd = jnp.concatenate([_rms(y[:, g * gw:(g + 1) * gw], snw[:, g * gw:(g + 1) * gw])
                             for g in range(SSD_GROUPS)], axis=-1)

    gate = lambda ref: _sigmoid(ref[0].astype(F32))
    m = (gate(gg_ref) * _mm(y_gdn, wg_ref[...]) + gate(gs_ref) * _mm(y_ssd, ws_ref[...])
         + gate(ga_ref) * _mm(att_ref[0], wa_ref[...]))
    g1 = mod_ref[0][2:3]
    o_ref[0] = x_ref[0] + g1 * _mm(m, wo_ref[...])


def _merge(x, mod_l, mod_row, o_f, o_b, y_f, y_b, att, p, gdn_norm, ssd_norm, ssd_d, wg, ws, wa, wo, layer, tm):
    bsz, seq, d = x.shape
    row = lambda cb: pl.BlockSpec((1, tm, d), lambda b, i: (b, i, cb))
    vec = lambda n: pl.BlockSpec((1, n), lambda b, i: (0, 0))
    wsp = pl.BlockSpec((None, d, d), lambda b, i: (layer, 0, 0), pipeline_mode=pl.Buffered(1))
    return pl.pallas_call(
        _merge_kernel,
        grid=(bsz, seq // tm),
        in_specs=[row(0), pl.BlockSpec((1, 6, d), lambda b, i: (mod_row(b), 0, 0)),
                  row(0), row(0), row(C_GZ // d), row(0), row(0), row(C_XBC // d), row(C_SZ // d), row(0),
                  row(C_GATE // d), row(C_GATE // d + 1), row(C_GATE // d + 2),
                  vec(GDN_DV), vec(d), vec(d), wsp, wsp, wsp, wsp],
        out_specs=row(0),
        out_shape=jax.ShapeDtypeStruct((bsz, seq, d), F32),
        compiler_params=_cparams(("arbitrary", "arbitrary")),
        name="merge",
    )(x, mod_l, o_f, o_b, p, y_f, y_b, p, p, att, p, p, p,
      gdn_norm.reshape(1, GDN_DV), ssd_norm.reshape(1, d),
      jnp.repeat(ssd_d, SSD_HEAD_DIM).reshape(1, d), wg, ws, wa, wo)


def _mlp_kernel(x_ref, mod_ref, nw_ref, w1_ref, w2_ref, o_ref, h_ref, acc_ref):
    k = pl.program_id(2)
    m = mod_ref[0]

    @pl.when(k == 0)
    def _():
        h_ref[...] = _norm_mod(x_ref[0], nw_ref[...], m[4:5], m[3:4]).astype(BF16)
        acc_ref[...] = jnp.zeros_like(acc_ref)

    a = jnp.maximum(jnp.dot(h_ref[...], w1_ref[...], preferred_element_type=F32), 0.0)
    acc_ref[...] += _mm(a * a, w2_ref[...])

    @pl.when(k == pl.num_programs(2) - 1)
    def _():
        o_ref[0] = x_ref[0] + m[5:6] * acc_ref[...]


def _mlp(x, mod_l, mod_row, nw, w1, w2, layer, tm, tf):
    bsz, seq, d = x.shape
    return pl.pallas_call(
        _mlp_kernel,
        grid=(bsz, seq // tm, D_FF // tf),
        in_specs=[pl.BlockSpec((1, tm, d), lambda b, i, k: (b, i, 0)),
                  pl.BlockSpec((1, 6, d), lambda b, i, k: (mod_row(b), 0, 0)),
                  pl.BlockSpec((1, d), lambda b, i, k: (0, 0)),
                  pl.BlockSpec((None, d, tf), lambda b, i, k: (layer, 0, k)),
                  pl.BlockSpec((None, tf, d), lambda b, i, k: (layer, k, 0))],
        out_specs=pl.BlockSpec((1, tm, d), lambda b, i, k: (b, i, 0)),
        out_shape=jax.ShapeDtypeStruct((bsz, seq, d), F32),
        scratch_shapes=[pltpu.VMEM((tm, d), BF16), pltpu.VMEM((tm, d), F32)],
        compiler_params=_cparams(("arbitrary", "arbitrary", "arbitrary")),
        name="mlp",
    )(x, mod_l, nw.reshape(1, d), w1, w2)


def _reorder_w_in(w_in):
    depth, d, _ = w_in.shape
    o = 0
    seg = {}
    for name, size in (("qkv", GDN_QKV), ("gz", GDN_V_DIM), ("beta", 2 * GDN_HEADS), ("a", 2 * GDN_HEADS),
                       ("sz", SSD_D_INNER), ("xbc", SSD_XBC), ("dt", 2 * SSD_HEADS),
                       ("aq", ATT_Q_DIM), ("akv", 2 * ATT_KV_DIM), ("gate", 3 * D_MODEL)):
        seg[name] = w_in[:, :, o:o + size]
        o += size
    pad = jnp.zeros((depth, d, NP_COLS - (C_SMALL + SM_DT + 2 * SSD_HEADS)), w_in.dtype)
    out = jnp.concatenate([seg["qkv"], seg["xbc"], seg["akv"], seg["gz"], seg["sz"], seg["aq"], seg["gate"],
                           seg["beta"], seg["a"], seg["dt"], pad], axis=-1)
    return out.astype(BF16)


def _rope_tables(seq):
    t = jnp.arange(seq, dtype=jnp.int32)
    q = ATT_HEAD_DIM // 4
    freqs = ROPE_THETA ** (-jnp.arange(q, dtype=F32) / q)
    ang_r = (t // GRID_W).astype(F32)[:, None] * freqs[None, :]
    ang_c = (t % GRID_W).astype(F32)[:, None] * freqs[None, :]
    cos = jnp.concatenate([jnp.cos(ang_r)] * 2 + [jnp.cos(ang_c)] * 2, axis=-1)
    sin = jnp.concatenate([-jnp.sin(ang_r), jnp.sin(ang_r), -jnp.sin(ang_c), jnp.sin(ang_c)], axis=-1)
    return cos, sin


TM_PROJ = 1024
TM_MLP = 1024
TF_MLP = 1024
TM_MERGE = 512
TQ_ATTN = 256
TS_PREP = 512


def _tile(seq, want):
    return min(seq, want)


def kernel(x, c, ctx, c_ctx, w_mod, b_mod, norm_mix, norm_mlp, w_in, gdn_conv, gdn_a_log, gdn_dt_bias, gdn_norm,
           ssd_conv_w, ssd_conv_b, ssd_a_log, ssd_dt_bias, ssd_d, ssd_norm, att_q_norm, att_k_norm,
           w_br_gdn, w_br_ssd, w_br_att, w_out, w_ff1, w_ff2):
    bsz, seq, d = x.shape
    ctx_len = ctx.shape[1]
    depth = w_in.shape[0]
    assert bsz < MOD_ROWS and d == D_MODEL

    cc = jnp.zeros((MOD_ROWS, d), F32).at[:bsz].set(c).at[bsz].set(c_ctx)
    mod = _modulation(cc, w_mod, b_mod).reshape(depth, MOD_ROWS, 6, d)
    lat_row = lambda b: b
    ctx_row = lambda b: bsz

    w_in_r = _reorder_w_in(w_in)
    wg, ws, wa, wo = (w.astype(BF16) for w in (w_br_gdn, w_br_ssd, w_br_att, w_out))
    w1, w2 = w_ff1.astype(BF16), w_ff2.astype(BF16)
    cos, sin = _rope_tables(seq)

    zg = jnp.zeros((bsz, GDN_PAIRS, GDN_DK, 2 * GDN_DV), F32)
    zs = jnp.zeros((bsz, SSD_HEADS // 2, SSD_STATE, 2 * SSD_HEAD_DIM), F32)
    conv_w = jnp.concatenate([gdn_conv, ssd_conv_w], axis=-1)
    conv_b = jnp.concatenate([jnp.zeros((depth, 1, GDN_QKV), F32), ssd_conv_b[:, None, :]], axis=-1)
    xc = ctx
    for l in range(depth):
        last = l == depth - 1
        p_lat, sm_lat = _in_projection(x, mod[l], lat_row, norm_mix[l], w_in_r, conv_w[l], conv_b[l], l,
                                       _tile(seq, TM_PROJ))
        p_ctx, sm_ctx = _in_projection(xc, mod[l], ctx_row, norm_mix[l], w_in_r, conv_w[l], conv_b[l], l,
                                       _tile(ctx_len, TM_PROJ))

        ogf_c, ogb_c, sgf, sgb = _gdn_scan(p_ctx, sm_ctx, gdn_a_log[l], gdn_dt_bias[l], zg, zg)
        ogf_l, ogb_l, _, _ = _gdn_scan(p_lat, sm_lat, gdn_a_log[l], gdn_dt_bias[l], sgf, sgb)
        ysf_c, ysb_c, ssf, ssb = _ssd_scan(p_ctx, sm_ctx, ssd_a_log[l], ssd_dt_bias[l], zs, zs)
        ysf_l, ysb_l, _, _ = _ssd_scan(p_lat, sm_lat, ssd_a_log[l], ssd_dt_bias[l], ssf, ssb)

        k_c, v_c = _kv_prep(p_ctx, att_k_norm[l], cos, sin, False, _tile(ctx_len, TS_PREP))
        k_l, v_l = _kv_prep(p_lat, att_k_norm[l], cos, sin, True, _tile(seq, TS_PREP))
        att_l = _attention(p_lat, att_q_norm[l], cos, sin, [(k_c, v_c), (k_l, v_l)], True, _tile(seq, TQ_ATTN))

        x = _merge(x, mod[l], lat_row, ogf_l, ogb_l, ysf_l, ysb_l, att_l, p_lat,
                   gdn_norm[l], ssd_norm[l], ssd_d[l], wg, ws, wa, wo, l, _tile(seq, TM_MERGE))
        x = _mlp(x, mod[l], lat_row, norm_mlp[l], w1, w2, l, _tile(seq, TM_MLP), TF_MLP)

        if not last:
            att_c = _attention(p_ctx, att_q_norm[l], cos, sin, [(k_c, v_c)], False, _tile(ctx_len, TQ_ATTN))
            xc = _merge(xc, mod[l], ctx_row, ogf_c, ogb_c, ysf_c, ysb_c, att_c, p_ctx,
                        gdn_norm[l], ssd_norm[l], ssd_d[l], wg, ws, wa, wo, l, _tile(ctx_len, TM_MERGE))
            xc = _mlp(xc, mod[l], ctx_row, norm_mlp[l], w1, w2, l, _tile(ctx_len, TM_MLP), TF_MLP)
    return x
```

```python
import functools

import jax
import jax.numpy as jnp
from jax import lax
from jax.experimental import pallas as pl
from jax.experimental.pallas import tpu as pltpu

F32 = jnp.float32
BF16 = jnp.bfloat16

D_MODEL = 1024
GRID_W = 64
EPS = 1e-6

GDN_HEADS = 8
GDN_DK = 128
GDN_DV = 128
GDN_CHUNK = 64
GDN_QK_DIM = GDN_HEADS * GDN_DK
GDN_V_DIM = GDN_HEADS * GDN_DV
GDN_QKV = 2 * GDN_QK_DIM + GDN_V_DIM

SSD_D_INNER = D_MODEL
SSD_HEAD_DIM = 64
SSD_HEADS = SSD_D_INNER // SSD_HEAD_DIM
SSD_GROUPS = 2
SSD_HPG = SSD_HEADS // SSD_GROUPS
SSD_STATE = 128
SSD_CHUNK = 128
SSD_XBC = SSD_D_INNER + 2 * SSD_GROUPS * SSD_STATE

ATT_HEADS = 8
ATT_KV_HEADS = 2
ATT_REP = ATT_HEADS // ATT_KV_HEADS
ATT_HEAD_DIM = 128
ATT_Q_DIM = ATT_HEADS * ATT_HEAD_DIM
ATT_KV_DIM = ATT_KV_HEADS * ATT_HEAD_DIM
ROPE_THETA = 10000.0
LOG2E = 1.4426950408889634
D_FF = 4 * D_MODEL

LANES = 128
HALO = 16
MOD_ROWS = 16

C_QKV = 0
C_XBC = C_QKV + GDN_QKV
C_AKV = C_XBC + SSD_XBC
C_GZ = C_AKV + 2 * ATT_KV_DIM
C_SZ = C_GZ + GDN_V_DIM
C_AQ = C_SZ + SSD_D_INNER
C_GATE = C_AQ + ATT_Q_DIM
C_SMALL = C_GATE + 3 * D_MODEL
NP_COLS = 11520
N_TILES = 5
CONV_TILES = 2
CONV_COLS = GDN_QKV + SSD_XBC
CONV_SUB = 256
SM_BETA = 0
SM_A = 2 * GDN_HEADS
SM_DT = 4 * GDN_HEADS

VMEM_LIMIT = 56 * 1024 * 1024


def _cparams(sem):
    return pltpu.CompilerParams(dimension_semantics=sem, vmem_limit_bytes=VMEM_LIMIT)


def _mm(a, b):
    return jnp.dot(a.astype(BF16), b.astype(BF16), preferred_element_type=F32)


def _mm_nt(a, b):
    return lax.dot_general(a.astype(BF16), b.astype(BF16), (((1,), (1,)), ((), ())),
                           preferred_element_type=F32)


def _mm_tn(a, b):
    return lax.dot_general(a.astype(BF16), b.astype(BF16), (((0,), (0,)), ((), ())),
                           preferred_element_type=F32)


def _split3(x):
    hi = x.astype(BF16)
    r = x - hi.astype(F32)
    mid = r.astype(BF16)
    lo = (r - mid.astype(F32)).astype(BF16)
    return hi, mid, lo


def _mm_sel_l(sel, x):
    hi, mid, lo = _split3(x)
    d = lambda p: jnp.dot(sel, p, preferred_element_type=F32)
    return (d(hi) + d(mid)) + d(lo)


def _mm_sel_r(x, sel):
    hi, mid, lo = _split3(x)
    d = lambda p: jnp.dot(p, sel, preferred_element_type=F32)
    return (d(hi) + d(mid)) + d(lo)


def _sigmoid(x):
    return 0.5 * jnp.tanh(0.5 * x) + 0.5


def _silu(x):
    return x * _sigmoid(x)


def _softplus(x):
    return jnp.maximum(x, 0.0) + jnp.log(1.0 + jnp.exp(-jnp.abs(x)))


def _rms(x, w):
    return x * lax.rsqrt(jnp.mean(x * x, axis=-1, keepdims=True) + EPS) * w


def _norm_mod(x, nw, scale, shift):
    return _rms(x, nw) * (1.0 + scale) + shift


def _tri(n, lower):
    i = lax.broadcasted_iota(jnp.int32, (n, n), 0)
    j = lax.broadcasted_iota(jnp.int32, (n, n), 1)
    return (i >= j) if lower else (i <= j)


def _rope(x, cos, sin):
    lane = lax.broadcasted_iota(jnp.int32, x.shape, 1)
    q = LANES // 4
    swapped = jnp.where((lane & q) == 0, pltpu.roll(x, LANES - q, 1), pltpu.roll(x, q, 1))
    return x * cos + swapped * sin


def _mod_kernel(c_ref, w_ref, b_ref, o_ref):
    o_ref[0] = _mm(_silu(c_ref[...]), w_ref[0]) + b_ref[0]


def _modulation(cc, w_mod, b_mod):
    depth = w_mod.shape[0]
    n = w_mod.shape[2]
    tn = D_MODEL
    return pl.pallas_call(
        _mod_kernel,
        grid=(depth, n // tn),
        in_specs=[pl.BlockSpec((MOD_ROWS, D_MODEL), lambda l, j: (0, 0)),
                  pl.BlockSpec((1, D_MODEL, tn), lambda l, j: (l, 0, j)),
                  pl.BlockSpec((1, 1, tn), lambda l, j: (l, 0, j))],
        out_specs=pl.BlockSpec((1, MOD_ROWS, tn), lambda l, j: (l, 0, j)),
        out_shape=jax.ShapeDtypeStruct((depth, MOD_ROWS, n), F32),
        compiler_params=_cparams(("arbitrary", "arbitrary")),
        name="modulation",
    )(cc, w_mod, b_mod.reshape(depth, 1, n))


def _conv_tile(h_ref, w_ref, cw_ref, cb_ref, o_ref, tm, first_tile, last_tile, l2_blocks, with_bias):
    sub = min(tm, CONV_SUB)
    nsub = tm // sub
    tn = w_ref.shape[1]
    cw = cw_ref[...]
    rid = lax.broadcasted_iota(jnp.int32, (sub, tn), 0)
    for r in range(nsub):
        p = jnp.dot(h_ref[r * sub:r * sub + sub + 2 * HALO], w_ref[...], preferred_element_type=F32)
        n = sub + 2 * HALO
        xm1 = pltpu.roll(p, 1, 0)[HALO:HALO + sub]
        xp1 = pltpu.roll(p, n - 1, 0)[HALO:HALO + sub]
        if r == 0:
            xm1 = jnp.where((rid == 0) & first_tile, 0.0, xm1)
        if r == nsub - 1:
            xp1 = jnp.where((rid == sub - 1) & last_tile, 0.0, xp1)
        y = cw[0:1] * xm1 + cw[1:2] * p[HALO:HALO + sub] + cw[2:3] * xp1
        y = _silu(y + cb_ref[...] if with_bias else y)
        if l2_blocks:
            parts = []
            for blk in range(tn // LANES):
                yb = y[:, blk * LANES:(blk + 1) * LANES]
                if blk < l2_blocks:
                    inv = lax.rsqrt(jnp.sum(yb * yb, axis=-1, keepdims=True) + EPS)
                    yb = yb * (inv * GDN_DK ** -0.5 if blk < GDN_HEADS else inv)
                parts.append(yb)
            y = jnp.concatenate(parts, axis=-1)
        o_ref[0, r * sub:(r + 1) * sub, :] = y.astype(o_ref.dtype)


def _inproj_kernel(x_ref, xp_ref, xn_ref, mod_ref, nw_ref, w_ref, cw_ref, cb_ref, o_ref, sm_ref, h_ref):
    i = pl.program_id(1)
    j = pl.program_id(2)
    tm = x_ref.shape[1]

    @pl.when(j == 0)
    def _():
        m = mod_ref[0]
        norm = lambda x: _norm_mod(x, nw_ref[...], m[1:2], m[0:1]).astype(BF16)
        h_ref[0:HALO] = norm(xp_ref[0])
        h_ref[HALO:HALO + tm] = norm(x_ref[0])
        h_ref[HALO + tm:2 * HALO + tm] = norm(xn_ref[0])

    first_tile = i == 0
    last_tile = i == pl.num_programs(1) - 1

    @pl.when(j == 0)
    def _():
        _conv_tile(h_ref, w_ref, cw_ref, cb_ref, o_ref, tm, first_tile, last_tile, 2 * GDN_HEADS, False)

    @pl.when((j > 0) & (j < CONV_TILES))
    def _():
        _conv_tile(h_ref, w_ref, cw_ref, cb_ref, o_ref, tm, first_tile, last_tile, 0, True)

    @pl.when(j >= CONV_TILES)
    def _():
        p = jnp.dot(h_ref[HALO:HALO + tm], w_ref[...], preferred_element_type=F32)
        o_ref[0] = p.astype(o_ref.dtype)

        @pl.when(j == pl.num_programs(2) - 1)
        def _():
            off = C_SMALL - (NP_COLS // N_TILES) * (N_TILES - 1)
            sm_ref[0] = p[:, off:off + LANES]


def _in_projection(x, mod_l, mod_row, nw, w_r, conv_w, conv_b, layer, tm):
    bsz, seq, d = x.shape
    tn = NP_COLS // N_TILES
    assert CONV_COLS == CONV_TILES * tn and 2 * GDN_QK_DIM <= tn <= GDN_QKV
    nrb = seq // HALO
    rpb = tm // HALO
    ct = lambda j: jnp.minimum(j, CONV_TILES - 1)
    return pl.pallas_call(
        _inproj_kernel,
        grid=(bsz, seq // tm, N_TILES),
        in_specs=[pl.BlockSpec((1, tm, d), lambda b, i, j: (b, i, 0)),
                  pl.BlockSpec((1, HALO, d), lambda b, i, j: (b, jnp.maximum(i * rpb - 1, 0), 0)),
                  pl.BlockSpec((1, HALO, d), lambda b, i, j: (b, jnp.minimum((i + 1) * rpb, nrb - 1), 0)),
                  pl.BlockSpec((1, 6, d), lambda b, i, j: (mod_row(b), 0, 0)),
                  pl.BlockSpec((1, d), lambda b, i, j: (0, 0)),
                  pl.BlockSpec((None, d, tn), lambda b, i, j: (layer, 0, j)),
                  pl.BlockSpec((3, tn), lambda b, i, j: (0, ct(j))),
                  pl.BlockSpec((1, tn), lambda b, i, j: (0, ct(j)))],
        out_specs=[pl.BlockSpec((1, tm, tn), lambda b, i, j: (b, i, j)),
                   pl.BlockSpec((1, tm, LANES), lambda b, i, j: (b, i, 0))],
        out_shape=[jax.ShapeDtypeStruct((bsz, seq, NP_COLS), BF16),
                   jax.ShapeDtypeStruct((bsz, seq, LANES), F32)],
        scratch_shapes=[pltpu.VMEM((tm + 2 * HALO, d), BF16)],
        compiler_params=_cparams(("arbitrary", "arbitrary", "arbitrary")),
        name="in_projection",
    )(x, x, x, mod_l, nw.reshape(1, d), w_r, conv_w, conv_b)


GDN_PACK = 2
GDN_STEP_CHUNKS = 4
GDN_PAIRS = GDN_HEADS // 2


def _bd4(x):
    x = x.astype(BF16)
    n = x.shape[0]
    tiles = x.shape[1] // LANES
    lo = lax.broadcasted_iota(jnp.int32, (n, LANES), 1) < LANES // 2
    z = jnp.zeros((n, LANES), BF16)
    rows = []
    for t in range(tiles):
        a = x[:, t * LANES:(t + 1) * LANES]
        for half in (jnp.where(lo, a, z), jnp.where(lo, z, a)):
            rows.append(jnp.concatenate([half if tt == t else z for tt in range(tiles)], 1) if tiles > 1 else half)
    return jnp.concatenate(rows, 0)


def _bd_blocks(blocks):
    z = jnp.zeros_like(blocks[0])
    n = len(blocks)
    return jnp.concatenate([jnp.concatenate([blk if j == i else z for j in range(n)], 1)
                            for i, blk in enumerate(blocks)], 0)


def _pack_cols(cols):
    n = cols[0].shape[0]
    lo = lax.broadcasted_iota(jnp.int32, (n, LANES), 1) < LANES // 2
    bc = [jnp.broadcast_to(col, (n, LANES)) for col in cols]
    tiles = [jnp.where(lo, bc[2 * t], bc[2 * t + 1]) for t in range(len(cols) // 2)]
    return jnp.concatenate(tiles, 1) if len(tiles) > 1 else tiles[0]


def _chunk_tri(n, c, lower):
    i = lax.broadcasted_iota(jnp.int32, (n, n), 0)
    j = lax.broadcasted_iota(jnp.int32, (n, n), 1)
    same = (i // c) == (j // c)
    return (same & ((i >= j) if lower else (i <= j))).astype(BF16)


def _gdn_gates(s_ref, alog_l, dtb_l, alog_s, dtb_s, fwd):
    raw = s_ref[0]
    n = raw.shape[0]
    lbeta = -_softplus(-raw)
    la = -jnp.exp(alog_l) * _softplus(raw + dtb_l)
    rraw = raw.T
    beta_r = _sigmoid(rraw)
    la_r = -jnp.exp(alog_s) * _softplus(rraw + dtb_s)
    low = _chunk_tri(n, GDN_CHUNK, True)
    up = _chunk_tri(n, GDN_CHUNK, False)
    if fwd:
        return lbeta, _mm_sel_l(low, la), beta_r, _mm_sel_r(la_r, up)
    return lbeta, _mm_sel_l(up, la), beta_r, _mm_sel_r(la_r, low)


def _gdn_pre(groups):
    c = GDN_CHUNK
    w = GDN_PACK * c
    ii = lax.broadcasted_iota(jnp.int32, (c, w), 0)
    jj = lax.broadcasted_iota(jnp.int32, (c, w), 1) & (c - 1)
    masks = {True: (ii >= jj, ii > jj), False: (ii <= jj, ii < jj)}
    for g in groups:
        g["gc_p"] = _pack_cols(g["gcs"])
        g["gcb_p"] = _pack_cols([gc + lb for gc, lb in zip(g["gcs"], g["lbs"])])
        g["gr_p"] = jnp.concatenate(g["grs"], 1)
        g["beta_rp"] = jnp.concatenate(g["brs"], 1)
        g["kbd"] = _bd_blocks([g["k4"][:, i * GDN_DK:(i + 1) * GDN_DK] for i in range(GDN_PACK)])
    prods = [_mm_nt(jnp.concatenate([g["k4"], g["q4"]], 0), g["kbd"]) for g in groups]
    for g, kq in zip(groups, prods):
        incl, strict = masks[g["fwd"]]
        g["a_p"] = kq[:c] * jnp.exp(jnp.where(strict, g["gcb_p"] - g["gr_p"], -jnp.inf))
        g["qk_p"] = kq[c:] * jnp.exp(jnp.where(incl, g["gc_p"] - g["gr_p"], -jnp.inf))
    for level in range(c.bit_length() - 1):
        bi, bj = ii >> level, jj >> level
        pair = {True: ((bi & 1) == 1) & (bj == bi - 1), False: ((bi & 1) == 0) & (bj == bi + 1)}
        offs = [jnp.where(pair[g["fwd"]], g["a_p"], 0.0) for g in groups]
        if level == 0:
            eye = (ii == jj).astype(F32)
            for g, off in zip(groups, offs):
                g["inv"] = eye - off
        else:
            tmps = [jnp.dot(g["inv"].astype(BF16), _bd4(off), preferred_element_type=F32)
                    for g, off in zip(groups, offs)]
            for g, tmp in zip(groups, tmps):
                g["inv"] = g["inv"] - jnp.dot(tmp.astype(BF16), _bd4(g["inv"]), preferred_element_type=F32)
    for g in groups:
        vbd = _bd_blocks([g["v4"][:, i * GDN_DV:(i + 1) * GDN_DV] for i in range(GDN_PACK)])
        g["u4"] = _mm(g["inv"] * g["beta_rp"], vbd)
        g["w4"] = _mm(g["inv"] * (g["beta_rp"] * jnp.exp(g["gr_p"])), g["kbd"])


def _gdn_pair_steps(items, states):
    c = GDN_CHUNK
    rs = []
    for it, s2 in zip(items, states):
        sbd = _bd_blocks([s2[:, :GDN_DV].astype(BF16), s2[:, GDN_DV:].astype(BF16)])
        egs = [jnp.exp(g) for g in it["gcs"]]
        qd2 = jnp.concatenate([it["q2"][:, i * GDN_DK:(i + 1) * GDN_DK] * egs[i] for i in range(2)], 1)
        rs.append(_mm(jnp.concatenate([it["w2"], qd2], 0), sbd))
    outs, news = [], []
    for it, s2, r in zip(items, states, rs):
        g_lasts = [g[c - 1:c] if it["fwd"] else g[0:1] for g in it["gcs"]]
        vn2 = it["u2"] - r[:c]
        vnbd = _bd_blocks([vn2[:, :GDN_DV].astype(BF16), vn2[:, GDN_DV:].astype(BF16)])
        outs.append(r[c:] + _mm(it["qk2"], vnbd))
        kd = jnp.concatenate([it["k2"][:, i * GDN_DK:(i + 1) * GDN_DK] * jnp.exp(g_lasts[i] - it["gcs"][i])
                              for i in range(2)], 0)
        cd2 = jnp.concatenate([jnp.broadcast_to(jnp.exp(gl), (1, GDN_DV)) for gl in g_lasts], 1)
        news.append(s2 * cd2 + _mm_tn(kd, vnbd))
    return outs, news


def _gdn_kernel(qf_ref, kf_ref, vf_ref, sf_ref, qb_ref, kb_ref, vb_ref, sb_ref,
                alog_l_ref, dtb_l_ref, alog_s_ref, dtb_s_ref, s0f_ref, s0b_ref,
                of_ref, ob_ref, stf_ref, stb_ref):
    @pl.when(pl.program_id(1) == 0)
    def _():
        stf_ref[...] = s0f_ref[...]
        stb_ref[...] = s0b_ref[...]

    c = GDN_CHUNK
    nck = qf_ref.shape[1] // c
    alog_l, dtb_l, alog_s, dtb_s = alog_l_ref[...], dtb_l_ref[...], alog_s_ref[...], dtb_s_ref[...]
    groups = []
    for d, (q_ref, k_ref, v_ref, s_ref, o_ref, st_ref) in enumerate(
            ((qf_ref, kf_ref, vf_ref, sf_ref, of_ref, stf_ref),
             (qb_ref, kb_ref, vb_ref, sb_ref, ob_ref, stb_ref))):
        fwd = d == 0
        lbeta_all, gc_all, beta_r_all, gr_all = _gdn_gates(s_ref, alog_l, dtb_l, alog_s, dtb_s, fwd)
        for t in range(nck):
            ci = t if fwd else nck - 1 - t
            rows = slice(ci * c, (ci + 1) * c)
            for half in range(GDN_HEADS // GDN_PACK):
                h0 = half * GDN_PACK
                lanes_b = [SM_BETA + d * GDN_HEADS + h0 + i for i in range(GDN_PACK)]
                lanes_a = [SM_A + d * GDN_HEADS + h0 + i for i in range(GDN_PACK)]
                sl4 = slice(h0 * GDN_DK, (h0 + GDN_PACK) * GDN_DK)
                groups.append(dict(
                    fwd=fwd, t=t, rows=rows, h0=h0, o_ref=o_ref, st_ref=st_ref,
                    q4=q_ref[0, rows, sl4], k4=k_ref[0, rows, sl4], v4=v_ref[0, rows, sl4],
                    gcs=[gc_all[rows, la:la + 1] for la in lanes_a],
                    lbs=[lbeta_all[rows, lb:lb + 1] for lb in lanes_b],
                    grs=[gr_all[la:la + 1, rows] for la in lanes_a],
                    brs=[beta_r_all[lb:lb + 1, rows] for lb in lanes_b]))
    _gdn_pre(groups)

    keys = [(d, pair) for d in range(2) for pair in range(GDN_PAIRS)]
    st_refs = (stf_ref, stb_ref)
    states = [st_refs[d][0, pair] for d, pair in keys]
    for t in range(nck):
        items = []
        for d, pair in keys:
            ppg = GDN_PACK // 2
            g = next(g for g in groups if g["fwd"] == (d == 0) and g["t"] == t and g["h0"] == (pair // ppg) * GDN_PACK)
            pr = pair % ppg
            s2 = slice(pr * 2 * GDN_DK, (pr + 1) * 2 * GDN_DK)
            items.append(dict(fwd=g["fwd"], q2=g["q4"][:, s2], k2=g["k4"][:, s2], u2=g["u4"][:, s2], w2=g["w4"][:, s2],
                              qk2=g["qk_p"][:, pr * 2 * c:(pr + 1) * 2 * c], gcs=g["gcs"][2 * pr:2 * pr + 2],
                              o_ref=g["o_ref"], rows=g["rows"], pair=pair))
        outs, states = _gdn_pair_steps(items, states)
        for it, o2 in zip(items, outs):
            it["o_ref"][0, it["rows"], it["pair"] * 2 * GDN_DV:(it["pair"] + 1) * 2 * GDN_DV] = o2.astype(BF16)
    for (d, pair), s2 in zip(keys, states):
        st_refs[d][0, pair] = s2


def _lane_param(vals, offset):
    flat = vals.reshape(-1).astype(F32)
    v = jnp.zeros((LANES,), F32).at[offset:offset + flat.shape[0]].set(flat)
    return v.reshape(1, LANES), v.reshape(LANES, 1)


def _gdn_scan(qkv, small, a_log, dt_bias, s0f, s0b):
    bsz, seq, _ = qkv.shape
    blk = min(seq, GDN_STEP_CHUNKS * GDN_CHUNK)
    nb = seq // blk
    alog_l, alog_s = _lane_param(a_log, SM_A)
    dtb_l, dtb_s = _lane_param(dt_bias, SM_A)
    hw = GDN_QK_DIM

    def block_specs(bmap):
        return [pl.BlockSpec((1, blk, hw), lambda b, i: (b, bmap(i), 0)),
                pl.BlockSpec((1, blk, hw), lambda b, i: (b, bmap(i), 1)),
                pl.BlockSpec((1, blk, hw), lambda b, i: (b, bmap(i), 2)),
                pl.BlockSpec((1, blk, LANES), lambda b, i: (b, bmap(i), 0))]

    fw = lambda i: i
    bw = lambda i: nb - 1 - i
    vec_l = pl.BlockSpec((1, LANES), lambda b, i: (0, 0))
    vec_s = pl.BlockSpec((LANES, 1), lambda b, i: (0, 0))
    st_spec = pl.BlockSpec((1, GDN_PAIRS, GDN_DK, 2 * GDN_DV), lambda b, i: (b, 0, 0, 0))
    out_shape = [jax.ShapeDtypeStruct((bsz, seq, GDN_V_DIM), BF16)] * 2 + \
                [jax.ShapeDtypeStruct((bsz, GDN_PAIRS, GDN_DK, 2 * GDN_DV), F32)] * 2
    return pl.pallas_call(
        _gdn_kernel,
        grid=(bsz, nb),
        in_specs=block_specs(fw) + block_specs(bw) + [vec_l, vec_l, vec_s, vec_s, st_spec, st_spec],
        out_specs=[pl.BlockSpec((1, blk, GDN_V_DIM), lambda b, i: (b, i, 0)),
                   pl.BlockSpec((1, blk, GDN_V_DIM), lambda b, i: (b, nb - 1 - i, 0)),
                   st_spec, st_spec],
        out_shape=out_shape,
        compiler_params=_cparams(("arbitrary", "arbitrary")),
        name="gdn_scan",
    )(qkv, qkv, qkv, small, qkv, qkv, qkv, small, alog_l, dtb_l, alog_s, dtb_s, s0f, s0b)


def _ssd_gates(s_ref, alog_l, dtb_l, alog_s, dtb_s, fwd):
    c = SSD_CHUNK
    raw = s_ref[0]
    da = _softplus(raw + dtb_l) * (-jnp.exp(alog_l))
    dt_r = _softplus(raw.T + dtb_s)
    da_r = dt_r * (-jnp.exp(alog_s))
    low = _tri(c, True).astype(BF16)
    up = _tri(c, False).astype(BF16)
    if fwd:
        return _mm_sel_l(low, da), dt_r, _mm_sel_r(da_r, up)
    return _mm_sel_l(up, da), dt_r, _mm_sel_r(da_r, low)


def _ssd_kernel(xf_ref, bf_ref, cf_ref, sf_ref, xb_ref, bb_ref, cb_ref, sb_ref,
                alog_l_ref, dtb_l_ref, alog_s_ref, dtb_s_ref, s0f_ref, s0b_ref,
                yf_ref, yb_ref, stf_ref, stb_ref):
    @pl.when(pl.program_id(1) == 0)
    def _():
        stf_ref[...] = s0f_ref[...]
        stb_ref[...] = s0b_ref[...]

    c = SSD_CHUNK
    hd = SSD_HEAD_DIM
    alog_l, dtb_l, alog_s, dtb_s = alog_l_ref[...], dtb_l_ref[...], alog_s_ref[...], dtb_s_ref[...]
    lo = lax.broadcasted_iota(jnp.int32, (c, 2 * hd), 1) < hd
    chains = []
    for d, (x_ref, b_ref, c_ref, s_ref, y_ref, st_ref) in enumerate(
            ((xf_ref, bf_ref, cf_ref, sf_ref, yf_ref, stf_ref),
             (xb_ref, bb_ref, cb_ref, sb_ref, yb_ref, stb_ref))):
        fwd = d == 0
        incl = _tri(c, fwd)
        ac_all, dtr_all, ar_all = _ssd_gates(s_ref, alog_l, dtb_l, alog_s, dtb_s, fwd)
        for g in range(SSD_GROUPS):
            bm = b_ref[0, :, g * SSD_STATE:(g + 1) * SSD_STATE].astype(F32)
            cm = c_ref[0, :, g * SSD_STATE:(g + 1) * SSD_STATE].astype(F32)
            cbm = _mm_nt(cm, bm)
            bm_t = bm.T
            for e in range(0, SSD_HPG, 2):
                pair = (g * SSD_HPG + e) // 2
                lns = [SM_DT + d * SSD_HEADS + g * SSD_HPG + e + i for i in range(2)]
                ars = [ar_all[ln:ln + 1, :] for ln in lns]
                chains.append(dict(
                    incl=incl, cm=cm, cbm=cbm, bm_t=bm_t, ars=ars, dtrs=[dtr_all[ln:ln + 1, :] for ln in lns],
                    acols=[jnp.broadcast_to(ac_all[:, ln:ln + 1], (c, c)) for ln in lns],
                    a_lasts=[ar[:, c - 1:c] if fwd else ar[:, 0:1] for ar in ars],
                    x2=x_ref[0, :, pair * 2 * hd:(pair + 1) * 2 * hd], state=st_ref[0, pair],
                    y_ref=y_ref, st_ref=st_ref, pair=pair))
    ys, sts = [], []
    for ch in chains:
        x2 = ch["x2"]
        zx = jnp.zeros_like(x2)
        xs = [jnp.where(lo, x2, zx), jnp.where(lo, zx, x2)]
        s2 = ch["state"].astype(BF16)
        zs = jnp.zeros_like(s2)
        ss = [jnp.where(lo, s2, zs), jnp.where(lo, zs, s2)]
        lhs, rhs, upd = [], [], []
        for i in range(2):
            lmat = jnp.exp(jnp.where(ch["incl"], ch["acols"][i] - ch["ars"][i], -jnp.inf))
            lhs += [(ch["cbm"] * lmat * ch["dtrs"][i]).astype(BF16), (ch["cm"] * jnp.exp(ch["acols"][i])).astype(BF16)]
            rhs += [xs[i], ss[i]]
            upd.append((ch["bm_t"] * (ch["dtrs"][i] * jnp.exp(ch["a_lasts"][i] - ch["ars"][i]))).astype(BF16))
        ys.append(jnp.dot(jnp.concatenate(lhs, 1), jnp.concatenate(rhs, 0), preferred_element_type=F32))
        decay2 = jnp.concatenate([jnp.broadcast_to(jnp.exp(al), (1, hd)) for al in ch["a_lasts"]], 1)
        sts.append(ch["state"] * decay2
                   + jnp.dot(jnp.concatenate(upd, 1), jnp.concatenate(xs, 0), preferred_element_type=F32))
    for ch, y, st in zip(chains, ys, sts):
        ch["y_ref"][0, :, ch["pair"] * 2 * hd:(ch["pair"] + 1) * 2 * hd] = y.astype(BF16)
        ch["st_ref"][0, ch["pair"]] = st


def _ssd_scan(xbc, small, a_log, dt_bias, s0f, s0b):
    bsz, seq, _ = xbc.shape
    c = SSD_CHUNK
    nc = seq // c
    alog_l, alog_s = _lane_param(a_log, SM_DT)
    dtb_l, dtb_s = _lane_param(dt_bias, SM_DT)
    gn = SSD_GROUPS * SSD_STATE

    def chunk_specs(cmap):
        return [pl.BlockSpec((1, c, SSD_D_INNER), lambda b, i: (b, cmap(i), C_XBC // SSD_D_INNER)),
                pl.BlockSpec((1, c, gn), lambda b, i: (b, cmap(i), (C_XBC + SSD_D_INNER) // gn)),
                pl.BlockSpec((1, c, gn), lambda b, i: (b, cmap(i), (C_XBC + SSD_D_INNER) // gn + 1)),
                pl.BlockSpec((1, c, LANES), lambda b, i: (b, cmap(i), 0))]

    fw = lambda i: i
    bw = lambda i: nc - 1 - i
    vec_l = pl.BlockSpec((1, LANES), lambda b, i: (0, 0))
    vec_s = pl.BlockSpec((LANES, 1), lambda b, i: (0, 0))
    st_spec = pl.BlockSpec((1, SSD_HEADS // 2, SSD_STATE, 2 * SSD_HEAD_DIM), lambda b, i: (b, 0, 0, 0))
    out_shape = [jax.ShapeDtypeStruct((bsz, seq, SSD_D_INNER), BF16)] * 2 + \
                [jax.ShapeDtypeStruct((bsz, SSD_HEADS // 2, SSD_STATE, 2 * SSD_HEAD_DIM), F32)] * 2
    return pl.pallas_call(
        _ssd_kernel,
        grid=(bsz, nc),
        in_specs=chunk_specs(fw) + chunk_specs(bw) + [vec_l, vec_l, vec_s, vec_s, st_spec, st_spec],
        out_specs=[pl.BlockSpec((1, c, SSD_D_INNER), lambda b, i: (b, i, 0)),
                   pl.BlockSpec((1, c, SSD_D_INNER), lambda b, i: (b, nc - 1 - i, 0)),
                   st_spec, st_spec],
        out_shape=out_shape,
        compiler_params=_cparams(("arbitrary", "arbitrary")),
        name="ssd_scan",
    )(xbc, xbc, xbc, small, xbc, xbc, xbc, small, alog_l, dtb_l, alog_s, dtb_s, s0f, s0b)


def _kv_prep_kernel(k_ref, v_ref, nw_ref, cos_ref, sin_ref, ko_ref, vo_ref, *, rope):
    parts = []
    for h in range(ATT_KV_HEADS):
        kh = _rms(k_ref[0, :, h * ATT_HEAD_DIM:(h + 1) * ATT_HEAD_DIM].astype(F32), nw_ref[...])
        if rope:
            kh = _rope(kh, cos_ref[...], sin_ref[...])
        parts.append(kh)
    ko_ref[0] = jnp.concatenate(parts, axis=-1).astype(BF16)
    v = v_ref[0]
    ones_blk = (lax.broadcasted_iota(jnp.int32, (v.shape[0], ATT_HEAD_DIM), 1) == 0).astype(BF16)
    vparts = []
    for h in range(ATT_KV_HEADS):
        vparts += [v[:, h * ATT_HEAD_DIM:(h + 1) * ATT_HEAD_DIM].astype(BF16), ones_blk]
    vo_ref[0] = jnp.concatenate(vparts, axis=-1)


def _kv_prep(p, k_norm, cos, sin, rope, ts):
    bsz, seq, _ = p.shape
    kb = C_AKV // ATT_KV_DIM
    tab = pl.BlockSpec((ts, ATT_HEAD_DIM), lambda b, i: (i if rope else 0, 0))
    return pl.pallas_call(
        functools.partial(_kv_prep_kernel, rope=rope),
        grid=(bsz, seq // ts),
        in_specs=[pl.BlockSpec((1, ts, ATT_KV_DIM), lambda b, i: (b, i, kb)),
                  pl.BlockSpec((1, ts, ATT_KV_DIM), lambda b, i: (b, i, kb + 1)),
                  pl.BlockSpec((1, ATT_HEAD_DIM), lambda b, i: (0, 0)), tab, tab],
        out_specs=[pl.BlockSpec((1, ts, ATT_KV_DIM), lambda b, i: (b, i, 0)),
                   pl.BlockSpec((1, ts, 2 * ATT_KV_DIM), lambda b, i: (b, i, 0))],
        out_shape=[jax.ShapeDtypeStruct((bsz, seq, ATT_KV_DIM), BF16),
                   jax.ShapeDtypeStruct((bsz, seq, 2 * ATT_KV_DIM), BF16)],
        compiler_params=_cparams(("arbitrary", "arbitrary")),
        name="kv_prep",
    )(p, p, k_norm.reshape(1, ATT_HEAD_DIM), cos, sin)


def _attn_kernel(*refs, n_seg, rope):
    q_ref, qn_ref, nw_ref, cos_ref, sin_ref, cosn_ref, sinn_ref = refs[:7]
    kv_refs = refs[7:7 + 2 * n_seg]
    o_ref = refs[7 + 2 * n_seg]
    qs_ref = refs[8 + 2 * n_seg]
    qscale = ATT_HEAD_DIM ** -0.5 * LOG2E
    heads = range(ATT_REP)

    def prepare(src_ref, c_ref, s_ref):
        for r in heads:
            qh = _rms(src_ref[0, :, r * ATT_HEAD_DIM:(r + 1) * ATT_HEAD_DIM].astype(F32), nw_ref[...])
            if rope:
                qh = _rope(qh, c_ref[...], s_ref[...])
            qs_ref[r] = (qh * qscale).astype(BF16)

    @pl.when(pl.program_id(2) == 0)
    def _():
        prepare(q_ref, cos_ref, sin_ref)

    qs = [qs_ref[r] for r in heads]
    scores = [[_mm_nt(qs[r], kv_refs[2 * s][0]) for s in range(n_seg)] for r in heads]
    prepare(qn_ref, cosn_ref, sinn_ref)
    maxes = [functools.reduce(jnp.maximum, [jnp.max(sc, axis=-1, keepdims=True) for sc in scores[r]]) for r in heads]
    accs = [functools.reduce(jnp.add, [_mm(jnp.exp2(scores[r][s] - maxes[r]), kv_refs[2 * s + 1][0])
                                        for s in range(n_seg)]) for r in heads]
    for r in heads:
        o_ref[0, :, r * ATT_HEAD_DIM:(r + 1) * ATT_HEAD_DIM] = (
            accs[r][:, :ATT_HEAD_DIM] / accs[r][:, ATT_HEAD_DIM:ATT_HEAD_DIM + 1]).astype(o_ref.dtype)


def _attention(p, q_norm, cos, sin, kv_segs, rope, tq):
    bsz, seq, _ = p.shape
    gw = ATT_REP * ATT_HEAD_DIM
    qb = C_AQ // gw
    n_seg = len(kv_segs)
    nq = seq // tq
    nxt = lambda i: jnp.minimum(i + 1, nq - 1)
    tab = lambda f: pl.BlockSpec((tq, ATT_HEAD_DIM), lambda b, g, i: (f(i) if rope else 0, 0))
    cur = lambda i: i
    in_specs = [pl.BlockSpec((1, tq, gw), lambda b, g, i: (b, i, qb + g)),
                pl.BlockSpec((1, tq, gw), lambda b, g, i: (b, nxt(i), qb + g)),
                pl.BlockSpec((1, ATT_HEAD_DIM), lambda b, g, i: (0, 0)), tab(cur), tab(cur), tab(nxt), tab(nxt)]
    args = [p, p, q_norm.reshape(1, ATT_HEAD_DIM), cos, sin, cos, sin]
    for k_arr, v_arr in kv_segs:
        lk = k_arr.shape[1]
        in_specs += [pl.BlockSpec((1, lk, ATT_HEAD_DIM), lambda b, g, i: (b, 0, g)),
                     pl.BlockSpec((1, lk, 2 * ATT_HEAD_DIM), lambda b, g, i: (b, 0, g))]
        args += [k_arr, v_arr]
    return pl.pallas_call(
        functools.partial(_attn_kernel, n_seg=n_seg, rope=rope),
        grid=(bsz, ATT_KV_HEADS, nq),
        in_specs=in_specs,
        out_specs=pl.BlockSpec((1, tq, gw), lambda b, g, i: (b, i, g)),
        out_shape=jax.ShapeDtypeStruct((bsz, seq, ATT_Q_DIM), BF16),
        scratch_shapes=[pltpu.VMEM((ATT_REP, tq, ATT_HEAD_DIM), BF16)],
        compiler_params=_cparams(("arbitrary", "arbitrary", "arbitrary")),
        name="attention",
    )(*args)


def _merge_kernel(x_ref, mod_ref, of_ref, ob_ref, gz_ref, yf_ref, yb_ref, xs_ref, sz_ref, att_ref,
                  gg_ref, gs_ref, ga_ref, gnw_ref, snw_ref, dsk_ref, wg_ref, ws_ref, wa_ref, wo_ref, o_ref):
    o = of_ref[0].astype(F32) + ob_ref[0].astype(F32)
    gz = gz_ref[0].astype(F32)
    parts = []
    for h in range(GDN_HEADS):
        sl = slice(h * GDN_DV, (h + 1) * GDN_DV)
        parts.append(_rms(o[:, sl], gnw_ref[...]) * _silu(gz[:, sl]))
    y_gdn = jnp.concatenate(parts, axis=-1)

    y = yf_ref[0].astype(F32) + yb_ref[0].astype(F32) + dsk_ref[...] * xs_ref[0].astype(F32)
    y = y * _silu(sz_ref[0].astype(F32))
    snw = snw_ref[...]
    gw = SSD_D_INNER // SSD_GROUPS
    y_ssd = jnp.concatenate([_rms(y[:, g * gw:(g + 1) * gw], snw[:, g * gw:(g + 1) * gw])
                             for g in range(SSD_GROUPS)], axis=-1)

    gate = lambda ref: _sigmoid(ref[0].astype(F32))
    m = (gate(gg_ref) * _mm(y_gdn, wg_ref[...]) + gate(gs_ref) * _mm(y_ssd, ws_ref[...])
         + gate(ga_ref) * _mm(att_ref[0], wa_ref[...]))
    g1 = mod_ref[0][2:3]
    o_ref[0] = x_ref[0] + g1 * _mm(m, wo_ref[...])


def _merge(x, mod_l, mod_row, o_f, o_b, y_f, y_b, att, p, gdn_norm, ssd_norm, ssd_d, wg, ws, wa, wo, layer, tm):
    bsz, seq, d = x.shape
    row = lambda cb: pl.BlockSpec((1, tm, d), lambda b, i: (b, i, cb))
    vec = lambda n: pl.BlockSpec((1, n), lambda b, i: (0, 0))
    wsp = pl.BlockSpec((None, d, d), lambda b, i: (layer, 0, 0), pipeline_mode=pl.Buffered(1))
    return pl.pallas_call(
        _merge_kernel,
        grid=(bsz, seq // tm),
        in_specs=[row(0), pl.BlockSpec((1, 6, d), lambda b, i: (mod_row(b), 0, 0)),
                  row(0), row(0), row(C_GZ // d), row(0), row(0), row(C_XBC // d), row(C_SZ // d), row(0),
                  row(C_GATE // d), row(C_GATE // d + 1), row(C_GATE // d + 2),
                  vec(GDN_DV), vec(d), vec(d), wsp, wsp, wsp, wsp],
        out_specs=row(0),
        out_shape=jax.ShapeDtypeStruct((bsz, seq, d), F32),
        compiler_params=_cparams(("arbitrary", "arbitrary")),
        name="merge",
    )(x, mod_l, o_f, o_b, p, y_f, y_b, p, p, att, p, p, p,
      gdn_norm.reshape(1, GDN_DV), ssd_norm.reshape(1, d),
      jnp.repeat(ssd_d, SSD_HEAD_DIM).reshape(1, d), wg, ws, wa, wo)


def _mlp_kernel(x_ref, mod_ref, nw_ref, w1_ref, w2_ref, o_ref, h_ref, acc_ref):
    k = pl.program_id(2)
    m = mod_ref[0]

    @pl.when(k == 0)
    def _():
        h_ref[...] = _norm_mod(x_ref[0], nw_ref[...], m[4:5], m[3:4]).astype(BF16)
        acc_ref[...] = jnp.zeros_like(acc_ref)

    a = jnp.maximum(jnp.dot(h_ref[...], w1_ref[...], preferred_element_type=F32), 0.0)
    acc_ref[...] += _mm(a * a, w2_ref[...])

    @pl.when(k == pl.num_programs(2) - 1)
    def _():
        o_ref[0] = x_ref[0] + m[5:6] * acc_ref[...]


def _mlp(x, mod_l, mod_row, nw, w1, w2, layer, tm, tf):
    bsz, seq, d = x.shape
    return pl.pallas_call(
        _mlp_kernel,
        grid=(bsz, seq // tm, D_FF // tf),
        in_specs=[pl.BlockSpec((1, tm, d), lambda b, i, k: (b, i, 0)),
                  pl.BlockSpec((1, 6, d), lambda b, i, k: (mod_row(b), 0, 0)),
                  pl.BlockSpec((1, d), lambda b, i, k: (0, 0)),
                  pl.BlockSpec((None, d, tf), lambda b, i, k: (layer, 0, k)),
                  pl.BlockSpec((None, tf, d), lambda b, i, k: (layer, k, 0))],
        out_specs=pl.BlockSpec((1, tm, d), lambda b, i, k: (b, i, 0)),
        out_shape=jax.ShapeDtypeStruct((bsz, seq, d), F32),
        scratch_shapes=[pltpu.VMEM((tm, d), BF16), pltpu.VMEM((tm, d), F32)],
        compiler_params=_cparams(("arbitrary", "arbitrary", "arbitrary")),
        name="mlp",
    )(x, mod_l, nw.reshape(1, d), w1, w2)


def _reorder_w_in(w_in):
    depth, d, _ = w_in.shape
    o = 0
    seg = {}
    for name, size in (("qkv", GDN_QKV), ("gz", GDN_V_DIM), ("beta", 2 * GDN_HEADS), ("a", 2 * GDN_HEADS),
                       ("sz", SSD_D_INNER), ("xbc", SSD_XBC), ("dt", 2 * SSD_HEADS),
                       ("aq", ATT_Q_DIM), ("akv", 2 * ATT_KV_DIM), ("gate", 3 * D_MODEL)):
        seg[name] = w_in[:, :, o:o + size]
        o += size
    pad = jnp.zeros((depth, d, NP_COLS - (C_SMALL + SM_DT + 2 * SSD_HEADS)), w_in.dtype)
    out = jnp.concatenate([seg["qkv"], seg["xbc"], seg["akv"], seg["gz"], seg["sz"], seg["aq"], seg["gate"],
                           seg["beta"], seg["a"], seg["dt"], pad], axis=-1)
    return out.astype(BF16)


def _rope_tables(seq):
    t = jnp.arange(seq, dtype=jnp.int32)
    q = ATT_HEAD_DIM // 4
    freqs = ROPE_THETA ** (-jnp.arange(q, dtype=F32) / q)
    ang_r = (t // GRID_W).astype(F32)[:, None] * freqs[None, :]
    ang_c = (t % GRID_W).astype(F32)[:, None] * freqs[None, :]
    cos = jnp.concatenate([jnp.cos(ang_r)] * 2 + [jnp.cos(ang_c)] * 2, axis=-1)
    sin = jnp.concatenate([-jnp.sin(ang_r), jnp.sin(ang_r), -jnp.sin(ang_c), jnp.sin(ang_c)], axis=-1)
    return cos, sin


TM_PROJ = 1024
TM_MLP = 1024
TF_MLP = 1024
TM_MERGE = 512
TQ_ATTN = 256
TS_PREP = 512


def _tile(seq, want):
    return min(seq, want)


def kernel(x, c, ctx, c_ctx, w_mod, b_mod, norm_mix, norm_mlp, w_in, gdn_conv, gdn_a_log, gdn_dt_bias, gdn_norm,
           ssd_conv_w, ssd_conv_b, ssd_a_log, ssd_dt_bias, ssd_d, ssd_norm, att_q_norm, att_k_norm,
           w_br_gdn, w_br_ssd, w_br_att, w_out, w_ff1, w_ff2):
    bsz, seq, d = x.shape
    ctx_len = ctx.shape[1]
    depth = w_in.shape[0]
    assert bsz < MOD_ROWS and d == D_MODEL

    cc = jnp.zeros((MOD_ROWS, d), F32).at[:bsz].set(c).at[bsz].set(c_ctx)
    mod = _modulation(cc, w_mod, b_mod).reshape(depth, MOD_ROWS, 6, d)
    lat_row = lambda b: b
    ctx_row = lambda b: bsz

    w_in_r = _reorder_w_in(w_in)
    wg, ws, wa, wo = (w.astype(BF16) for w in (w_br_gdn, w_br_ssd, w_br_att, w_out))
    w1, w2 = w_ff1.astype(BF16), w_ff2.astype(BF16)
    cos, sin = _rope_tables(seq)

    zg = jnp.zeros((bsz, GDN_PAIRS, GDN_DK, 2 * GDN_DV), F32)
    zs = jnp.zeros((bsz, SSD_HEADS // 2, SSD_STATE, 2 * SSD_HEAD_DIM), F32)
    conv_w = jnp.concatenate([gdn_conv, ssd_conv_w], axis=-1)
    conv_b = jnp.concatenate([jnp.zeros((depth, 1, GDN_QKV), F32), ssd_conv_b[:, None, :]], axis=-1)
    xc = ctx
    for l in range(depth):
        last = l == depth - 1
        p_lat, sm_lat = _in_projection(x, mod[l], lat_row, norm_mix[l], w_in_r, conv_w[l], conv_b[l], l,
                                       _tile(seq, TM_PROJ))
        p_ctx, sm_ctx = _in_projection(xc, mod[l], ctx_row, norm_mix[l], w_in_r, conv_w[l], conv_b[l], l,
                                       _tile(ctx_len, TM_PROJ))

        ogf_c, ogb_c, sgf, sgb = _gdn_scan(p_ctx, sm_ctx, gdn_a_log[l], gdn_dt_bias[l], zg, zg)
        ogf_l, ogb_l, _, _ = _gdn_scan(p_lat, sm_lat, gdn_a_log[l], gdn_dt_bias[l], sgf, sgb)
        ysf_c, ysb_c, ssf, ssb = _ssd_scan(p_ctx, sm_ctx, ssd_a_log[l], ssd_dt_bias[l], zs, zs)
        ysf_l, ysb_l, _, _ = _ssd_scan(p_lat, sm_lat, ssd_a_log[l], ssd_dt_bias[l], ssf, ssb)

        k_c, v_c = _kv_prep(p_ctx, att_k_norm[l], cos, sin, False, _tile(ctx_len, TS_PREP))
        k_l, v_l = _kv_prep(p_lat, att_k_norm[l], cos, sin, True, _tile(seq, TS_PREP))
        att_l = _attention(p_lat, att_q_norm[l], cos, sin, [(k_c, v_c), (k_l, v_l)], True, _tile(seq, TQ_ATTN))

        x = _merge(x, mod[l], lat_row, ogf_l, ogb_l, ysf_l, ysb_l, att_l, p_lat,
                   gdn_norm[l], ssd_norm[l], ssd_d[l], wg, ws, wa, wo, l, _tile(seq, TM_MERGE))
        x = _mlp(x, mod[l], lat_row, norm_mlp[l], w1, w2, l, _tile(seq, TM_MLP), TF_MLP)

        if not last:
            att_c = _attention(p_ctx, att_q_norm[l], cos, sin, [(k_c, v_c)], False, _tile(ctx_len, TQ_ATTN))
            xc = _merge(xc, mod[l], ctx_row, ogf_c, ogb_c, ysf_c, ysb_c, att_c, p_ctx,
                        gdn_norm[l], ssd_norm[l], ssd_d[l], wg, ws, wa, wo, l, _tile(ctx_len, TM_MERGE))
            xc = _mlp(xc, mod[l], ctx_row, norm_mlp[l], w1, w2, l, _tile(ctx_len, TM_MLP), TF_MLP)
    return x
```

```python
import functools

import jax
import jax.numpy as jnp
from jax import lax
from jax.experimental import pallas as pl
from jax.experimental.pallas import tpu as pltpu

F32 = jnp.float32
BF16 = jnp.bfloat16

D_MODEL = 1024
GRID_W = 64
EPS = 1e-6

GDN_HEADS = 8
GDN_DK = 128
GDN_DV = 128
GDN_CHUNK = 64
GDN_QK_DIM = GDN_HEADS * GDN_DK
GDN_V_DIM = GDN_HEADS * GDN_DV
GDN_QKV = 2 * GDN_QK_DIM + GDN_V_DIM

SSD_D_INNER = D_MODEL
SSD_HEAD_DIM = 64
SSD_HEADS = SSD_D_INNER // SSD_HEAD_DIM
SSD_GROUPS = 2
SSD_HPG = SSD_HEADS // SSD_GROUPS
SSD_STATE = 128
SSD_CHUNK = 128
SSD_XBC = SSD_D_INNER + 2 * SSD_GROUPS * SSD_STATE

ATT_HEADS = 8
ATT_KV_HEADS = 2
ATT_REP = ATT_HEADS // ATT_KV_HEADS
ATT_HEAD_DIM = 128
ATT_Q_DIM = ATT_HEADS * ATT_HEAD_DIM
ATT_KV_DIM = ATT_KV_HEADS * ATT_HEAD_DIM
ROPE_THETA = 10000.0
LOG2E = 1.4426950408889634
D_FF = 4 * D_MODEL

LANES = 128
HALO = 16
MOD_ROWS = 16

C_QKV = 0
C_XBC = C_QKV + GDN_QKV
C_AKV = C_XBC + SSD_XBC
C_GZ = C_AKV + 2 * ATT_KV_DIM
C_SZ = C_GZ + GDN_V_DIM
C_AQ = C_SZ + SSD_D_INNER
C_GATE = C_AQ + ATT_Q_DIM
C_SMALL = C_GATE + 3 * D_MODEL
NP_COLS = 11520
N_TILES = 5
CONV_TILES = 2
CONV_COLS = GDN_QKV + SSD_XBC
CONV_SUB = 256
SM_BETA = 0
SM_A = 2 * GDN_HEADS
SM_DT = 4 * GDN_HEADS

VMEM_LIMIT = 56 * 1024 * 1024


def _cparams(sem):
    return pltpu.CompilerParams(dimension_semantics=sem, vmem_limit_bytes=VMEM_LIMIT)


def _mm(a, b):
    return jnp.dot(a.astype(BF16), b.astype(BF16), preferred_element_type=F32)


def _mm_nt(a, b):
    return lax.dot_general(a.astype(BF16), b.astype(BF16), (((1,), (1,)), ((), ())),
                           preferred_element_type=F32)


def _mm_tn(a, b):
    return lax.dot_general(a.astype(BF16), b.astype(BF16), (((0,), (0,)), ((), ())),
                           preferred_element_type=F32)


def _split3(x):
    hi = x.astype(BF16)
    r = x - hi.astype(F32)
    mid = r.astype(BF16)
    lo = (r - mid.astype(F32)).astype(BF16)
    return hi, mid, lo


def _mm_sel_l(sel, x):
    hi, mid, lo = _split3(x)
    d = lambda p: jnp.dot(sel, p, preferred_element_type=F32)
    return (d(hi) + d(mid)) + d(lo)


def _mm_sel_r(x, sel):
    hi, mid, lo = _split3(x)
    d = lambda p: jnp.dot(p, sel, preferred_element_type=F32)
    return (d(hi) + d(mid)) + d(lo)


def _sigmoid(x):
    return 0.5 * jnp.tanh(0.5 * x) + 0.5


def _silu(x):
    h = 0.5 * x
    return h * jnp.tanh(h) + h


def _softplus(x):
    return jnp.maximum(x, 0.0) + jnp.log(1.0 + jnp.exp(-jnp.abs(x)))


def _rms(x, w):
    return x * lax.rsqrt(jnp.mean(x * x, axis=-1, keepdims=True) + EPS) * w


def _norm_mod(x, nw, scale, shift):
    return _rms(x, nw) * (1.0 + scale) + shift


def _tri(n, lower):
    i = lax.broadcasted_iota(jnp.int32, (n, n), 0)
    j = lax.broadcasted_iota(jnp.int32, (n, n), 1)
    return (i >= j) if lower else (i <= j)


def _rope(x, cos, sin):
    lane = lax.broadcasted_iota(jnp.int32, x.shape, 1)
    q = LANES // 4
    swapped = jnp.where((lane & q) == 0, pltpu.roll(x, LANES - q, 1), pltpu.roll(x, q, 1))
    return x * cos + swapped * sin


def _mod_kernel(c_ref, w_ref, b_ref, o_ref):
    o_ref[0] = _mm(_silu(c_ref[...]), w_ref[0]) + b_ref[0]


def _modulation(cc, w_mod, b_mod):
    depth = w_mod.shape[0]
    n = w_mod.shape[2]
    tn = D_MODEL
    return pl.pallas_call(
        _mod_kernel,
        grid=(depth, n // tn),
        in_specs=[pl.BlockSpec((MOD_ROWS, D_MODEL), lambda l, j: (0, 0)),
                  pl.BlockSpec((1, D_MODEL, tn), lambda l, j: (l, 0, j)),
                  pl.BlockSpec((1, 1, tn), lambda l, j: (l, 0, j))],
        out_specs=pl.BlockSpec((1, MOD_ROWS, tn), lambda l, j: (l, 0, j)),
        out_shape=jax.ShapeDtypeStruct((depth, MOD_ROWS, n), F32),
        compiler_params=_cparams(("arbitrary", "arbitrary")),
        name="modulation",
    )(cc, w_mod, b_mod.reshape(depth, 1, n))


def _conv_tile(h_ref, w_ref, cw_ref, cb_ref, o_ref, tm, first_tile, last_tile, l2_blocks, with_bias):
    sub = min(tm, CONV_SUB)
    nsub = tm // sub
    tn = w_ref.shape[1]
    cw = cw_ref[...]
    rid = lax.broadcasted_iota(jnp.int32, (sub, tn), 0)
    for r in range(nsub):
        p = jnp.dot(h_ref[r * sub:r * sub + sub + 2 * HALO], w_ref[...], preferred_element_type=F32)
        n = sub + 2 * HALO
        xm1 = pltpu.roll(p, 1, 0)[HALO:HALO + sub]
        xp1 = pltpu.roll(p, n - 1, 0)[HALO:HALO + sub]
        if r == 0:
            xm1 = jnp.where((rid == 0) & first_tile, 0.0, xm1)
        if r == nsub - 1:
            xp1 = jnp.where((rid == sub - 1) & last_tile, 0.0, xp1)
        y = cw[0:1] * xm1 + cw[1:2] * p[HALO:HALO + sub] + cw[2:3] * xp1
        y = _silu(y + cb_ref[...] if with_bias else y)
        if l2_blocks:
            parts = []
            for blk in range(tn // LANES):
                yb = y[:, blk * LANES:(blk + 1) * LANES]
                if blk < l2_blocks:
                    inv = lax.rsqrt(jnp.sum(yb * yb, axis=-1, keepdims=True) + EPS)
                    yb = yb * (inv * GDN_DK ** -0.5 if blk < GDN_HEADS else inv)
                parts.append(yb)
            y = jnp.concatenate(parts, axis=-1)
        o_ref[0, r * sub:(r + 1) * sub, :] = y.astype(o_ref.dtype)


def _inproj_kernel(x_ref, xp_ref, xn_ref, mod_ref, nw_ref, w_ref, cw_ref, cb_ref, o_ref, sm_ref, h_ref):
    i = pl.program_id(1)
    j = pl.program_id(2)
    tm = x_ref.shape[1]

    @pl.when(j == 0)
    def _():
        m = mod_ref[0]
        norm = lambda x: _norm_mod(x, nw_ref[...], m[1:2], m[0:1]).astype(BF16)
        h_ref[0:HALO] = norm(xp_ref[0])
        h_ref[HALO:HALO + tm] = norm(x_ref[0])
        h_ref[HALO + tm:2 * HALO + tm] = norm(xn_ref[0])

    first_tile = i == 0
    last_tile = i == pl.num_programs(1) - 1

    @pl.when(j == 0)
    def _():
        _conv_tile(h_ref, w_ref, cw_ref, cb_ref, o_ref, tm, first_tile, last_tile, 2 * GDN_HEADS, False)

    @pl.when((j > 0) & (j < CONV_TILES))
    def _():
        _conv_tile(h_ref, w_ref, cw_ref, cb_ref, o_ref, tm, first_tile, last_tile, 0, True)

    @pl.when(j >= CONV_TILES)
    def _():
        p = jnp.dot(h_ref[HALO:HALO + tm], w_ref[...], preferred_element_type=F32)
        o_ref[0] = p.astype(o_ref.dtype)

        @pl.when(j == pl.num_programs(2) - 1)
        def _():
            off = C_SMALL - (NP_COLS // N_TILES) * (N_TILES - 1)
            sm_ref[0] = p[:, off:off + LANES]


def _in_projection(x, mod_l, mod_row, nw, w_r, conv_w, conv_b, layer, tm):
    bsz, seq, d = x.shape
    tn = NP_COLS // N_TILES
    assert CONV_COLS == CONV_TILES * tn and 2 * GDN_QK_DIM <= tn <= GDN_QKV
    nrb = seq // HALO
    rpb = tm // HALO
    ct = lambda j: jnp.minimum(j, CONV_TILES - 1)
    return pl.pallas_call(
        _inproj_kernel,
        grid=(bsz, seq // tm, N_TILES),
        in_specs=[pl.BlockSpec((1, tm, d), lambda b, i, j: (b, i, 0)),
                  pl.BlockSpec((1, HALO, d), lambda b, i, j: (b, jnp.maximum(i * rpb - 1, 0), 0)),
                  pl.BlockSpec((1, HALO, d), lambda b, i, j: (b, jnp.minimum((i + 1) * rpb, nrb - 1), 0)),
                  pl.BlockSpec((1, 6, d), lambda b, i, j: (mod_row(b), 0, 0)),
                  pl.BlockSpec((1, d), lambda b, i, j: (0, 0)),
                  pl.BlockSpec((None, d, tn), lambda b, i, j: (layer, 0, j)),
                  pl.BlockSpec((3, tn), lambda b, i, j: (0, ct(j))),
                  pl.BlockSpec((1, tn), lambda b, i, j: (0, ct(j)))],
        out_specs=[pl.BlockSpec((1, tm, tn), lambda b, i, j: (b, i, j)),
                   pl.BlockSpec((1, tm, LANES), lambda b, i, j: (b, i, 0))],
        out_shape=[jax.ShapeDtypeStruct((bsz, seq, NP_COLS), BF16),
                   jax.ShapeDtypeStruct((bsz, seq, LANES), F32)],
        scratch_shapes=[pltpu.VMEM((tm + 2 * HALO, d), BF16)],
        compiler_params=_cparams(("arbitrary", "arbitrary", "arbitrary")),
        name="in_projection",
    )(x, x, x, mod_l, nw.reshape(1, d), w_r, conv_w, conv_b)


GDN_PACK = 2
GDN_STEP_CHUNKS = 4
GDN_PAIRS = GDN_HEADS // 2


def _bd4(x):
    x = x.astype(BF16)
    n = x.shape[0]
    tiles = x.shape[1] // LANES
    lo = lax.broadcasted_iota(jnp.int32, (n, LANES), 1) < LANES // 2
    z = jnp.zeros((n, LANES), BF16)
    rows = []
    for t in range(tiles):
        a = x[:, t * LANES:(t + 1) * LANES]
        for half in (jnp.where(lo, a, z), jnp.where(lo, z, a)):
            rows.append(jnp.concatenate([half if tt == t else z for tt in range(tiles)], 1) if tiles > 1 else half)
    return jnp.concatenate(rows, 0)


def _bd_blocks(blocks):
    z = jnp.zeros_like(blocks[0])
    n = len(blocks)
    return jnp.concatenate([jnp.concatenate([blk if j == i else z for j in range(n)], 1)
                            for i, blk in enumerate(blocks)], 0)


def _pack_cols(cols):
    n = cols[0].shape[0]
    lo = lax.broadcasted_iota(jnp.int32, (n, LANES), 1) < LANES // 2
    bc = [jnp.broadcast_to(col, (n, LANES)) for col in cols]
    tiles = [jnp.where(lo, bc[2 * t], bc[2 * t + 1]) for t in range(len(cols) // 2)]
    return jnp.concatenate(tiles, 1) if len(tiles) > 1 else tiles[0]


def _chunk_tri(n, c, lower):
    i = lax.broadcasted_iota(jnp.int32, (n, n), 0)
    j = lax.broadcasted_iota(jnp.int32, (n, n), 1)
    same = (i // c) == (j // c)
    return (same & ((i >= j) if lower else (i <= j))).astype(BF16)


def _gdn_gates(s_ref, alog_l, dtb_l, alog_s, dtb_s, fwd):
    raw = s_ref[0]
    n = raw.shape[0]
    lbeta = -_softplus(-raw)
    la = -jnp.exp(alog_l) * _softplus(raw + dtb_l)
    rraw = raw.T
    beta_r = _sigmoid(rraw)
    la_r = -jnp.exp(alog_s) * _softplus(rraw + dtb_s)
    low = _chunk_tri(n, GDN_CHUNK, True)
    up = _chunk_tri(n, GDN_CHUNK, False)
    if fwd:
        return lbeta, _mm_sel_l(low, la), beta_r, _mm_sel_r(la_r, up)
    return lbeta, _mm_sel_l(up, la), beta_r, _mm_sel_r(la_r, low)


def _gdn_pre(groups):
    c = GDN_CHUNK
    w = GDN_PACK * c
    ii = lax.broadcasted_iota(jnp.int32, (c, w), 0)
    jj = lax.broadcasted_iota(jnp.int32, (c, w), 1) & (c - 1)
    masks = {True: (ii >= jj, ii > jj), False: (ii <= jj, ii < jj)}
    for g in groups:
        g["gc_p"] = _pack_cols(g["gcs"])
        g["gcb_p"] = _pack_cols([gc + lb for gc, lb in zip(g["gcs"], g["lbs"])])
        g["gr_p"] = jnp.concatenate(g["grs"], 1)
        g["beta_rp"] = jnp.concatenate(g["brs"], 1)
        g["kbd"] = _bd_blocks([g["k4"][:, i * GDN_DK:(i + 1) * GDN_DK] for i in range(GDN_PACK)])
    prods = [_mm_nt(jnp.concatenate([g["k4"], g["q4"]], 0), g["kbd"]) for g in groups]
    for g, kq in zip(groups, prods):
        incl, strict = masks[g["fwd"]]
        g["a_p"] = kq[:c] * jnp.exp(jnp.where(strict, g["gcb_p"] - g["gr_p"], -jnp.inf))
        g["qk_p"] = kq[c:] * jnp.exp(jnp.where(incl, g["gc_p"] - g["gr_p"], -jnp.inf))
    for level in range(c.bit_length() - 1):
        bi, bj = ii >> level, jj >> level
        pair = {True: ((bi & 1) == 1) & (bj == bi - 1), False: ((bi & 1) == 0) & (bj == bi + 1)}
        offs = [jnp.where(pair[g["fwd"]], g["a_p"], 0.0) for g in groups]
        if level == 0:
            eye = (ii == jj).astype(F32)
            for g, off in zip(groups, offs):
                g["inv"] = eye - off
        else:
            tmps = [jnp.dot(g["inv"].astype(BF16), _bd4(off), preferred_element_type=F32)
                    for g, off in zip(groups, offs)]
            for g, tmp in zip(groups, tmps):
                g["inv"] = g["inv"] - jnp.dot(tmp.astype(BF16), _bd4(g["inv"]), preferred_element_type=F32)
    for g in groups:
        vbd = _bd_blocks([g["v4"][:, i * GDN_DV:(i + 1) * GDN_DV] for i in range(GDN_PACK)])
        g["u4"] = _mm(g["inv"] * g["beta_rp"], vbd)
        g["w4"] = _mm(g["inv"] * (g["beta_rp"] * jnp.exp(g["gr_p"])), g["kbd"])


def _gdn_pair_steps(items, states):
    c = GDN_CHUNK
    rs = []
    for it, s2 in zip(items, states):
        sbd = _bd_blocks([s2[:, :GDN_DV].astype(BF16), s2[:, GDN_DV:].astype(BF16)])
        egs = [jnp.exp(g) for g in it["gcs"]]
        qd2 = jnp.concatenate([it["q2"][:, i * GDN_DK:(i + 1) * GDN_DK] * egs[i] for i in range(2)], 1)
        rs.append(_mm(jnp.concatenate([it["w2"], qd2], 0), sbd))
    outs, news = [], []
    for it, s2, r in zip(items, states, rs):
        g_lasts = [g[c - 1:c] if it["fwd"] else g[0:1] for g in it["gcs"]]
        vn2 = it["u2"] - r[:c]
        vnbd = _bd_blocks([vn2[:, :GDN_DV].astype(BF16), vn2[:, GDN_DV:].astype(BF16)])
        outs.append(r[c:] + _mm(it["qk2"], vnbd))
        kd = jnp.concatenate([it["k2"][:, i * GDN_DK:(i + 1) * GDN_DK] * jnp.exp(g_lasts[i] - it["gcs"][i])
                              for i in range(2)], 0)
        cd2 = jnp.concatenate([jnp.broadcast_to(jnp.exp(gl), (1, GDN_DV)) for gl in g_lasts], 1)
        news.append(s2 * cd2 + _mm_tn(kd, vnbd))
    return outs, news


def _gdn_kernel(qf_ref, kf_ref, vf_ref, sf_ref, qb_ref, kb_ref, vb_ref, sb_ref,
                alog_l_ref, dtb_l_ref, alog_s_ref, dtb_s_ref, s0f_ref, s0b_ref,
                of_ref, ob_ref, stf_ref, stb_ref):
    @pl.when(pl.program_id(1) == 0)
    def _():
        stf_ref[...] = s0f_ref[...]
        stb_ref[...] = s0b_ref[...]

    c = GDN_CHUNK
    nck = qf_ref.shape[1] // c
    alog_l, dtb_l, alog_s, dtb_s = alog_l_ref[...], dtb_l_ref[...], alog_s_ref[...], dtb_s_ref[...]
    groups = []
    for d, (q_ref, k_ref, v_ref, s_ref, o_ref, st_ref) in enumerate(
            ((qf_ref, kf_ref, vf_ref, sf_ref, of_ref, stf_ref),
             (qb_ref, kb_ref, vb_ref, sb_ref, ob_ref, stb_ref))):
        fwd = d == 0
        lbeta_all, gc_all, beta_r_all, gr_all = _gdn_gates(s_ref, alog_l, dtb_l, alog_s, dtb_s, fwd)
        for t in range(nck):
            ci = t if fwd else nck - 1 - t
            rows = slice(ci * c, (ci + 1) * c)
            for half in range(GDN_HEADS // GDN_PACK):
                h0 = half * GDN_PACK
                lanes_b = [SM_BETA + d * GDN_HEADS + h0 + i for i in range(GDN_PACK)]
                lanes_a = [SM_A + d * GDN_HEADS + h0 + i for i in range(GDN_PACK)]
                sl4 = slice(h0 * GDN_DK, (h0 + GDN_PACK) * GDN_DK)
                groups.append(dict(
                    fwd=fwd, t=t, rows=rows, h0=h0, o_ref=o_ref, st_ref=st_ref,
                    q4=q_ref[0, rows, sl4], k4=k_ref[0, rows, sl4], v4=v_ref[0, rows, sl4],
                    gcs=[gc_all[rows, la:la + 1] for la in lanes_a],
                    lbs=[lbeta_all[rows, lb:lb + 1] for lb in lanes_b],
                    grs=[gr_all[la:la + 1, rows] for la in lanes_a],
                    brs=[beta_r_all[lb:lb + 1, rows] for lb in lanes_b]))
    _gdn_pre(groups)

    keys = [(d, pair) for d in range(2) for pair in range(GDN_PAIRS)]
    st_refs = (stf_ref, stb_ref)
    states = [st_refs[d][0, pair] for d, pair in keys]
    for t in range(nck):
        items = []
        for d, pair in keys:
            ppg = GDN_PACK // 2
            g = next(g for g in groups if g["fwd"] == (d == 0) and g["t"] == t and g["h0"] == (pair // ppg) * GDN_PACK)
            pr = pair % ppg
            s2 = slice(pr * 2 * GDN_DK, (pr + 1) * 2 * GDN_DK)
            items.append(dict(fwd=g["fwd"], q2=g["q4"][:, s2], k2=g["k4"][:, s2], u2=g["u4"][:, s2], w2=g["w4"][:, s2],
                              qk2=g["qk_p"][:, pr * 2 * c:(pr + 1) * 2 * c], gcs=g["gcs"][2 * pr:2 * pr + 2],
                              o_ref=g["o_ref"], rows=g["rows"], pair=pair))
        outs, states = _gdn_pair_steps(items, states)
        for it, o2 in zip(items, outs):
            it["o_ref"][0, it["rows"], it["pair"] * 2 * GDN_DV:(it["pair"] + 1) * 2 * GDN_DV] = o2.astype(BF16)
    for (d, pair), s2 in zip(keys, states):
        st_refs[d][0, pair] = s2


def _lane_param(vals, offset):
    flat = vals.reshape(-1).astype(F32)
    v = jnp.zeros((LANES,), F32).at[offset:offset + flat.shape[0]].set(flat)
    return v.reshape(1, LANES), v.reshape(LANES, 1)


def _gdn_scan(qkv, small, a_log, dt_bias, s0f, s0b):
    bsz, seq, _ = qkv.shape
    blk = min(seq, GDN_STEP_CHUNKS * GDN_CHUNK)
    nb = seq // blk
    alog_l, alog_s = _lane_param(a_log, SM_A)
    dtb_l, dtb_s = _lane_param(dt_bias, SM_A)
    hw = GDN_QK_DIM

    def block_specs(bmap):
        return [pl.BlockSpec((1, blk, hw), lambda b, i: (b, bmap(i), 0)),
                pl.BlockSpec((1, blk, hw), lambda b, i: (b, bmap(i), 1)),
                pl.BlockSpec((1, blk, hw), lambda b, i: (b, bmap(i), 2)),
                pl.BlockSpec((1, blk, LANES), lambda b, i: (b, bmap(i), 0))]

    fw = lambda i: i
    bw = lambda i: nb - 1 - i
    vec_l = pl.BlockSpec((1, LANES), lambda b, i: (0, 0))
    vec_s = pl.BlockSpec((LANES, 1), lambda b, i: (0, 0))
    st_spec = pl.BlockSpec((1, GDN_PAIRS, GDN_DK, 2 * GDN_DV), lambda b, i: (b, 0, 0, 0))
    out_shape = [jax.ShapeDtypeStruct((bsz, seq, GDN_V_DIM), BF16)] * 2 + \
                [jax.ShapeDtypeStruct((bsz, GDN_PAIRS, GDN_DK, 2 * GDN_DV), F32)] * 2
    return pl.pallas_call(
        _gdn_kernel,
        grid=(bsz, nb),
        in_specs=block_specs(fw) + block_specs(bw) + [vec_l, vec_l, vec_s, vec_s, st_spec, st_spec],
        out_specs=[pl.BlockSpec((1, blk, GDN_V_DIM), lambda b, i: (b, i, 0)),
                   pl.BlockSpec((1, blk, GDN_V_DIM), lambda b, i: (b, nb - 1 - i, 0)),
                   st_spec, st_spec],
        out_shape=out_shape,
        compiler_params=_cparams(("arbitrary", "arbitrary")),
        name="gdn_scan",
    )(qkv, qkv, qkv, small, qkv, qkv, qkv, small, alog_l, dtb_l, alog_s, dtb_s, s0f, s0b)


def _ssd_gates(s_ref, alog_l, dtb_l, alog_s, dtb_s, fwd):
    c = SSD_CHUNK
    raw = s_ref[0]
    da = _softplus(raw + dtb_l) * (-jnp.exp(alog_l))
    dt_r = _softplus(raw.T + dtb_s)
    da_r = dt_r * (-jnp.exp(alog_s))
    low = _tri(c, True).astype(BF16)
    up = _tri(c, False).astype(BF16)
    if fwd:
        return _mm_sel_l(low, da), dt_r, _mm_sel_r(da_r, up)
    return _mm_sel_l(up, da), dt_r, _mm_sel_r(da_r, low)


def _ssd_kernel(xf_ref, bf_ref, cf_ref, sf_ref, xb_ref, bb_ref, cb_ref, sb_ref,
                alog_l_ref, dtb_l_ref, alog_s_ref, dtb_s_ref, s0f_ref, s0b_ref,
                yf_ref, yb_ref, stf_ref, stb_ref):
    @pl.when(pl.program_id(1) == 0)
    def _():
        stf_ref[...] = s0f_ref[...]
        stb_ref[...] = s0b_ref[...]

    c = SSD_CHUNK
    hd = SSD_HEAD_DIM
    alog_l, dtb_l, alog_s, dtb_s = alog_l_ref[...], dtb_l_ref[...], alog_s_ref[...], dtb_s_ref[...]
    lo = lax.broadcasted_iota(jnp.int32, (c, 2 * hd), 1) < hd
    chains = []
    for d, (x_ref, b_ref, c_ref, s_ref, y_ref, st_ref) in enumerate(
            ((xf_ref, bf_ref, cf_ref, sf_ref, yf_ref, stf_ref),
             (xb_ref, bb_ref, cb_ref, sb_ref, yb_ref, stb_ref))):
        fwd = d == 0
        incl = _tri(c, fwd)
        ac_all, dtr_all, ar_all = _ssd_gates(s_ref, alog_l, dtb_l, alog_s, dtb_s, fwd)
        for g in range(SSD_GROUPS):
            bm = b_ref[0, :, g * SSD_STATE:(g + 1) * SSD_STATE].astype(F32)
            cm = c_ref[0, :, g * SSD_STATE:(g + 1) * SSD_STATE].astype(F32)
            cbm = _mm_nt(cm, bm)
            bm_t = bm.T
            for e in range(0, SSD_HPG, 2):
                pair = (g * SSD_HPG + e) // 2
                lns = [SM_DT + d * SSD_HEADS + g * SSD_HPG + e + i for i in range(2)]
                ars = [ar_all[ln:ln + 1, :] for ln in lns]
                chains.append(dict(
                    incl=incl, cm=cm, cbm=cbm, bm_t=bm_t, ars=ars, dtrs=[dtr_all[ln:ln + 1, :] for ln in lns],
                    acols=[jnp.broadcast_to(ac_all[:, ln:ln + 1], (c, c)) for ln in lns],
                    a_lasts=[ar[:, c - 1:c] if fwd else ar[:, 0:1] for ar in ars],
                    x2=x_ref[0, :, pair * 2 * hd:(pair + 1) * 2 * hd], state=st_ref[0, pair],
                    y_ref=y_ref, st_ref=st_ref, pair=pair))
    ys, sts = [], []
    for ch in chains:
        x2 = ch["x2"]
        zx = jnp.zeros_like(x2)
        xs = [jnp.where(lo, x2, zx), jnp.where(lo, zx, x2)]
        s2 = ch["state"].astype(BF16)
        zs = jnp.zeros_like(s2)
        ss = [jnp.where(lo, s2, zs), jnp.where(lo, zs, s2)]
        lhs, rhs, upd = [], [], []
        for i in range(2):
            lmat = jnp.exp(jnp.where(ch["incl"], ch["acols"][i] - ch["ars"][i], -jnp.inf))
            lhs += [(ch["cbm"] * lmat * ch["dtrs"][i]).astype(BF16), (ch["cm"] * jnp.exp(ch["acols"][i])).astype(BF16)]
            rhs += [xs[i], ss[i]]
            upd.append((ch["bm_t"] * (ch["dtrs"][i] * jnp.exp(ch["a_lasts"][i] - ch["ars"][i]))).astype(BF16))
        ys.append(jnp.dot(jnp.concatenate(lhs, 1), jnp.concatenate(rhs, 0), preferred_element_type=F32))
        decay2 = jnp.concatenate([jnp.broadcast_to(jnp.exp(al), (1, hd)) for al in ch["a_lasts"]], 1)
        sts.append(ch["state"] * decay2
                   + jnp.dot(jnp.concatenate(upd, 1), jnp.concatenate(xs, 0), preferred_element_type=F32))
    for ch, y, st in zip(chains, ys, sts):
        ch["y_ref"][0, :, ch["pair"] * 2 * hd:(ch["pair"] + 1) * 2 * hd] = y.astype(BF16)
        ch["st_ref"][0, ch["pair"]] = st


def _ssd_scan(xbc, small, a_log, dt_bias, s0f, s0b):
    bsz, seq, _ = xbc.shape
    c = SSD_CHUNK
    nc = seq // c
    alog_l, alog_s = _lane_param(a_log, SM_DT)
    dtb_l, dtb_s = _lane_param(dt_bias, SM_DT)
    gn = SSD_GROUPS * SSD_STATE

    def chunk_specs(cmap):
        return [pl.BlockSpec((1, c, SSD_D_INNER), lambda b, i: (b, cmap(i), C_XBC // SSD_D_INNER)),
                pl.BlockSpec((1, c, gn), lambda b, i: (b, cmap(i), (C_XBC + SSD_D_INNER) // gn)),
                pl.BlockSpec((1, c, gn), lambda b, i: (b, cmap(i), (C_XBC + SSD_D_INNER) // gn + 1)),
                pl.BlockSpec((1, c, LANES), lambda b, i: (b, cmap(i), 0))]

    fw = lambda i: i
    bw = lambda i: nc - 1 - i
    vec_l = pl.BlockSpec((1, LANES), lambda b, i: (0, 0))
    vec_s = pl.BlockSpec((LANES, 1), lambda b, i: (0, 0))
    st_spec = pl.BlockSpec((1, SSD_HEADS // 2, SSD_STATE, 2 * SSD_HEAD_DIM), lambda b, i: (b, 0, 0, 0))
    out_shape = [jax.ShapeDtypeStruct((bsz, seq, SSD_D_INNER), BF16)] * 2 + \
                [jax.ShapeDtypeStruct((bsz, SSD_HEADS // 2, SSD_STATE, 2 * SSD_HEAD_DIM), F32)] * 2
    return pl.pallas_call(
        _ssd_kernel,
        grid=(bsz, nc),
        in_specs=chunk_specs(fw) + chunk_specs(bw) + [vec_l, vec_l, vec_s, vec_s, st_spec, st_spec],
        out_specs=[pl.BlockSpec((1, c, SSD_D_INNER), lambda b, i: (b, i, 0)),
                   pl.BlockSpec((1, c, SSD_D_INNER), lambda b, i: (b, nc - 1 - i, 0)),
                   st_spec, st_spec],
        out_shape=out_shape,
        compiler_params=_cparams(("arbitrary", "arbitrary")),
        name="ssd_scan",
    )(xbc, xbc, xbc, small, xbc, xbc, xbc, small, alog_l, dtb_l, alog_s, dtb_s, s0f, s0b)


def _kv_prep_kernel(k_ref, v_ref, nw_ref, cos_ref, sin_ref, ko_ref, vo_ref, *, rope):
    parts = []
    for h in range(ATT_KV_HEADS):
        kh = _rms(k_ref[0, :, h * ATT_HEAD_DIM:(h + 1) * ATT_HEAD_DIM].astype(F32), nw_ref[...])
        if rope:
            kh = _rope(kh, cos_ref[...], sin_ref[...])
        parts.append(kh)
    ko_ref[0] = jnp.concatenate(parts, axis=-1).astype(BF16)
    v = v_ref[0]
    ones_blk = (lax.broadcasted_iota(jnp.int32, (v.shape[0], ATT_HEAD_DIM), 1) == 0).astype(BF16)
    vparts = []
    for h in range(ATT_KV_HEADS):
        vparts += [v[:, h * ATT_HEAD_DIM:(h + 1) * ATT_HEAD_DIM].astype(BF16), ones_blk]
    vo_ref[0] = jnp.concatenate(vparts, axis=-1)


def _kv_prep(p, k_norm, cos, sin, rope, ts):
    bsz, seq, _ = p.shape
    kb = C_AKV // ATT_KV_DIM
    tab = pl.BlockSpec((ts, ATT_HEAD_DIM), lambda b, i: (i if rope else 0, 0))
    return pl.pallas_call(
        functools.partial(_kv_prep_kernel, rope=rope),
        grid=(bsz, seq // ts),
        in_specs=[pl.BlockSpec((1, ts, ATT_KV_DIM), lambda b, i: (b, i, kb)),
                  pl.BlockSpec((1, ts, ATT_KV_DIM), lambda b, i: (b, i, kb + 1)),
                  pl.BlockSpec((1, ATT_HEAD_DIM), lambda b, i: (0, 0)), tab, tab],
        out_specs=[pl.BlockSpec((1, ts, ATT_KV_DIM), lambda b, i: (b, i, 0)),
                   pl.BlockSpec((1, ts, 2 * ATT_KV_DIM), lambda b, i: (b, i, 0))],
        out_shape=[jax.ShapeDtypeStruct((bsz, seq, ATT_KV_DIM), BF16),
                   jax.ShapeDtypeStruct((bsz, seq, 2 * ATT_KV_DIM), BF16)],
        compiler_params=_cparams(("arbitrary", "arbitrary")),
        name="kv_prep",
    )(p, p, k_norm.reshape(1, ATT_HEAD_DIM), cos, sin)


def _attn_kernel(*refs, n_seg, rope):
    q_ref, qn_ref, nw_ref, cos_ref, sin_ref, cosn_ref, sinn_ref = refs[:7]
    kv_refs = refs[7:7 + 2 * n_seg]
    o_ref = refs[7 + 2 * n_seg]
    qs_ref = refs[8 + 2 * n_seg]
    qscale = ATT_HEAD_DIM ** -0.5 * LOG2E
    heads = range(ATT_REP)

    def prepare(src_ref, c_ref, s_ref):
        for r in heads:
            qh = _rms(src_ref[0, :, r * ATT_HEAD_DIM:(r + 1) * ATT_HEAD_DIM].astype(F32), nw_ref[...])
            if rope:
                qh = _rope(qh, c_ref[...], s_ref[...])
            qs_ref[r] = (qh * qscale).astype(BF16)

    @pl.when(pl.program_id(2) == 0)
    def _():
        prepare(q_ref, cos_ref, sin_ref)

    qs = [qs_ref[r] for r in heads]
    scores = [[_mm_nt(qs[r], kv_refs[2 * s][0]) for s in range(n_seg)] for r in heads]
    prepare(qn_ref, cosn_ref, sinn_ref)
    maxes = [functools.reduce(jnp.maximum, [jnp.max(sc, axis=-1, keepdims=True) for sc in scores[r]]) for r in heads]
    accs = [functools.reduce(jnp.add, [_mm(jnp.exp2(scores[r][s] - maxes[r]), kv_refs[2 * s + 1][0])
                                        for s in range(n_seg)]) for r in heads]
    for r in heads:
        o_ref[0, :, r * ATT_HEAD_DIM:(r + 1) * ATT_HEAD_DIM] = (
            accs[r][:, :ATT_HEAD_DIM] / accs[r][:, ATT_HEAD_DIM:ATT_HEAD_DIM + 1]).astype(o_ref.dtype)


def _attention(p, q_norm, cos, sin, kv_segs, rope, tq):
    bsz, seq, _ = p.shape
    gw = ATT_REP * ATT_HEAD_DIM
    qb = C_AQ // gw
    n_seg = len(kv_segs)
    nq = seq // tq
    nxt = lambda i: jnp.minimum(i + 1, nq - 1)
    tab = lambda f: pl.BlockSpec((tq, ATT_HEAD_DIM), lambda b, g, i: (f(i) if rope else 0, 0))
    cur = lambda i: i
    in_specs = [pl.BlockSpec((1, tq, gw), lambda b, g, i: (b, i, qb + g)),
                pl.BlockSpec((1, tq, gw), lambda b, g, i: (b, nxt(i), qb + g)),
                pl.BlockSpec((1, ATT_HEAD_DIM), lambda b, g, i: (0, 0)), tab(cur), tab(cur), tab(nxt), tab(nxt)]
    args = [p, p, q_norm.reshape(1, ATT_HEAD_DIM), cos, sin, cos, sin]
    for k_arr, v_arr in kv_segs:
        lk = k_arr.shape[1]
        in_specs += [pl.BlockSpec((1, lk, ATT_HEAD_DIM), lambda b, g, i: (b, 0, g)),
                     pl.BlockSpec((1, lk, 2 * ATT_HEAD_DIM), lambda b, g, i: (b, 0, g))]
        args += [k_arr, v_arr]
    return pl.pallas_call(
        functools.partial(_attn_kernel, n_seg=n_seg, rope=rope),
        grid=(bsz, ATT_KV_HEADS, nq),
        in_specs=in_specs,
        out_specs=pl.BlockSpec((1, tq, gw), lambda b, g, i: (b, i, g)),
        out_shape=jax.ShapeDtypeStruct((bsz, seq, ATT_Q_DIM), BF16),
        scratch_shapes=[pltpu.VMEM((ATT_REP, tq, ATT_HEAD_DIM), BF16)],
        compiler_params=_cparams(("arbitrary", "arbitrary", "arbitrary")),
        name="attention",
    )(*args)


def _merge_kernel(x_ref, mod_ref, of_ref, ob_ref, gz_ref, yf_ref, yb_ref, xs_ref, sz_ref, att_ref,
                  gg_ref, gs_ref, ga_ref, gnw_ref, snw_ref, dsk_ref, wg_ref, ws_ref, wa_ref, wo_ref, o_ref):
    o = of_ref[0].astype(F32) + ob_ref[0].astype(F32)
    gz = gz_ref[0].astype(F32)
    parts = []
    for h in range(GDN_HEADS):
        sl = slice(h * GDN_DV, (h + 1) * GDN_DV)
        parts.append(_rms(o[:, sl], gnw_ref[...]) * _silu(gz[:, sl]))
    y_gdn = jnp.concatenate(parts, axis=-1)

    y = yf_ref[0].astype(F32) + yb_ref[0].astype(F32) + dsk_ref[...] * xs_ref[0].astype(F32)
    y = y * _silu(sz_ref[0].astype(F32))
    snw = snw_ref[...]
    gw = SSD_D_INNER // SSD_GROUPS
    y_ssd = jnp.concatenate([_rms(y[:, g * gw:(g + 1) * gw], snw[:, g * gw:(g + 1) * gw])
                             for g in range(SSD_GROUPS)], axis=-1)

    gate = lambda ref: _sigmoid(ref[0].astype(F32))
    m = (gate(gg_ref) * _mm(y_gdn, wg_ref[...]) + gate(gs_ref) * _mm(y_ssd, ws_ref[...])
         + gate(ga_ref) * _mm(att_ref[0], wa_ref[...]))
    g1 = mod_ref[0][2:3]
    o_ref[0] = x_ref[0] + g1 * _mm(m, wo_ref[...])


def _merge(x, mod_l, mod_row, o_f, o_b, y_f, y_b, att, p, gdn_norm, ssd_norm, ssd_d, wg, ws, wa, wo, layer, tm):
    bsz, seq, d = x.shape
    row = lambda cb: pl.BlockSpec((1, tm, d), lambda b, i: (b, i, cb))
    vec = lambda n: pl.BlockSpec((1, n), lambda b, i: (0, 0))
    wsp = pl.BlockSpec((None, d, d), lambda b, i: (layer, 0, 0), pipeline_mode=pl.Buffered(1))
    return pl.pallas_call(
        _merge_kernel,
        grid=(bsz, seq // tm),
        in_specs=[row(0), pl.BlockSpec((1, 6, d), lambda b, i: (mod_row(b), 0, 0)),
                  row(0), row(0), row(C_GZ // d), row(0), row(0), row(C_XBC // d), row(C_SZ // d), row(0),
                  row(C_GATE // d), row(C_GATE // d + 1), row(C_GATE // d + 2),
                  vec(GDN_DV), vec(d), vec(d), wsp, wsp, wsp, wsp],
        out_specs=row(0),
        out_shape=jax.ShapeDtypeStruct((bsz, seq, d), F32),
        compiler_params=_cparams(("arbitrary", "arbitrary")),
        name="merge",
    )(x, mod_l, o_f, o_b, p, y_f, y_b, p, p, att, p, p, p,
      gdn_norm.reshape(1, GDN_DV), ssd_norm.reshape(1, d),
      jnp.repeat(ssd_d, SSD_HEAD_DIM).reshape(1, d), wg, ws, wa, wo)


def _mlp_kernel(x_ref, mod_ref, nw_ref, w1_ref, w2_ref, o_ref, h_ref, acc_ref):
    k = pl.program_id(2)
    m = mod_ref[0]

    @pl.when(k == 0)
    def _():
        h_ref[...] = _norm_mod(x_ref[0], nw_ref[...], m[4:5], m[3:4]).astype(BF16)
        acc_ref[...] = jnp.zeros_like(acc_ref)

    a = jnp.maximum(jnp.dot(h_ref[...], w1_ref[...], preferred_element_type=F32), 0.0)
    acc_ref[...] += _mm(a * a, w2_ref[...])

    @pl.when(k == pl.num_programs(2) - 1)
    def _():
        o_ref[0] = x_ref[0] + m[5:6] * acc_ref[...]


def _mlp(x, mod_l, mod_row, nw, w1, w2, layer, tm, tf):
    bsz, seq, d = x.shape
    return pl.pallas_call(
        _mlp_kernel,
        grid=(bsz, seq // tm, D_FF // tf),
        in_specs=[pl.BlockSpec((1, tm, d), lambda b, i, k: (b, i, 0)),
                  pl.BlockSpec((1, 6, d), lambda b, i, k: (mod_row(b), 0, 0)),
                  pl.BlockSpec((1, d), lambda b, i, k: (0, 0)),
                  pl.BlockSpec((None, d, tf), lambda b, i, k: (layer, 0, k)),
                  pl.BlockSpec((None, tf, d), lambda b, i, k: (layer, k, 0))],
        out_specs=pl.BlockSpec((1, tm, d), lambda b, i, k: (b, i, 0)),
        out_shape=jax.ShapeDtypeStruct((bsz, seq, d), F32),
        scratch_shapes=[pltpu.VMEM((tm, d), BF16), pltpu.VMEM((tm, d), F32)],
        compiler_params=_cparams(("arbitrary", "arbitrary", "arbitrary")),
        name="mlp",
    )(x, mod_l, nw.reshape(1, d), w1, w2)


def _reorder_w_in(w_in):
    depth, d, _ = w_in.shape
    o = 0
    seg = {}
    for name, size in (("qkv", GDN_QKV), ("gz", GDN_V_DIM), ("beta", 2 * GDN_HEADS), ("a", 2 * GDN_HEADS),
                       ("sz", SSD_D_INNER), ("xbc", SSD_XBC), ("dt", 2 * SSD_HEADS),
                       ("aq", ATT_Q_DIM), ("akv", 2 * ATT_KV_DIM), ("gate", 3 * D_MODEL)):
        seg[name] = w_in[:, :, o:o + size]
        o += size
    pad = jnp.zeros((depth, d, NP_COLS - (C_SMALL + SM_DT + 2 * SSD_HEADS)), w_in.dtype)
    out = jnp.concatenate([seg["qkv"], seg["xbc"], seg["akv"], seg["gz"], seg["sz"], seg["aq"], seg["gate"],
                           seg["beta"], seg["a"], seg["dt"], pad], axis=-1)
    return out.astype(BF16)


def _rope_tables(seq):
    t = jnp.arange(seq, dtype=jnp.int32)
    q = ATT_HEAD_DIM // 4
    freqs = ROPE_THETA ** (-jnp.arange(q, dtype=F32) / q)
    ang_r = (t // GRID_W).astype(F32)[:, None] * freqs[None, :]
    ang_c = (t % GRID_W).astype(F32)[:, None] * freqs[None, :]
    cos = jnp.concatenate([jnp.cos(ang_r)] * 2 + [jnp.cos(ang_c)] * 2, axis=-1)
    sin = jnp.concatenate([-jnp.sin(ang_r), jnp.sin(ang_r), -jnp.sin(ang_c), jnp.sin(ang_c)], axis=-1)
    return cos, sin


TM_PROJ = 1024
TM_MLP = 1024
TF_MLP = 1024
TM_MERGE = 512
TQ_ATTN = 256
TS_PREP = 512


def _tile(seq, want):
    return min(seq, want)


def kernel(x, c, ctx, c_ctx, w_mod, b_mod, norm_mix, norm_mlp, w_in, gdn_conv, gdn_a_log, gdn_dt_bias, gdn_norm,
           ssd_conv_w, ssd_conv_b, ssd_a_log, ssd_dt_bias, ssd_d, ssd_norm, att_q_norm, att_k_norm,
           w_br_gdn, w_br_ssd, w_br_att, w_out, w_ff1, w_ff2):
    bsz, seq, d = x.shape
    ctx_len = ctx.shape[1]
    depth = w_in.shape[0]
    assert bsz < MOD_ROWS and d == D_MODEL

    cc = jnp.zeros((MOD_ROWS, d), F32).at[:bsz].set(c).at[bsz].set(c_ctx)
    mod = _modulation(cc, w_mod, b_mod).reshape(depth, MOD_ROWS, 6, d)
    lat_row = lambda b: b
    ctx_row = lambda b: bsz

    w_in_r = _reorder_w_in(w_in)
    wg, ws, wa, wo = (w.astype(BF16) for w in (w_br_gdn, w_br_ssd, w_br_att, w_out))
    w1, w2 = w_ff1.astype(BF16), w_ff2.astype(BF16)
    cos, sin = _rope_tables(seq)

    zg = jnp.zeros((bsz, GDN_PAIRS, GDN_DK, 2 * GDN_DV), F32)
    zs = jnp.zeros((bsz, SSD_HEADS // 2, SSD_STATE, 2 * SSD_HEAD_DIM), F32)
    conv_w = jnp.concatenate([gdn_conv, ssd_conv_w], axis=-1)
    conv_b = jnp.concatenate([jnp.zeros((depth, 1, GDN_QKV), F32), ssd_conv_b[:, None, :]], axis=-1)
    xc = ctx
    for l in range(depth):
        last = l == depth - 1
        p_lat, sm_lat = _in_projection(x, mod[l], lat_row, norm_mix[l], w_in_r, conv_w[l], conv_b[l], l,
                                       _tile(seq, TM_PROJ))
        p_ctx, sm_ctx = _in_projection(xc, mod[l], ctx_row, norm_mix[l], w_in_r, conv_w[l], conv_b[l], l,
                                       _tile(ctx_len, TM_PROJ))

        ogf_c, ogb_c, sgf, sgb = _gdn_scan(p_ctx, sm_ctx, gdn_a_log[l], gdn_dt_bias[l], zg, zg)
        ogf_l, ogb_l, _, _ = _gdn_scan(p_lat, sm_lat, gdn_a_log[l], gdn_dt_bias[l], sgf, sgb)
        ysf_c, ysb_c, ssf, ssb = _ssd_scan(p_ctx, sm_ctx, ssd_a_log[l], ssd_dt_bias[l], zs, zs)
        ysf_l, ysb_l, _, _ = _ssd_scan(p_lat, sm_lat, ssd_a_log[l], ssd_dt_bias[l], ssf, ssb)

        k_c, v_c = _kv_prep(p_ctx, att_k_norm[l], cos, sin, False, _tile(ctx_len, TS_PREP))
        k_l, v_l = _kv_prep(p_lat, att_k_norm[l], cos, sin, True, _tile(seq, TS_PREP))
        att_l = _attention(p_lat, att_q_norm[l], cos, sin, [(k_c, v_c), (k_l, v_l)], True, _tile(seq, TQ_ATTN))

        x = _merge(x, mod[l], lat_row, ogf_l, ogb_l, ysf_l, ysb_l, att_l, p_lat,
                   gdn_norm[l], ssd_norm[l], ssd_d[l], wg, ws, wa, wo, l, _tile(seq, TM_MERGE))
        x = _mlp(x, mod[l], lat_row, norm_mlp[l], w1, w2, l, _tile(seq, TM_MLP), TF_MLP)

        if not last:
            att_c = _attention(p_ctx, att_q_norm[l], cos, sin, [(k_c, v_c)], False, _tile(ctx_len, TQ_ATTN))
            xc = _merge(xc, mod[l], ctx_row, ogf_c, ogb_c, ysf_c, ysb_c, att_c, p_ctx,
                        gdn_norm[l], ssd_norm[l], ssd_d[l], wg, ws, wa, wo, l, _tile(ctx_len, TM_MERGE))
            xc = _mlp(xc, mod[l], ctx_row, norm_mlp[l], w1, w2, l, _tile(ctx_len, TM_MLP), TF_MLP)
    return x
```
